```python
import math
import jax
import jax.numpy as jnp
from jax import lax
import numpy as np

D_MODEL = 1024
BATCH = 2
SEQ = 16384
DEPTH = 2

PLE_DIM = 256
DN_ALPHA = (2.0 * DEPTH) ** 0.25
DN_BETA = (8.0 * DEPTH) ** -0.25
LN_EPS = 1e-5
HEAD_NORM_EPS = 1e-6
N_EVEN = (DEPTH + 1) // 2
N_ODD = DEPTH // 2

MOBA_HEADS = 8
MOBA_HEAD_DIM = 64
MOBA_BLOCK = 256
MOBA_TOPK = 3
MOBA_Q_BLOCK = 128
REL_BUCKETS = 32
REL_MAX_DIST = 128
GLA_HEADS = 4
GLA_DK = 64
GLA_DV = 128
GLA_GATE_RANK = 16
GLA_TAU = 16.0
HGRN_HEADS = 4
HGRN_DK = 64
HGRN_DV = 128
RET_HEADS = 4
RET_DK = 64
RET_DV = 128
ROPE_BASE = 10000.0
CHUNK = 64
FFN_DENSE = 2752
N_EXPERTS = 8
TOP_K = 2
FFN_EXPERT = 3584
MOE_BLOCK = 128

MOBA_W = MOBA_HEADS * MOBA_HEAD_DIM
EVEN_SPLITS = (MOBA_W, MOBA_W, MOBA_W,
               GLA_HEADS * GLA_DK, GLA_HEADS * GLA_DK, GLA_HEADS * GLA_DV,
               GLA_GATE_RANK, GLA_HEADS * GLA_DV)
ODD_SPLITS = (HGRN_HEADS * HGRN_DK, HGRN_HEADS * HGRN_DK, HGRN_HEADS * HGRN_DV, HGRN_HEADS * HGRN_DV,
              RET_HEADS * RET_DK, RET_HEADS * RET_DK, RET_HEADS * RET_DV, RET_HEADS * RET_DV)
EVEN_IN = sum(EVEN_SPLITS)
ODD_IN = sum(ODD_SPLITS)
EVEN_OUT = MOBA_W + GLA_HEADS * GLA_DV
ODD_OUT = HGRN_HEADS * HGRN_DV + RET_HEADS * RET_DV

kernel_name = 'hybrid_moba_gla_hgrn2_retnet_moe'


def split_cols(h, sizes):
    return jnp.split(h, [int(c) for c in np.cumsum(sizes)[:-1]], axis=-1)


def split_heads(t, n_heads):
    b, s, w = t.shape
    return t.reshape(b, s, n_heads, w // n_heads).transpose(0, 2, 1, 3)


def merge_heads(o):
    b, h, s, d = o.shape
    return o.transpose(0, 2, 1, 3).reshape(b, s, h * d)


def layer_norm(x, g, b):
    xf = x.astype(jnp.float32)
    xc = xf - jnp.mean(xf, axis=-1, keepdims=True)
    y = xc * lax.rsqrt(jnp.mean(xc * xc, axis=-1, keepdims=True) + LN_EPS)
    return (y * g.astype(jnp.float32) + b.astype(jnp.float32)).astype(x.dtype)


def head_rms_norm(o, g):
    o = o * lax.rsqrt(jnp.mean(o * o, axis=-1, keepdims=True) + HEAD_NORM_EPS)
    return merge_heads(o) * g.astype(jnp.float32)


def head_layer_norm(o, g):
    o = o - jnp.mean(o, axis=-1, keepdims=True)
    o = o * lax.rsqrt(jnp.mean(o * o, axis=-1, keepdims=True) + HEAD_NORM_EPS)
    return merge_heads(o) * g.astype(jnp.float32)


def t5_bucket(dist):
    n = jnp.maximum(dist, 0)
    max_exact = REL_BUCKETS // 2
    nf = jnp.maximum(n, 1).astype(jnp.float32)
    large = max_exact + (jnp.log(nf / max_exact) / math.log(REL_MAX_DIST / max_exact)
                         * (REL_BUCKETS - max_exact)).astype(jnp.int32)
    large = jnp.minimum(large, REL_BUCKETS - 1)
    return jnp.where(n < max_exact, n, large)


def moba_attention(q, k, v, rel_bias):
    B, H, S, dh = q.shape
    n_blk = max(-(-S // MOBA_BLOCK), MOBA_TOPK)
    pad = n_blk * MOBA_BLOCK - S
    kf = jnp.pad(k.astype(jnp.float32), ((0, 0), (0, 0), (0, pad), (0, 0)))
    vf = jnp.pad(v.astype(jnp.float32), ((0, 0), (0, 0), (0, pad), (0, 0)))
    kb = kf.reshape(B, H, n_blk, MOBA_BLOCK, dh)
    vb = vf.reshape(B, H, n_blk, MOBA_BLOCK, dh)
    k_mean = jnp.mean(kb, axis=3)
    tbl = rel_bias.astype(jnp.float32).T
    scale = dh ** -0.5
    n_q = S // MOBA_Q_BLOCK
    qs = q.astype(jnp.float32).reshape(B, H, n_q, MOBA_Q_BLOCK, dh).transpose(2, 0, 1, 3, 4)
    b_idx = jnp.arange(B)[:, None, None, None]
    h_idx = jnp.arange(H)[None, :, None, None]
    blk_ids = jnp.arange(n_blk)
    offs = jnp.arange(MOBA_BLOCK)
    rank_ids = jnp.arange(MOBA_TOPK)
    sel_w = MOBA_TOPK * MOBA_BLOCK

    def attend(args):
        qb, ci = args
        q0 = ci * MOBA_Q_BLOCK
        q_pos = q0 + jnp.arange(MOBA_Q_BLOCK)
        cur = q0 // MOBA_BLOCK
        gate = jnp.einsum('bhqd,bhnd->bhqn', qb, k_mean)
        gate = jnp.where(blk_ids < cur, gate, -jnp.inf)
        _, sel = lax.top_k(gate, MOBA_TOPK)
        valid = rank_ids < cur
        ks = kb[b_idx, h_idx, sel]
        vs = vb[b_idx, h_idx, sel]
        k_pos = sel[..., None] * MOBA_BLOCK + offs
        dist = q_pos[:, None, None] - k_pos
        s_sel = (jnp.einsum('bhqd,bhqnkd->bhqnk', qb, ks) * scale
                 + tbl[h_idx[..., None], t5_bucket(dist)])
        s_sel = jnp.where(valid[:, None], s_sel, -jnp.inf)
        k_own = lax.dynamic_index_in_dim(kb, cur, axis=2, keepdims=False)
        v_own = lax.dynamic_index_in_dim(vb, cur, axis=2, keepdims=False)
        dist_own = q_pos[:, None] - (cur * MOBA_BLOCK + offs)[None, :]
        s_own = jnp.einsum('bhqd,bhkd->bhqk', qb, k_own) * scale + tbl[:, t5_bucket(dist_own)]
        s_own = jnp.where(dist_own >= 0, s_own, -jnp.inf)
        logits = jnp.concatenate([s_sel.reshape(B, H, MOBA_Q_BLOCK, sel_w), s_own], axis=-1)
        probs = jax.nn.softmax(logits, axis=-1)
        p_sel = probs[..., :sel_w].reshape(B, H, MOBA_Q_BLOCK, MOBA_TOPK, MOBA_BLOCK)
        p_own = probs[..., sel_w:]
        return (jnp.einsum('bhqnk,bhqnkd->bhqd', p_sel, vs)
                + jnp.einsum('bhqk,bhkd->bhqd', p_own, v_own))

    out = lax.map(attend, (qs, jnp.arange(n_q)))
    return out.transpose(1, 2, 0, 3, 4).reshape(B, H, S, dh)


def chunked_gated_linear(q, k, v, log_a):
    B, H, S, dk = q.shape
    dv = v.shape[-1]
    n = S // CHUNK

    def to_chunks(t):
        return t.astype(jnp.float32).reshape(B, H, n, CHUNK, t.shape[-1]).transpose(2, 0, 1, 3, 4)

    qc, kc, vc, ac = to_chunks(q), to_chunks(k), to_chunks(v), to_chunks(log_a)
    bc = jnp.cumsum(ac, axis=-2)
    causal = jnp.tril(jnp.ones((CHUNK, CHUNK), dtype=bool))

    def step(state, inp):
        qb, kb, vb, bb = inp
        o_inter = jnp.einsum('bhtd,bhde->bhte', qb * jnp.exp(bb), state)
        diff = bb[:, :, :, None, :] - bb[:, :, None, :, :]
        decay = jnp.exp(jnp.where(causal[:, :, None], diff, -jnp.inf))
        scores = jnp.einsum('bhtd,bhsd,bhtsd->bhts', qb, kb, decay)
        o_intra = jnp.einsum('bhts,bhse->bhte', scores, vb)
        b_last = bb[:, :, -1:, :]
        new_state = (jnp.exp(b_last[:, :, 0, :])[..., None] * state
                     + jnp.einsum('bhsd,bhse->bhde', kb * jnp.exp(b_last - bb), vb))
        return new_state, o_inter + o_intra

    init = jnp.zeros((B, H, dk, dv), jnp.float32)
    _, out = lax.scan(step, init, (qc, kc, vc, bc))
    return out.transpose(1, 2, 0, 3, 4).reshape(B, H, S, dv)


def rotary(x):
    S, d = x.shape[2], x.shape[3]
    half = d // 2
    inv = ROPE_BASE ** (-jnp.arange(half, dtype=jnp.float32) / half)
    ang = jnp.arange(S, dtype=jnp.float32)[:, None] * inv[None, :]
    cos, sin = jnp.cos(ang), jnp.sin(ang)
    xf = x.astype(jnp.float32)
    x1, x2 = xf[..., :half], xf[..., half:]
    return jnp.concatenate([x1 * cos - x2 * sin, x1 * sin + x2 * cos], axis=-1)


def retention_chunkwise(q, k, v):
    B, H, S, dk = q.shape
    dv = v.shape[-1]
    n = S // CHUNK
    log_g = jnp.log1p(-jnp.exp2(-5.0 - jnp.arange(H, dtype=jnp.float32)))
    qc = q.astype(jnp.float32).reshape(B, H, n, CHUNK, dk)
    kc = k.astype(jnp.float32).reshape(B, H, n, CHUNK, dk)
    vc = v.astype(jnp.float32).reshape(B, H, n, CHUNK, dv)
    pos = jnp.arange(CHUNK, dtype=jnp.float32)
    rel = pos[:, None] - pos[None, :]
    dmask = jnp.where(rel >= 0, jnp.exp(jnp.maximum(rel, 0.0)[None] * log_g[:, None, None]), 0.0)
    scores = jnp.einsum('bhnid,bhnjd->bhnij', qc, kc) * dmask[:, None]
    o_intra = jnp.einsum('bhnij,bhnje->bhnie', scores, vc)
    zeta = jnp.exp((CHUNK - 1 - pos)[None, :] * log_g[:, None])
    u = jnp.einsum('bhnjd,bhnje->nbhde', kc * zeta[:, None, :, None], vc)
    g_chunk = jnp.exp(CHUNK * log_g)[:, None, None]

    def step(state, u_n):
        return g_chunk * state + u_n, state

    _, s_prev = lax.scan(step, jnp.zeros((B, H, dk, dv), jnp.float32), u)
    xi = jnp.exp((pos + 1.0)[None, :] * log_g[:, None])
    o_cross = jnp.einsum('bhnid,nbhde->bhnie', qc * xi[:, None, :, None], s_prev)
    return (o_intra + o_cross).reshape(B, H, S, dv)


def even_mixer(x, w_in, gla_w_gate, gla_b_gate, gla_norm, w_out, rel_bias):
    h = x @ w_in
    mq, mk, mv, gq, gk, gv, ga, gr = split_cols(h, EVEN_SPLITS)
    o_a = merge_heads(moba_attention(split_heads(mq, MOBA_HEADS), split_heads(mk, MOBA_HEADS),
                                     split_heads(mv, MOBA_HEADS), rel_bias))
    log_a = jax.nn.log_sigmoid((ga @ gla_w_gate + gla_b_gate).astype(jnp.float32)) / GLA_TAU
    o_b = chunked_gated_linear(split_heads(gq, GLA_HEADS).astype(jnp.float32) * GLA_DK ** -0.5,
                               split_heads(gk, GLA_HEADS), split_heads(gv, GLA_HEADS),
                               split_heads(log_a, GLA_HEADS))
    o_b = head_rms_norm(o_b, gla_norm) * jax.nn.silu(gr.astype(jnp.float32))
    return jnp.concatenate([o_a, o_b], axis=-1).astype(x.dtype) @ w_out


def odd_mixer(x, w_in, lb, hgrn_norm, ret_norm, w_out):
    h = x @ w_in
    hq, hf, hi, hg, rq, rk, rv, rg = split_cols(h, ODD_SPLITS)
    f = lb + (1.0 - lb) * jax.nn.sigmoid(hf.astype(jnp.float32))
    o_c = chunked_gated_linear(split_heads(hq, HGRN_HEADS), split_heads(1.0 - f, HGRN_HEADS),
                               split_heads(hi, HGRN_HEADS), split_heads(jnp.log(f), HGRN_HEADS))
    o_c = head_rms_norm(o_c, hgrn_norm) * jax.nn.silu(hg.astype(jnp.float32))
    q = rotary(split_heads(rq, RET_HEADS))
    k = rotary(split_heads(rk, RET_HEADS)) * RET_DK ** -0.5
    o_d = retention_chunkwise(q, k, split_heads(rv, RET_HEADS))
    o_d = head_layer_norm(o_d, ret_norm) * jax.nn.silu(rg.astype(jnp.float32))
    return jnp.concatenate([o_c, o_d], axis=-1).astype(x.dtype) @ w_out


def swiglu(x, w1, w3, w2):
    return (jax.nn.silu(x @ w1) * (x @ w3)) @ w2


def moe_swiglu(x, w_router, b_router, w1, w3, w2):
    B, S, D = x.shape
    n_tok = B * S
    n_asg = n_tok * TOP_K
    xf = x.reshape(n_tok, D)
    logits = (xf @ w_router).astype(jnp.float32) + b_router.astype(jnp.float32)
    top_logit, top_e = lax.top_k(logits, TOP_K)
    gates = jax.nn.softmax(top_logit, axis=-1)
    e_flat = top_e.reshape(-1).astype(jnp.int32)
    tok_flat = jnp.repeat(jnp.arange(n_tok, dtype=jnp.int32), TOP_K)
    g_flat = gates.reshape(-1)
    order = jnp.argsort(e_flat)
    e_s, tok_s, g_s = e_flat[order], tok_flat[order], g_flat[order]
    counts = jax.ops.segment_sum(jnp.ones_like(e_flat), e_flat, num_segments=N_EXPERTS)
    starts = jnp.cumsum(counts) - counts
    padded = (counts + MOE_BLOCK - 1) // MOE_BLOCK * MOE_BLOCK
    p_ends = jnp.cumsum(padded)
    p_starts = p_ends - padded
    dest = p_starts[e_s] + jnp.arange(n_asg, dtype=jnp.int32) - starts[e_s]
    n_rows = n_asg + N_EXPERTS * MOE_BLOCK
    tok_buf = jnp.zeros((n_rows,), jnp.int32).at[dest].set(tok_s)
    g_buf = jnp.zeros((n_rows,), jnp.float32).at[dest].set(g_s)
    n_blocks = n_rows // MOE_BLOCK
    blk_start = jnp.arange(n_blocks, dtype=jnp.int32) * MOE_BLOCK
    blk_e = jnp.minimum(jnp.searchsorted(p_ends, blk_start, side='right'), N_EXPERTS - 1)

    def expert_block(args):
        toks, e = args
        xb = xf[toks]
        return (jax.nn.silu(xb @ w1[e]) * (xb @ w3[e])) @ w2[e]

    y = lax.map(expert_block, (tok_buf.reshape(n_blocks, MOE_BLOCK), blk_e))
    y = y.reshape(n_rows, D).astype(jnp.float32) * g_buf[:, None]
    out = jnp.zeros((n_tok, D), jnp.float32).at[tok_buf].add(y)
    return out.reshape(B, S, D).astype(x.dtype)


def setup_inputs(seed: int = 0) -> dict:
    key = jax.random.key(seed)
    ks = iter(jax.random.split(key, 32))

    def nrm(shape, scale):
        return jax.random.normal(next(ks), shape, jnp.float32) * scale

    D = D_MODEL
    return {
        'x': nrm((BATCH, SEQ, D), 1.0),
        'p': nrm((DEPTH, BATCH, SEQ, PLE_DIM), 1.0),
        'rel_bias': nrm((REL_BUCKETS, MOBA_HEADS), 0.5),
        'even_w_in': nrm((N_EVEN, D, EVEN_IN), D ** -0.5),
        'gla_w_gate': nrm((N_EVEN, GLA_GATE_RANK, GLA_HEADS * GLA_DK), GLA_GATE_RANK ** -0.5),
        'gla_b_gate': nrm((N_EVEN, GLA_HEADS * GLA_DK), 0.1),
        'gla_norm': 1.0 + nrm((N_EVEN, GLA_HEADS * GLA_DV), 0.02),
        'even_w_out': nrm((N_EVEN, EVEN_OUT, D), EVEN_OUT ** -0.5 * DN_BETA),
        'odd_w_in': nrm((N_ODD, D, ODD_IN), D ** -0.5),
        'hgrn_gamma': nrm((DEPTH, HGRN_HEADS * HGRN_DK), 1.0),
        'hgrn_norm': 1.0 + nrm((N_ODD, HGRN_HEADS * HGRN_DV), 0.02),
        'ret_norm': 1.0 + nrm((N_ODD, RET_HEADS * RET_DV), 0.02),
        'odd_w_out': nrm((N_ODD, ODD_OUT, D), ODD_OUT ** -0.5 * DN_BETA),
        'ln_mix_g': 1.0 + nrm((DEPTH, D), 0.02),
        'ln_mix_b': nrm((DEPTH, D), 0.02),
        'ln_ffn_g': 1.0 + nrm((DEPTH, D), 0.02),
        'ln_ffn_b': nrm((DEPTH, D), 0.02),
        'ffn_w1': nrm((N_EVEN, D, FFN_DENSE), D ** -0.5),
        'ffn_w3': nrm((N_EVEN, D, FFN_DENSE), D ** -0.5),
        'ffn_w2': nrm((N_EVEN, FFN_DENSE, D), FFN_DENSE ** -0.5 * DN_BETA),
        'router_w': nrm((N_ODD, D, N_EXPERTS), D ** -0.5),
        'router_b': nrm((N_ODD, N_EXPERTS), 0.01),
        'expert_w1': nrm((N_ODD, N_EXPERTS, D, FFN_EXPERT), D ** -0.5),
        'expert_w3': nrm((N_ODD, N_EXPERTS, D, FFN_EXPERT), D ** -0.5),
        'expert_w2': nrm((N_ODD, N_EXPERTS, FFN_EXPERT, D), FFN_EXPERT ** -0.5 * DN_BETA),
        'ple_w_gate': nrm((DEPTH, D, D), D ** -0.5),
        'ple_w_proj': nrm((DEPTH, PLE_DIM, D), PLE_DIM ** -0.5),
    }


def reference(x, p, rel_bias, even_w_in, gla_w_gate, gla_b_gate, gla_norm, even_w_out,
              odd_w_in, hgrn_gamma, hgrn_norm, ret_norm, odd_w_out,
              ln_mix_g, ln_mix_b, ln_ffn_g, ln_ffn_b, ffn_w1, ffn_w3, ffn_w2,
              router_w, router_b, expert_w1, expert_w3, expert_w2, ple_w_gate, ple_w_proj):
    lb_p = jax.nn.softmax(hgrn_gamma.astype(jnp.float32), axis=0)
    lb_all = jnp.cumsum(lb_p, axis=0) - lb_p[0]
    for i in range(DEPTH):
        j = i // 2
        if i % 2 == 0:
            mix = even_mixer(x, even_w_in[j], gla_w_gate[j], gla_b_gate[j], gla_norm[j],
                             even_w_out[j], rel_bias)
        else:
            mix = odd_mixer(x, odd_w_in[j], lb_all[i], hgrn_norm[j], ret_norm[j], odd_w_out[j])
        x = layer_norm(DN_ALPHA * x + mix, ln_mix_g[i], ln_mix_b[i])
        if i % 2 == 0:
            ffn = swiglu(x, ffn_w1[j], ffn_w3[j], ffn_w2[j])
        else:
            ffn = moe_swiglu(x, router_w[j], router_b[j], expert_w1[j], expert_w3[j], expert_w2[j])
        x = layer_norm(DN_ALPHA * x + ffn, ln_ffn_g[i], ln_ffn_b[i])
        x = x + jax.nn.sigmoid(x @ ple_w_gate[i]) * (p[i] @ ple_w_proj[i])
    return x
```

```python
import functools
import math

import numpy as np
import jax
import jax.numpy as jnp
from jax import lax
from jax.experimental import pallas as pl
from jax.experimental.pallas import tpu as pltpu

D_MODEL = 1024
PLE_DIM = 256
LN_EPS = 1e-5
HEAD_NORM_EPS = 1e-6
MOBA_HEADS, MOBA_DH, MOBA_BLOCK, MOBA_TOPK = 8, 64, 256, 3
REL_BUCKETS, REL_MAX_DIST = 32, 128
GLA_HEADS, GLA_DK, GLA_DV, GLA_RANK, GLA_TAU = 4, 64, 128, 16, 16.0
HGRN_HEADS = 4
RET_HEADS = 4
ROPE_BASE = 10000.0
CHUNK = 64
FFN_DENSE = 2752
N_EXPERTS, TOP_K, FFN_EXPERT = 8, 2, 3584

LANES = 128
VMEM_LIMIT = 56 * 1024 * 1024
ROW_TILE = 512
SEQ_TILE = 512
MOE_ROW_TILE = 512
MOE_F_TILE = 512
FFN_F_PAD = 2816
FFN_F_TILE = 1408
NEG = -1e30

BF16 = jnp.bfloat16
F32 = jnp.float32
NT_DIMS = (((1,), (1,)), ((), ()))
TN_DIMS = (((0,), (0,)), ((), ()))


def _cparams(sem):
    return pltpu.CompilerParams(dimension_semantics=sem, vmem_limit_bytes=VMEM_LIMIT)


def _sigmoid(x):
    return 1.0 / (1.0 + jnp.exp(-x))


def _silu(x):
    return x * _sigmoid(x)


def _layer_norm(y, g, b):
    yc = y - jnp.mean(y, axis=-1, keepdims=True)
    return yc * lax.rsqrt(jnp.mean(yc * yc, axis=-1, keepdims=True) + LN_EPS) * g + b


def _proj_kernel(x_ref, w_ref, o_ref):
    o_ref[...] = jnp.dot(x_ref[...].astype(BF16), w_ref[...],
                         preferred_element_type=F32).astype(o_ref.dtype)


def _proj(x, w, out_dtype):
    m, d = x.shape
    n = w.shape[1]
    tm = min(ROW_TILE, m)
    return pl.pallas_call(
        _proj_kernel,
        name="in_proj",
        grid=(m // tm,),
        in_specs=[pl.BlockSpec((tm, d), lambda i: (i, 0)),
                  pl.BlockSpec((d, n), lambda i: (0, 0))],
        out_specs=pl.BlockSpec((tm, n), lambda i: (i, 0)),
        out_shape=jax.ShapeDtypeStruct((m, n), out_dtype),
        compiler_params=_cparams(("parallel",)),
    )(x, w)


def _moba_kernel(c_far_ref, q_ref, k_ref, v_ref, bias_ref, o_ref, kmean_ref, *, nb_pad):
    hp = pl.program_id(1)
    qi = pl.program_id(2)
    tq = MOBA_BLOCK

    @pl.when(qi == 0)
    def _():
        kmean_ref[...] = jnp.zeros_like(kmean_ref)

    own = pl.ds(pl.multiple_of(qi * tq, tq), tq)
    k_own = k_ref[own, :]
    v_own = v_ref[own, :]
    q = q_ref[...]
    kmean = kmean_ref[...]
    kmean_ref[pl.ds(qi, 1), :] = jnp.mean(k_own.astype(F32), axis=0, keepdims=True)

    lane = lax.broadcasted_iota(jnp.int32, (tq, LANES), 1)
    blk = lax.broadcasted_iota(jnp.int32, (tq, nb_pad), 1)
    blk_f = blk.astype(F32)
    outs = []
    for hd in range(2):
        in_head = (lane >= MOBA_DH * hd) & (lane < MOBA_DH * (hd + 1))
        qh = jnp.where(in_head, q, jnp.zeros_like(q))
        gate = lax.dot_general(qh.astype(F32), kmean, NT_DIMS, preferred_element_type=F32,
                               precision=lax.Precision.HIGHEST)
        g = jnp.where(blk < qi, gate, -jnp.inf)
        sel = jnp.zeros((tq, nb_pad), jnp.bool_)
        for _ in range(MOBA_TOPK):
            gmax = jnp.max(g, axis=1, keepdims=True)
            idx = jnp.min(jnp.where(g == gmax, blk_f, float(nb_pad)), axis=1, keepdims=True)
            pick = (blk_f == idx) & (gmax > -jnp.inf)
            sel = sel | pick
            g = jnp.where(pick, -jnp.inf, g)
        negmask = jnp.where(sel, 0.0, NEG).astype(BF16)
        qs = qh * jnp.asarray(MOBA_DH ** -0.5, BF16)
        qa = jnp.concatenate([qs, negmask], axis=1)

        s = lax.dot_general(qs, k_own, NT_DIMS, preferred_element_type=F32) + bias_ref[hd, 0]
        m0 = jnp.max(s, axis=1, keepdims=True)
        p = jnp.exp(s - m0)
        l0 = jnp.sum(p, axis=1, keepdims=True)
        acc0 = jnp.dot(p.astype(BF16), v_own, preferred_element_type=F32)

        def past_block(j, carry, bias):
            m, l, acc = carry
            rows = pl.ds(pl.multiple_of(j * tq, tq), tq)
            onehot = jnp.where(blk == j, 1.0, 0.0).astype(BF16)
            ka = jnp.concatenate([k_ref[rows, :], onehot], axis=1)
            sj = lax.dot_general(qa, ka, NT_DIMS, preferred_element_type=F32) + bias
            m_new = jnp.maximum(m, jnp.max(sj, axis=1, keepdims=True))
            alpha = jnp.exp(m - m_new)
            pj = jnp.exp(sj - m_new)
            l = alpha * l + jnp.sum(pj, axis=1, keepdims=True)
            acc = alpha * acc + jnp.dot(pj.astype(BF16), v_ref[rows, :], preferred_element_type=F32)
            return m_new, l, acc

        c_far = c_far_ref[2 * hp + hd]
        carry = lax.fori_loop(0, jnp.maximum(qi - 1, 0),
                              lambda j, c: past_block(j, c, c_far), (m0, l0, acc0))
        carry = lax.cond(qi >= 1,
                         lambda c: past_block(qi - 1, c, bias_ref[hd, 1]),
                         lambda c: c, carry)
        _, l, acc = carry
        outs.append(acc / l)
    o_ref[...] = jnp.where(lane < MOBA_DH, outs[0], outs[1]).astype(o_ref.dtype)


def _t5_bucket(dist):
    n = jnp.maximum(dist, 0)
    max_exact = REL_BUCKETS // 2
    nf = jnp.maximum(n, 1).astype(F32)
    large = max_exact + (jnp.log(nf / max_exact) / math.log(REL_MAX_DIST / max_exact)
                         * (REL_BUCKETS - max_exact)).astype(jnp.int32)
    large = jnp.minimum(large, REL_BUCKETS - 1)
    return jnp.where(n < max_exact, n, large)


def _moba_bias_tables(rel_bias):
    tbl = rel_bias.astype(F32).T
    qpos = jnp.arange(MOBA_BLOCK)[:, None]
    kpos = jnp.arange(MOBA_BLOCK)[None, :]
    d_own = qpos - kpos
    own = jnp.where(d_own >= 0, tbl[:, _t5_bucket(d_own)], NEG)
    adj = tbl[:, _t5_bucket(d_own + MOBA_BLOCK)]
    return jnp.stack([own, adj], axis=1), tbl[:, REL_BUCKETS - 1]


def _moba(hb, rel_bias, batch, seq):
    m = hb.shape[0]
    nq = seq // MOBA_BLOCK
    nb_pad = LANES * (-(-nq // LANES))
    bias, c_far = _moba_bias_tables(rel_bias)
    hp_n = MOBA_HEADS // 2
    kern = functools.partial(_moba_kernel, nb_pad=nb_pad)
    return pl.pallas_call(
        kern,
        name="moba",
        grid=(batch, hp_n, nq),
        in_specs=[pl.BlockSpec(memory_space=pltpu.SMEM),
                  pl.BlockSpec((MOBA_BLOCK, LANES), lambda b, h, i: (b * nq + i, h)),
                  pl.BlockSpec((seq, LANES), lambda b, h, i: (b, hp_n + h)),
                  pl.BlockSpec((seq, LANES), lambda b, h, i: (b, 2 * hp_n + h)),
                  pl.BlockSpec((2, 2, MOBA_BLOCK, MOBA_BLOCK), lambda b, h, i: (h, 0, 0, 0))],
        out_specs=pl.BlockSpec((MOBA_BLOCK, LANES), lambda b, h, i: (b * nq + i, h)),
        out_shape=jax.ShapeDtypeStruct((m, MOBA_HEADS * MOBA_DH), BF16),
        scratch_shapes=[pltpu.VMEM((nb_pad, LANES), F32)],
        compiler_params=_cparams(("parallel", "parallel", "arbitrary")),
    )(c_far, hb, hb, hb, bias)


LEVELS = (32, 16, 8, 4, 2, 1)


def _chunk_cumsum(la):
    t = la.shape[0]
    row = lax.broadcasted_iota(jnp.int32, la.shape, 0) % CHUNK
    b = la
    sh = 1
    while sh < CHUNK:
        b = b + jnp.where(row >= sh, pltpu.roll(b, sh, 0), 0.0)
        sh *= 2
    return b


def _level_reference(b, h):
    t, w = b.shape
    if 2 * h >= 8:
        b3 = b.reshape(t // (2 * h), 2 * h, w)
        return jnp.broadcast_to(b3[:, h - 1:h, :], b3.shape).reshape(t, w)
    row = lax.broadcasted_iota(jnp.int32, b.shape, 0)
    if h == 2:
        r4 = row % 4
        return jnp.where(r4 == 0, pltpu.roll(b, t - 1, 0),
                         jnp.where(r4 == 1, b,
                                   jnp.where(r4 == 2, pltpu.roll(b, 1, 0), pltpu.roll(b, 2, 0))))
    return jnp.where(row % 2 == 1, pltpu.roll(b, 1, 0), b)


def _pair_masks():
    t = lax.broadcasted_iota(jnp.int32, (CHUNK, 2 * CHUNK), 0)
    s = lax.broadcasted_iota(jnp.int32, (CHUNK, 2 * CHUNK), 1) % CHUNK
    masks = []
    for h in LEVELS:
        masks.append((t // (2 * h) == s // (2 * h)) & ((t // h) % 2 == 1) & ((s // h) % 2 == 0))
    masks.append(t == s)
    return masks


def _stack_heads(x, split):
    lane = lax.broadcasted_iota(jnp.int32, x.shape, 1)
    z = jnp.zeros_like(x)
    return jnp.concatenate([jnp.where(lane < split, x, z), jnp.where(lane >= split, x, z)], axis=0)


def _gated_linear_tile(q, k, v, la, st_ref):
    t = q.shape[0]
    b = _chunk_cumsum(la)
    q_lv, k_lv = [], []
    for h in LEVELS:
        w = jnp.exp(-jnp.abs(b - _level_reference(b, h)))
        q_lv.append((q * w).astype(BF16))
        k_lv.append((k * w).astype(BF16))
    q_lv.append(q.astype(BF16))
    k_lv.append(k.astype(BF16))
    b3 = b.reshape(t // CHUNK, CHUNK, LANES)
    b_last = jnp.broadcast_to(b3[:, CHUNK - 1:CHUNK, :], b3.shape).reshape(t, LANES)
    q_in = (q * jnp.exp(b)).astype(BF16)
    k_out = (k * jnp.exp(b_last - b)).astype(BF16)
    masks = _pair_masks()
    e = lax.broadcasted_iota(jnp.int32, st_ref.shape, 0)
    d = lax.broadcasted_iota(jnp.int32, st_ref.shape, 1)
    same_head = (e < GLA_DV) == (d < GLA_DK)
    outs = []
    for c in range(t // CHUNK):
        rows = slice(c * CHUNK, (c + 1) * CHUNK)
        a = jnp.zeros((CHUNK, 2 * CHUNK), F32)
        for lv in range(len(masks)):
            p = lax.dot_general(q_lv[lv][rows], _stack_heads(k_lv[lv][rows], GLA_DK), NT_DIMS,
                                preferred_element_type=F32)
            a = a + jnp.where(masks[lv], p, 0.0)
        vc = v[rows]
        o = jnp.dot(a.astype(BF16), _stack_heads(vc, GLA_DV), preferred_element_type=F32)
        st = st_ref[...]
        o = o + lax.dot_general(q_in[rows], st.astype(BF16), NT_DIMS, preferred_element_type=F32)
        upd = lax.dot_general(vc, k_out[rows], TN_DIMS, preferred_element_type=F32)
        st_ref[...] = st * jnp.exp(b[(c + 1) * CHUNK - 1:(c + 1) * CHUNK, :]) + jnp.where(same_head, upd, 0.0)
        outs.append(o)
    return jnp.concatenate(outs, axis=0)


def _head_rms_gate(o, norm, gate):
    parts = []
    for hd in range(2):
        oh = o[:, hd * GLA_DV:(hd + 1) * GLA_DV]
        parts.append(oh * lax.rsqrt(jnp.mean(oh * oh, axis=-1, keepdims=True) + HEAD_NORM_EPS))
    return jnp.concatenate(parts, axis=1) * norm * _silu(gate)


def _log_sigmoid(x):
    return jnp.minimum(x, 0.0) - jnp.log1p(jnp.exp(-jnp.abs(x)))


def _gla_kernel(q_ref, k_ref, v_ref, ga_ref, gr_ref, wg_ref, bg_ref, norm_ref, o_ref, st_ref):
    @pl.when(pl.program_id(2) == 0)
    def _():
        st_ref[...] = jnp.zeros_like(st_ref)

    gate_in = jnp.dot(ga_ref[...].astype(BF16), wg_ref[...], preferred_element_type=F32) + bg_ref[...]
    la = _log_sigmoid(gate_in) / GLA_TAU
    q = q_ref[...].astype(F32) * (GLA_DK ** -0.5)
    o = _gated_linear_tile(q, k_ref[...].astype(F32), v_ref[...], la, st_ref)
    o_ref[...] = _head_rms_gate(o, norm_ref[...], gr_ref[...]).astype(o_ref.dtype)


def _gla(hb, hf, w_gate, b_gate, norm, batch, seq):
    m = hb.shape[0]
    tt = min(SEQ_TILE, seq)
    nt = seq // tt
    hp_n = GLA_HEADS // 2
    wg = jnp.zeros((LANES, GLA_HEADS * GLA_DK), BF16).at[:GLA_RANK].set(w_gate.astype(BF16))
    row = lambda b, h, i: b * nt + i
    return pl.pallas_call(
        _gla_kernel,
        name="gla",
        grid=(batch, hp_n, nt),
        in_specs=[pl.BlockSpec((tt, LANES), lambda b, h, i: (row(b, h, i), 12 + h)),
                  pl.BlockSpec((tt, LANES), lambda b, h, i: (row(b, h, i), 14 + h)),
                  pl.BlockSpec((tt, 2 * GLA_DV), lambda b, h, i: (row(b, h, i), 8 + h)),
                  pl.BlockSpec((tt, LANES), lambda b, h, i: (row(b, h, i), 4)),
                  pl.BlockSpec((tt, 2 * GLA_DV), lambda b, h, i: (row(b, h, i), h)),
                  pl.BlockSpec((LANES, LANES), lambda b, h, i: (0, h)),
                  pl.BlockSpec((1, LANES), lambda b, h, i: (0, h)),
                  pl.BlockSpec((1, 2 * GLA_DV), lambda b, h, i: (0, h))],
        out_specs=pl.BlockSpec((tt, 2 * GLA_DV), lambda b, h, i: (row(b, h, i), h)),
        out_shape=jax.ShapeDtypeStruct((m, GLA_HEADS * GLA_DV), BF16),
        scratch_shapes=[pltpu.VMEM((2 * GLA_DV, LANES), F32)],
        compiler_params=_cparams(("parallel", "parallel", "arbitrary")),
    )(hb, hb, hb, hf, hf, wg, b_gate.reshape(1, -1), norm.reshape(1, -1))


def _hgrn_kernel(q_ref, f_ref, v_ref, g_ref, lb_ref, norm_ref, o_ref, st_ref):
    @pl.when(pl.program_id(2) == 0)
    def _():
        st_ref[...] = jnp.zeros_like(st_ref)

    lb = lb_ref[...]
    f = lb + (1.0 - lb) * _sigmoid(f_ref[...])
    o = _gated_linear_tile(q_ref[...], 1.0 - f, v_ref[...].astype(BF16), jnp.log(f), st_ref)
    o_ref[...] = _head_rms_gate(o, norm_ref[...], g_ref[...]).astype(o_ref.dtype)


def _hgrn(h, lb, norm, batch, seq):
    m = h.shape[0]
    tt = min(SEQ_TILE, seq)
    nt = seq // tt
    row = lambda b, hh, i: b * nt + i
    return pl.pallas_call(
        _hgrn_kernel,
        name="hgrn",
        grid=(batch, HGRN_HEADS // 2, nt),
        in_specs=[pl.BlockSpec((tt, LANES), lambda b, hh, i: (row(b, hh, i), hh)),
                  pl.BlockSpec((tt, LANES), lambda b, hh, i: (row(b, hh, i), 2 + hh)),
                  pl.BlockSpec((tt, 2 * GLA_DV), lambda b, hh, i: (row(b, hh, i), 2 + hh)),
                  pl.BlockSpec((tt, 2 * GLA_DV), lambda b, hh, i: (row(b, hh, i), 4 + hh)),
                  pl.BlockSpec((1, LANES), lambda b, hh, i: (0, hh)),
                  pl.BlockSpec((1, 2 * GLA_DV), lambda b, hh, i: (0, hh))],
        out_specs=pl.BlockSpec((tt, 2 * GLA_DV), lambda b, hh, i: (row(b, hh, i), hh)),
        out_shape=jax.ShapeDtypeStruct((m, HGRN_HEADS * GLA_DV), BF16),
        scratch_shapes=[pltpu.VMEM((2 * GLA_DV, LANES), F32)],
        compiler_params=_cparams(("parallel", "parallel", "arbitrary")),
    )(h, h, h, h, lb.reshape(1, -1), norm.reshape(1, -1))


def _rotate(x, cos, sin_signed):
    lane = lax.broadcasted_iota(jnp.int32, x.shape, 1)
    half = GLA_DK // 2
    swapped = jnp.where(lane % GLA_DK < half, pltpu.roll(x, LANES - half, 1), pltpu.roll(x, half, 1))
    return x * cos + swapped * sin_signed


def _ret_kernel(q_ref, k_ref, v_ref, g_ref, cos_ref, sin_ref, dmask_ref, zeta_ref, xi_ref, gch_ref,
                norm_ref, o_ref, st_ref):
    @pl.when(pl.program_id(2) == 0)
    def _():
        st_ref[...] = jnp.zeros_like(st_ref)

    t = q_ref.shape[0]
    cos, sin = cos_ref[...], sin_ref[...]
    q = _rotate(q_ref[...], cos, sin)
    k = _rotate(k_ref[...], cos, sin) * (GLA_DK ** -0.5)
    v = v_ref[...].astype(BF16)
    nc = t // CHUNK
    zeta = jnp.broadcast_to(zeta_ref[0][None], (nc, CHUNK, LANES)).reshape(t, LANES)
    xi = jnp.broadcast_to(xi_ref[0][None], (nc, CHUNK, LANES)).reshape(t, LANES)
    q_b, k_b = q.astype(BF16), k.astype(BF16)
    q_in = (q * xi).astype(BF16)
    k_out = (k * zeta).astype(BF16)
    dmask = dmask_ref[0]
    g_chunk = gch_ref[0]
    e = lax.broadcasted_iota(jnp.int32, st_ref.shape, 0)
    d = lax.broadcasted_iota(jnp.int32, st_ref.shape, 1)
    same_head = (e < GLA_DV) == (d < GLA_DK)
    outs = []
    for c in range(nc):
        rows = slice(c * CHUNK, (c + 1) * CHUNK)
        a = lax.dot_general(q_b[rows], _stack_heads(k_b[rows], GLA_DK), NT_DIMS,
                            preferred_element_type=F32) * dmask
        vc = v[rows]
        o = jnp.dot(a.astype(BF16), _stack_heads(vc, GLA_DV), preferred_element_type=F32)
        st = st_ref[...]
        o = o + lax.dot_general(q_in[rows], st.astype(BF16), NT_DIMS, preferred_element_type=F32)
        upd = lax.dot_general(vc, k_out[rows], TN_DIMS, preferred_element_type=F32)
        st_ref[...] = st * g_chunk + jnp.where(same_head, upd, 0.0)
        outs.append(o)
    o = jnp.concatenate(outs, axis=0)
    parts = []
    for hd in range(2):
        oh = o[:, hd * GLA_DV:(hd + 1) * GLA_DV]
        oh = oh - jnp.mean(oh, axis=-1, keepdims=True)
        parts.append(oh * lax.rsqrt(jnp.mean(oh * oh, axis=-1, keepdims=True) + HEAD_NORM_EPS))
    o_ref[...] = (jnp.concatenate(parts, axis=1) * norm_ref[...] * _silu(g_ref[...])).astype(o_ref.dtype)


def _retention_tables(seq):
    half = GLA_DK // 2
    inv = ROPE_BASE ** (-jnp.arange(half, dtype=F32) / half)
    ang = jnp.arange(seq, dtype=F32)[:, None] * inv[None, :]
    cos, sin = jnp.cos(ang), jnp.sin(ang)
    cos_t = jnp.tile(cos, (1, 4))
    sin_t = jnp.tile(jnp.concatenate([-sin, sin], axis=1), (1, 2))
    log_g = jnp.log1p(-jnp.exp2(-5.0 - jnp.arange(RET_HEADS, dtype=F32)))
    pos = jnp.arange(CHUNK, dtype=F32)
    rel = pos[:, None] - pos[None, :]
    dmask = jnp.where(rel >= 0, jnp.exp(jnp.maximum(rel, 0.0)[None] * log_g[:, None, None]), 0.0)
    zeta = jnp.exp((CHUNK - 1 - pos)[None, :] * log_g[:, None])
    xi = jnp.exp((pos + 1.0)[None, :] * log_g[:, None])
    g_chunk = jnp.exp(CHUNK * log_g)
    hp_n = RET_HEADS // 2
    pair = lambda x: x.reshape(hp_n, 2, *x.shape[1:])
    dmask_p = jnp.concatenate([pair(dmask)[:, 0], pair(dmask)[:, 1]], axis=-1)
    lanes = lambda x: jnp.repeat(pair(x), GLA_DK, axis=1)
    zeta_p = jnp.swapaxes(lanes(zeta), 1, 2)
    xi_p = jnp.swapaxes(lanes(xi), 1, 2)
    gch_p = lanes(g_chunk)[:, None, :]
    return cos_t, sin_t, dmask_p, zeta_p, xi_p, gch_p


def _retention(h, norm, batch, seq):
    m = h.shape[0]
    tt = min(SEQ_TILE, seq)
    nt = seq // tt
    cos_t, sin_t, dmask, zeta, xi, gch = _retention_tables(seq)
    row = lambda b, hh, i: b * nt + i
    const3 = lambda shape: pl.BlockSpec((1,) + shape, lambda b, hh, i: (hh, 0, 0))
    return pl.pallas_call(
        _ret_kernel,
        name="retention",
        grid=(batch, RET_HEADS // 2, nt),
        in_specs=[pl.BlockSpec((tt, LANES), lambda b, hh, i: (row(b, hh, i), 12 + hh)),
                  pl.BlockSpec((tt, LANES), lambda b, hh, i: (row(b, hh, i), 14 + hh)),
                  pl.BlockSpec((tt, 2 * GLA_DV), lambda b, hh, i: (row(b, hh, i), 8 + hh)),
                  pl.BlockSpec((tt, 2 * GLA_DV), lambda b, hh, i: (row(b, hh, i), 10 + hh)),
                  pl.BlockSpec((tt, LANES), lambda b, hh, i: (i, 0)),
                  pl.BlockSpec((tt, LANES), lambda b, hh, i: (i, 0)),
                  const3((CHUNK, 2 * CHUNK)), const3((CHUNK, LANES)), const3((CHUNK, LANES)),
                  const3((1, LANES)),
                  pl.BlockSpec((1, 2 * GLA_DV), lambda b, hh, i: (0, hh))],
        out_specs=pl.BlockSpec((tt, 2 * GLA_DV), lambda b, hh, i: (row(b, hh, i), hh)),
        out_shape=jax.ShapeDtypeStruct((m, RET_HEADS * GLA_DV), BF16),
        scratch_shapes=[pltpu.VMEM((2 * GLA_DV, LANES), F32)],
        compiler_params=_cparams(("parallel", "parallel", "arbitrary")),
    )(h, h, h, h, cos_t, sin_t, dmask, zeta, xi, gch, norm.reshape(1, -1))


def _outproj_kernel(oa_ref, ob_ref, x_ref, wa_ref, wb_ref, g_ref, b_ref, o_ref, *, alpha):
    y = jnp.dot(oa_ref[...], wa_ref[...], preferred_element_type=F32)
    y = y + jnp.dot(ob_ref[...], wb_ref[...], preferred_element_type=F32)
    o_ref[...] = _layer_norm(alpha * x_ref[...] + y, g_ref[...], b_ref[...])


def _outproj_ln(oa, ob, x, w_out, g, b, alpha):
    m, d = x.shape
    tm = min(ROW_TILE, m)
    wa = w_out[:oa.shape[1]].astype(BF16)
    wb = w_out[oa.shape[1]:].astype(BF16)
    full = lambda a: pl.BlockSpec(a.shape, lambda i: (0, 0))
    rows = lambda a: pl.BlockSpec((tm, a.shape[1]), lambda i: (i, 0))
    g2, b2 = g.reshape(1, -1), b.reshape(1, -1)
    return pl.pallas_call(
        functools.partial(_outproj_kernel, alpha=alpha),
        name="out_proj_ln",
        grid=(m // tm,),
        in_specs=[rows(oa), rows(ob), rows(x), full(wa), full(wb), full(g2), full(b2)],
        out_specs=rows(x),
        out_shape=jax.ShapeDtypeStruct((m, d), F32),
        compiler_params=_cparams(("parallel",)),
    )(oa, ob, x, wa, wb, g2, b2)


def _ln_ple(x, ffn, p, g, b, wg, wp, alpha):
    y = _layer_norm(alpha * x + ffn, g, b)
    gate = _sigmoid(jnp.dot(y.astype(BF16), wg, preferred_element_type=F32))
    return y + gate * jnp.dot(p.astype(BF16), wp, preferred_element_type=F32)


def _swiglu_kernel(x_ref, p_ref, w1_ref, w3_ref, w2_ref, g_ref, b_ref, wg_ref, wp_ref, o_ref,
                   xb_ref, acc_ref, *, alpha):
    f = pl.program_id(1)

    @pl.when(f == 0)
    def _():
        xb_ref[...] = x_ref[...].astype(BF16)
        acc_ref[...] = jnp.zeros_like(acc_ref)

    xb = xb_ref[...]
    h1 = jnp.dot(xb, w1_ref[...], preferred_element_type=F32)
    h3 = jnp.dot(xb, w3_ref[...], preferred_element_type=F32)
    acc_ref[...] += jnp.dot((_silu(h1) * h3).astype(BF16), w2_ref[...], preferred_element_type=F32)

    @pl.when(f == pl.num_programs(1) - 1)
    def _():
        o_ref[...] = _ln_ple(x_ref[...], acc_ref[...], p_ref[...], g_ref[...], b_ref[...],
                             wg_ref[...], wp_ref[...], alpha)


def _swiglu_ln_ple(x, p, w1, w3, w2, g, b, wg, wp, alpha):
    m, d = x.shape
    tm = min(ROW_TILE, m)
    pad = FFN_F_PAD - w1.shape[1]
    w1p = jnp.pad(w1.astype(BF16), ((0, 0), (0, pad)))
    w3p = jnp.pad(w3.astype(BF16), ((0, 0), (0, pad)))
    w2p = jnp.pad(w2.astype(BF16), ((0, pad), (0, 0)))
    nf = FFN_F_PAD // FFN_F_TILE
    g2, b2 = g.reshape(1, -1), b.reshape(1, -1)
    wgb, wpb = wg.astype(BF16), wp.astype(BF16)
    full = lambda a: pl.BlockSpec(a.shape, lambda i, f: (0, 0))
    rows = lambda a: pl.BlockSpec((tm, a.shape[1]), lambda i, f: (i, 0))
    return pl.pallas_call(
        functools.partial(_swiglu_kernel, alpha=alpha),
        name="swiglu_ln_ple",
        grid=(m // tm, nf),
        in_specs=[rows(x), rows(p),
                  pl.BlockSpec((d, FFN_F_TILE), lambda i, f: (0, f)),
                  pl.BlockSpec((d, FFN_F_TILE), lambda i, f: (0, f)),
                  pl.BlockSpec((FFN_F_TILE, d), lambda i, f: (f, 0)),
                  full(g2), full(b2), full(wgb), full(wpb)],
        out_specs=rows(x),
        out_shape=jax.ShapeDtypeStruct((m, d), F32),
        scratch_shapes=[pltpu.VMEM((tm, d), BF16), pltpu.VMEM((tm, d), F32)],
        compiler_params=_cparams(("parallel", "arbitrary")),
    )(x, p, w1p, w3p, w2p, g2, b2, wgb, wpb)


def _router_kernel(x_ref, w_ref, b_ref, e_ref, g_ref):
    logits = jnp.dot(x_ref[...], w_ref[...], preferred_element_type=F32,
                     precision=lax.Precision.HIGHEST) + b_ref[...]
    lane = lax.broadcasted_iota(jnp.int32, logits.shape, 1)
    lane_f = lane.astype(F32)
    logits = jnp.where(lane < N_EXPERTS, logits, -jnp.inf)
    m1 = jnp.max(logits, axis=1, keepdims=True)
    i1 = jnp.min(jnp.where(logits == m1, lane_f, float(LANES)), axis=1, keepdims=True)
    rest = jnp.where(lane_f == i1, -jnp.inf, logits)
    m2 = jnp.max(rest, axis=1, keepdims=True)
    i2 = jnp.min(jnp.where(rest == m2, lane_f, float(LANES)), axis=1, keepdims=True)
    z = jnp.exp(m2 - m1)
    g1 = 1.0 / (1.0 + z)
    e_ref[...] = jnp.where(lane == 0, i1, jnp.where(lane == 1, i2, 0.0)).astype(jnp.int32)
    g_ref[...] = jnp.where(lane == 0, g1, jnp.where(lane == 1, z * g1, 0.0))


def _router(x, w_router, b_router):
    m, d = x.shape
    tm = min(ROW_TILE, m)
    w = jnp.zeros((d, LANES), F32).at[:, :N_EXPERTS].set(w_router)
    b = jnp.zeros((1, LANES), F32).at[0, :N_EXPERTS].set(b_router)
    e, g = pl.pallas_call(
        _router_kernel,
        name="router",
        grid=(m // tm,),
        in_specs=[pl.BlockSpec((tm, d), lambda i: (i, 0)),
                  pl.BlockSpec((d, LANES), lambda i: (0, 0)),
                  pl.BlockSpec((1, LANES), lambda i: (0, 0))],
        out_specs=[pl.BlockSpec((tm, LANES), lambda i: (i, 0)),
                   pl.BlockSpec((tm, LANES), lambda i: (i, 0))],
        out_shape=[jax.ShapeDtypeStruct((m, LANES), jnp.int32),
                   jax.ShapeDtypeStruct((m, LANES), F32)],
        compiler_params=_cparams(("parallel",)),
    )(x, w, b)
    return e[:, :TOP_K], g[:, :TOP_K]


def _experts_kernel(be_ref, x_ref, w1_ref, w3_ref, w2_ref, o_ref, acc_ref):
    f = pl.program_id(1)

    @pl.when(f == 0)
    def _():
        acc_ref[...] = jnp.zeros_like(acc_ref)

    xb = x_ref[...]
    h1 = jnp.dot(xb, w1_ref[0], preferred_element_type=F32)
    h3 = jnp.dot(xb, w3_ref[0], preferred_element_type=F32)
    acc_ref[...] += jnp.dot((_silu(h1) * h3).astype(BF16), w2_ref[0], preferred_element_type=F32)

    @pl.when(f == pl.num_programs(1) - 1)
    def _():
        o_ref[...] = acc_ref[...]


def _experts(xg, blk_e, w1, w3, w2):
    r, d = xg.shape
    fe = w1.shape[2]
    nf = fe // MOE_F_TILE
    grid_spec = pltpu.PrefetchScalarGridSpec(
        num_scalar_prefetch=1,
        grid=(r // MOE_ROW_TILE, nf),
        in_specs=[pl.BlockSpec((MOE_ROW_TILE, d), lambda i, f, be: (i, 0)),
                  pl.BlockSpec((1, d, MOE_F_TILE), lambda i, f, be: (be[i], 0, f)),
                  pl.BlockSpec((1, d, MOE_F_TILE), lambda i, f, be: (be[i], 0, f)),
                  pl.BlockSpec((1, MOE_F_TILE, d), lambda i, f, be: (be[i], f, 0))],
        out_specs=pl.BlockSpec((MOE_ROW_TILE, d), lambda i, f, be: (i, 0)),
        scratch_shapes=[pltpu.VMEM((MOE_ROW_TILE, d), F32)],
    )
    return pl.pallas_call(
        _experts_kernel,
        name="experts",
        grid_spec=grid_spec,
        out_shape=jax.ShapeDtypeStruct((r, d), F32),
        compiler_params=_cparams(("parallel", "arbitrary")),
    )(blk_e, xg, w1, w3, w2)


def _combine_kernel(x_ref, y0_ref, y1_ref, gt_ref, p_ref, g_ref, b_ref, wg_ref, wp_ref, o_ref, *, alpha):
    gt = gt_ref[...]
    ffn = y0_ref[...] * gt[:, 0:1] + y1_ref[...] * gt[:, 1:2]
    o_ref[...] = _ln_ple(x_ref[...], ffn, p_ref[...], g_ref[...], b_ref[...], wg_ref[...], wp_ref[...],
                         alpha)


def _combine_ln_ple(x, y0, y1, gates, p, g, b, wg, wp, alpha):
    m, d = x.shape
    tm = min(ROW_TILE, m)
    g2, b2 = g.reshape(1, -1), b.reshape(1, -1)
    wgb, wpb = wg.astype(BF16), wp.astype(BF16)
    gt = jnp.zeros((m, LANES), F32).at[:, :TOP_K].set(gates)
    full = lambda a: pl.BlockSpec(a.shape, lambda i: (0, 0))
    rows = lambda a: pl.BlockSpec((tm, a.shape[1]), lambda i: (i, 0))
    return pl.pallas_call(
        functools.partial(_combine_kernel, alpha=alpha),
        name="combine_ln_ple",
        grid=(m // tm,),
        in_specs=[rows(x), rows(y0), rows(y1), rows(gt), rows(p), full(g2), full(b2), full(wgb), full(wpb)],
        out_specs=rows(x),
        out_shape=jax.ShapeDtypeStruct((m, d), F32),
        compiler_params=_cparams(("parallel",)),
    )(x, y0, y1, gt, p, g2, b2, wgb, wpb)


def _moe_ln_ple(x, p, w_router, b_router, w1, w3, w2, g, b, wg, wp, alpha):
    m, d = x.shape
    top_e, gates = _router(x, w_router, b_router)
    n_asg = m * TOP_K
    e_flat = top_e.reshape(-1)
    order = jnp.argsort(e_flat)
    e_s = e_flat[order]
    tok_s = (order // TOP_K).astype(jnp.int32)
    counts = jnp.zeros((N_EXPERTS,), jnp.int32).at[e_flat].add(1)
    starts = jnp.cumsum(counts) - counts
    padded = (counts + MOE_ROW_TILE - 1) // MOE_ROW_TILE * MOE_ROW_TILE
    p_ends = jnp.cumsum(padded)
    p_starts = p_ends - padded
    dest = p_starts[e_s] + jnp.arange(n_asg, dtype=jnp.int32) - starts[e_s]
    n_rows = (n_asg // MOE_ROW_TILE + N_EXPERTS) * MOE_ROW_TILE
    tok_buf = jnp.zeros((n_rows,), jnp.int32).at[dest].set(tok_s)
    pos = jnp.zeros((n_asg,), jnp.int32).at[order].set(dest).reshape(m, TOP_K)
    blk_start = jnp.arange(n_rows // MOE_ROW_TILE, dtype=jnp.int32) * MOE_ROW_TILE
    blk_e = jnp.minimum(jnp.searchsorted(p_ends, blk_start, side='right'), N_EXPERTS - 1).astype(jnp.int32)
    xg = x.astype(BF16)[tok_buf]
    y = _experts(xg, blk_e, w1.astype(BF16), w3.astype(BF16), w2.astype(BF16))
    return _combine_ln_ple(x, y[pos[:, 0]], y[pos[:, 1]], gates, p, g, b, wg, wp, alpha)


def kernel(x, p, rel_bias, even_w_in, gla_w_gate, gla_b_gate, gla_norm, even_w_out, odd_w_in, hgrn_gamma,
           hgrn_norm, ret_norm, odd_w_out, ln_mix_g, ln_mix_b, ln_ffn_g, ln_ffn_b, ffn_w1, ffn_w3, ffn_w2,
           router_w, router_b, expert_w1, expert_w3, expert_w2, ple_w_gate, ple_w_proj):
    batch, seq, d = x.shape
    depth = p.shape[0]
    m = batch * seq
    alpha = (2.0 * depth) ** 0.25
    lb_p = jax.nn.softmax(hgrn_gamma.astype(F32), axis=0)
    lb_all = jnp.cumsum(lb_p, axis=0) - lb_p[0]
    xf = x.reshape(m, d)
    pf = p.reshape(depth, m, PLE_DIM)
    n_mq = 3 * MOBA_HEADS * MOBA_DH + 2 * GLA_HEADS * GLA_DK + GLA_HEADS * GLA_DV
    for i in range(depth):
        j = i // 2
        if i % 2 == 0:
            w_in = even_w_in[j]
            w_b = w_in[:, :n_mq].astype(BF16)
            n_gr = GLA_HEADS * GLA_DV
            w_f = jnp.concatenate([w_in[:, n_mq + GLA_RANK:], w_in[:, n_mq:n_mq + GLA_RANK],
                                   jnp.zeros((d, LANES - GLA_RANK), F32)], axis=1).astype(BF16)
            hb = _proj(xf, w_b, BF16)
            hf = _proj(xf, w_f, F32)
            o_a = _moba(hb, rel_bias, batch, seq)
            o_b = _gla(hb, hf, gla_w_gate[j], gla_b_gate[j], gla_norm[j], batch, seq)
            xf = _outproj_ln(o_a, o_b, xf, even_w_out[j], ln_mix_g[i], ln_mix_b[i], alpha)
            xf = _swiglu_ln_ple(xf, pf[i], ffn_w1[j], ffn_w3[j], ffn_w2[j], ln_ffn_g[i], ln_ffn_b[i],
                                ple_w_gate[i], ple_w_proj[i], alpha)
        else:
            h = _proj(xf, odd_w_in[j].astype(BF16), F32)
            o_c = _hgrn(h, lb_all[i], hgrn_norm[j], batch, seq)
            o_d = _retention(h, ret_norm[j], batch, seq)
            xf = _outproj_ln(o_c, o_d, xf, odd_w_out[j], ln_mix_g[i], ln_mix_b[i], alpha)
            xf = _moe_ln_ple(xf, pf[i], router_w[j], router_b[j], expert_w1[j], expert_w3[j], expert_w2[j],
                             ln_ffn_g[i], ln_ffn_b[i], ple_w_gate[i], ple_w_proj[i], alpha)
    return xf.reshape(batch, seq, d)
```

```python
import functools
import math

import numpy as np
import jax
import jax.numpy as jnp
from jax import lax
from jax.experimental import pallas as pl
from jax.experimental.pallas import tpu as pltpu

D_MODEL = 1024
PLE_DIM = 256
LN_EPS = 1e-5
HEAD_NORM_EPS = 1e-6
MOBA_HEADS, MOBA_DH, MOBA_BLOCK, MOBA_TOPK = 8, 64, 256, 3
REL_BUCKETS, REL_MAX_DIST = 32, 128
GLA_HEADS, GLA_DK, GLA_DV, GLA_RANK, GLA_TAU = 4, 64, 128, 16, 16.0
HGRN_HEADS = 4
RET_HEADS = 4
ROPE_BASE = 10000.0
CHUNK = 64
FFN_DENSE = 2752
N_EXPERTS, TOP_K, FFN_EXPERT = 8, 2, 3584

LANES = 128
VMEM_LIMIT = 56 * 1024 * 1024
ROW_TILE = 512
SEQ_TILE = 512
MOE_ROW_TILE = 512
MOE_F_TILE = 512
FFN_F_PAD = 2816
FFN_F_TILE = 1408
NEG = -1e30

BF16 = jnp.bfloat16
F32 = jnp.float32
NT_DIMS = (((1,), (1,)), ((), ()))
TN_DIMS = (((0,), (0,)), ((), ()))


def _cparams(sem):
    return pltpu.CompilerParams(dimension_semantics=sem, vmem_limit_bytes=VMEM_LIMIT)


def _sigmoid(x):
    return 1.0 / (1.0 + jnp.exp(-x))


def _silu(x):
    return x * _sigmoid(x)


def _layer_norm(y, g, b):
    yc = y - jnp.mean(y, axis=-1, keepdims=True)
    return yc * lax.rsqrt(jnp.mean(yc * yc, axis=-1, keepdims=True) + LN_EPS) * g + b


def _proj_kernel(x_ref, w_ref, o_ref):
    o_ref[...] = jnp.dot(x_ref[...].astype(BF16), w_ref[...],
                         preferred_element_type=F32).astype(o_ref.dtype)


def _proj(x, w, out_dtype):
    m, d = x.shape
    n = w.shape[1]
    tm = min(ROW_TILE, m)
    return pl.pallas_call(
        _proj_kernel,
        name="in_proj",
        grid=(m // tm,),
        in_specs=[pl.BlockSpec((tm, d), lambda i: (i, 0)),
                  pl.BlockSpec((d, n), lambda i: (0, 0))],
        out_specs=pl.BlockSpec((tm, n), lambda i: (i, 0)),
        out_shape=jax.ShapeDtypeStruct((m, n), out_dtype),
        compiler_params=_cparams(("parallel",)),
    )(x, w)


def _moba_kernel(c_far_ref, q_ref, k_ref, vt_ref, bias_ref, o_ref, kmean_ref, s_ref, *, nb_pad):
    hp = pl.program_id(1)
    qi = pl.program_id(2)
    tq = MOBA_BLOCK

    @pl.when(qi == 0)
    def _():
        kmean_ref[...] = jnp.zeros_like(kmean_ref)

    k_own = k_ref[pl.ds(pl.multiple_of(qi * tq, tq), tq), :]
    q = q_ref[...]
    kmean = kmean_ref[...]
    kmean_ref[pl.ds(qi, 1), :] = jnp.mean(k_own.astype(F32), axis=0, keepdims=True)

    lane = lax.broadcasted_iota(jnp.int32, (tq, LANES), 1)
    blk = lax.broadcasted_iota(jnp.int32, (tq, nb_pad), 1)
    blk_f = blk.astype(F32)
    qs, qa = [], []
    for hd in range(2):
        in_head = (lane >= MOBA_DH * hd) & (lane < MOBA_DH * (hd + 1))
        qh = jnp.where(in_head, q, jnp.zeros_like(q))
        gate = lax.dot_general(qh.astype(F32), kmean, NT_DIMS, preferred_element_type=F32,
                               precision=lax.Precision.HIGHEST)
        g = jnp.where(blk < qi, gate, -jnp.inf)
        sel = jnp.zeros((tq, nb_pad), jnp.bool_)
        for _ in range(MOBA_TOPK):
            gmax = jnp.max(g, axis=1, keepdims=True)
            idx = jnp.min(jnp.where(g == gmax, blk_f, float(nb_pad)), axis=1, keepdims=True)
            pick = (blk_f == idx) & (gmax > -jnp.inf)
            sel = sel | pick
            g = jnp.where(pick, -jnp.inf, g)
        qs_t = (qh.astype(F32) * MOBA_DH ** -0.5).T
        qs.append(qs_t.astype(BF16))
        qa.append(jnp.concatenate([qs_t, jnp.where(sel, 0.0, NEG).T], axis=0).astype(BF16))

    def v_t(j, hd):
        return vt_ref[j, hd * MOBA_DH:(hd + 1) * MOBA_DH, :]

    def update(state, s, shift, vt):
        m, l, acc = state
        m_new = jnp.maximum(m, jnp.max(s, axis=0, keepdims=True) + shift)
        alpha = jnp.exp(m - m_new)
        p = jnp.exp(s - (m_new - shift))
        l = alpha * l + jnp.sum(p, axis=0, keepdims=True)
        acc = alpha * acc + jnp.dot(vt, p.astype(BF16), preferred_element_type=F32)
        return m_new, l, acc

    state = []
    for hd in range(2):
        s = jnp.dot(k_own, qs[hd], preferred_element_type=F32) + bias_ref[hd, 0]
        m0 = jnp.max(s, axis=0, keepdims=True)
        p = jnp.exp(s - m0)
        l0 = jnp.sum(p, axis=0, keepdims=True)
        state.append((m0, l0, jnp.dot(v_t(qi, hd), p.astype(BF16), preferred_element_type=F32)))

    def scores_into(slot, j):
        onehot = jnp.where(blk == j, 1.0, 0.0).astype(BF16)
        ka = jnp.concatenate([k_ref[pl.ds(pl.multiple_of(j * tq, tq), tq), :], onehot], axis=1)
        for hd in range(2):
            s_ref[slot, hd] = jnp.dot(ka, qa[hd], preferred_element_type=F32)

    n_far = qi - 1

    def far_pair(jj, st):
        a = 2 * jj
        scores_into(1, a + 1)
        st = tuple(update(st[hd], s_ref[0, hd], c_far_ref[2 * hp + hd], v_t(a, hd)) for hd in range(2))
        scores_into(0, a + 2)
        return tuple(update(st[hd], s_ref[1, hd], jnp.where(a + 1 < n_far, c_far_ref[2 * hp + hd], NEG),
                            v_t(a + 1, hd)) for hd in range(2))

    def past_blocks(st):
        scores_into(0, 0)
        st = lax.fori_loop(0, (n_far + 1) // 2, far_pair, st)
        return tuple(update(st[hd], s_ref[n_far % 2, hd] + bias_ref[hd, 1], 0.0, v_t(qi - 1, hd))
                     for hd in range(2))

    state = lax.cond(qi >= 1, past_blocks, lambda st: st, tuple(state))
    o_t = jnp.concatenate([state[hd][2] / state[hd][1] for hd in range(2)], axis=0)
    o_ref[...] = o_t.T.astype(o_ref.dtype)


def _t5_bucket(dist):
    n = jnp.maximum(dist, 0)
    max_exact = REL_BUCKETS // 2
    nf = jnp.maximum(n, 1).astype(F32)
    large = max_exact + (jnp.log(nf / max_exact) / math.log(REL_MAX_DIST / max_exact)
                         * (REL_BUCKETS - max_exact)).astype(jnp.int32)
    large = jnp.minimum(large, REL_BUCKETS - 1)
    return jnp.where(n < max_exact, n, large)


def _moba_bias_tables(rel_bias):
    tbl = rel_bias.astype(F32).T
    kpos = jnp.arange(MOBA_BLOCK)[:, None]
    qpos = jnp.arange(MOBA_BLOCK)[None, :]
    d_own = qpos - kpos
    own = jnp.where(d_own >= 0, tbl[:, _t5_bucket(d_own)], NEG)
    adj = tbl[:, _t5_bucket(d_own + MOBA_BLOCK)]
    return jnp.stack([own, adj], axis=1), tbl[:, REL_BUCKETS - 1]


def _proj_t_kernel(x_ref, wt_ref, o_ref):
    o_ref[0] = lax.dot_general(wt_ref[...], x_ref[...].astype(BF16), NT_DIMS,
                               preferred_element_type=F32).astype(o_ref.dtype)


def _proj_t(x, wt):
    m, d = x.shape
    n = wt.shape[0]
    return pl.pallas_call(
        _proj_t_kernel,
        name="in_proj_t",
        grid=(m // MOBA_BLOCK,),
        in_specs=[pl.BlockSpec((MOBA_BLOCK, d), lambda i: (i, 0)),
                  pl.BlockSpec((n, d), lambda i: (0, 0))],
        out_specs=pl.BlockSpec((1, n, MOBA_BLOCK), lambda i: (i, 0, 0)),
        out_shape=jax.ShapeDtypeStruct((m // MOBA_BLOCK, n, MOBA_BLOCK), BF16),
        compiler_params=_cparams(("parallel",)),
    )(x, wt)


def _moba(hb, vt, rel_bias, batch, seq):
    m = hb.shape[0]
    nq = seq // MOBA_BLOCK
    nb_pad = LANES * (-(-nq // LANES))
    bias, c_far = _moba_bias_tables(rel_bias)
    hp_n = MOBA_HEADS // 2
    kern = functools.partial(_moba_kernel, nb_pad=nb_pad)
    return pl.pallas_call(
        kern,
        name="moba",
        grid=(batch, hp_n, nq),
        in_specs=[pl.BlockSpec(memory_space=pltpu.SMEM),
                  pl.BlockSpec((MOBA_BLOCK, LANES), lambda b, h, i: (b * nq + i, h)),
                  pl.BlockSpec((seq, LANES), lambda b, h, i: (b, hp_n + h)),
                  pl.BlockSpec((nq, LANES, MOBA_BLOCK), lambda b, h, i: (b, h, 0)),
                  pl.BlockSpec((2, 2, MOBA_BLOCK, MOBA_BLOCK), lambda b, h, i: (h, 0, 0, 0))],
        out_specs=pl.BlockSpec((MOBA_BLOCK, LANES), lambda b, h, i: (b * nq + i, h)),
        out_shape=jax.ShapeDtypeStruct((m, MOBA_HEADS * MOBA_DH), BF16),
        scratch_shapes=[pltpu.VMEM((nb_pad, LANES), F32),
                        pltpu.VMEM((2, 2, MOBA_BLOCK, MOBA_BLOCK), F32)],
        compiler_params=_cparams(("parallel", "parallel", "arbitrary")),
    )(c_far, hb, hb, vt, bias)


LEVELS = (32, 16, 8, 4, 2, 1)


def _chunk_cumsum(la):
    t = la.shape[0]
    row = lax.broadcasted_iota(jnp.int32, la.shape, 0) % CHUNK
    b = la
    sh = 1
    while sh < CHUNK:
        b = b + jnp.where(row >= sh, pltpu.roll(b, sh, 0), 0.0)
        sh *= 2
    return b


def _level_reference(b, h):
    t, w = b.shape
    if 2 * h >= 8:
        b3 = b.reshape(t // (2 * h), 2 * h, w)
        return jnp.broadcast_to(b3[:, h - 1:h, :], b3.shape).reshape(t, w)
    row = lax.broadcasted_iota(jnp.int32, b.shape, 0)
    if h == 2:
        r4 = row % 4
        return jnp.where(r4 == 0, pltpu.roll(b, t - 1, 0),
                         jnp.where(r4 == 1, b,
                                   jnp.where(r4 == 2, pltpu.roll(b, 1, 0), pltpu.roll(b, 2, 0))))
    return jnp.where(row % 2 == 1, pltpu.roll(b, 1, 0), b)


def _pair_masks():
    t = lax.broadcasted_iota(jnp.int32, (CHUNK, 2 * CHUNK), 0)
    s = lax.broadcasted_iota(jnp.int32, (CHUNK, 2 * CHUNK), 1) % CHUNK
    masks = []
    for h in LEVELS:
        masks.append((t // (2 * h) == s // (2 * h)) & ((t // h) % 2 == 1) & ((s // h) % 2 == 0))
    masks.append(t == s)
    return masks


def _stack_heads(x, split):
    lane = lax.broadcasted_iota(jnp.int32, x.shape, 1)
    z = jnp.zeros_like(x)
    return jnp.concatenate([jnp.where(lane < split, x, z), jnp.where(lane >= split, x, z)], axis=0)


def _gated_linear_tile(q, k, v, la, st_ref):
    t = q.shape[0]
    b = _chunk_cumsum(la)
    q_lv, k_lv = [], []
    for h in LEVELS:
        w = jnp.exp(-jnp.abs(b - _level_reference(b, h)))
        q_lv.append((q * w).astype(BF16))
        k_lv.append((k * w).astype(BF16))
    q_lv.append(q.astype(BF16))
    k_lv.append(k.astype(BF16))
    b3 = b.reshape(t // CHUNK, CHUNK, LANES)
    b_last = jnp.broadcast_to(b3[:, CHUNK - 1:CHUNK, :], b3.shape).reshape(t, LANES)
    q_in = (q * jnp.exp(b)).astype(BF16)
    k_out = (k * jnp.exp(b_last - b)).astype(BF16)
    masks = _pair_masks()
    e = lax.broadcasted_iota(jnp.int32, st_ref.shape, 0)
    d = lax.broadcasted_iota(jnp.int32, st_ref.shape, 1)
    same_head = (e < GLA_DV) == (d < GLA_DK)
    outs = []
    for c in range(t // CHUNK):
        rows = slice(c * CHUNK, (c + 1) * CHUNK)
        a = jnp.zeros((CHUNK, 2 * CHUNK), F32)
        for lv in range(len(masks)):
            p = lax.dot_general(q_lv[lv][rows], _stack_heads(k_lv[lv][rows], GLA_DK), NT_DIMS,
                                preferred_element_type=F32)
            a = a + jnp.where(masks[lv], p, 0.0)
        vc = v[rows]
        o = jnp.dot(a.astype(BF16), _stack_heads(vc, GLA_DV), preferred_element_type=F32)
        st = st_ref[...]
        o = o + lax.dot_general(q_in[rows], st.astype(BF16), NT_DIMS, preferred_element_type=F32)
        upd = lax.dot_general(vc, k_out[rows], TN_DIMS, preferred_element_type=F32)
        st_ref[...] = st * jnp.exp(b[(c + 1) * CHUNK - 1:(c + 1) * CHUNK, :]) + jnp.where(same_head, upd, 0.0)
        outs.append(o)
    return jnp.concatenate(outs, axis=0)


def _head_rms_gate(o, norm, gate):
    parts = []
    for hd in range(2):
        oh = o[:, hd * GLA_DV:(hd + 1) * GLA_DV]
        parts.append(oh * lax.rsqrt(jnp.mean(oh * oh, axis=-1, keepdims=True) + HEAD_NORM_EPS))
    return jnp.concatenate(parts, axis=1) * norm * _silu(gate)


def _log_sigmoid(x):
    return jnp.minimum(x, 0.0) - jnp.log1p(jnp.exp(-jnp.abs(x)))


def _gla_kernel(q_ref, k_ref, v_ref, ga_ref, gr_ref, wg_ref, bg_ref, norm_ref, o_ref, st_ref):
    @pl.when(pl.program_id(2) == 0)
    def _():
        st_ref[...] = jnp.zeros_like(st_ref)

    gate_in = jnp.dot(ga_ref[...].astype(BF16), wg_ref[...], preferred_element_type=F32) + bg_ref[...]
    la = _log_sigmoid(gate_in) / GLA_TAU
    q = q_ref[...].astype(F32) * (GLA_DK ** -0.5)
    o = _gated_linear_tile(q, k_ref[...].astype(F32), v_ref[...], la, st_ref)
    o_ref[...] = _head_rms_gate(o, norm_ref[...], gr_ref[...]).astype(o_ref.dtype)


def _gla(hb, hf, w_gate, b_gate, norm, batch, seq):
    m = hb.shape[0]
    tt = min(SEQ_TILE, seq)
    nt = seq // tt
    hp_n = GLA_HEADS // 2
    wg = jnp.zeros((LANES, GLA_HEADS * GLA_DK), BF16).at[:GLA_RANK].set(w_gate.astype(BF16))
    row = lambda b, h, i: b * nt + i
    return pl.pallas_call(
        _gla_kernel,
        name="gla",
        grid=(batch, hp_n, nt),
        in_specs=[pl.BlockSpec((tt, LANES), lambda b, h, i: (row(b, h, i), 8 + h)),
                  pl.BlockSpec((tt, LANES), lambda b, h, i: (row(b, h, i), 10 + h)),
                  pl.BlockSpec((tt, 2 * GLA_DV), lambda b, h, i: (row(b, h, i), 6 + h)),
                  pl.BlockSpec((tt, LANES), lambda b, h, i: (row(b, h, i), 4)),
                  pl.BlockSpec((tt, 2 * GLA_DV), lambda b, h, i: (row(b, h, i), h)),
                  pl.BlockSpec((LANES, LANES), lambda b, h, i: (0, h)),
                  pl.BlockSpec((1, LANES), lambda b, h, i: (0, h)),
                  pl.BlockSpec((1, 2 * GLA_DV), lambda b, h, i: (0, h))],
        out_specs=pl.BlockSpec((tt, 2 * GLA_DV), lambda b, h, i: (row(b, h, i), h)),
        out_shape=jax.ShapeDtypeStruct((m, GLA_HEADS * GLA_DV), BF16),
        scratch_shapes=[pltpu.VMEM((2 * GLA_DV, LANES), F32)],
        compiler_params=_cparams(("parallel", "parallel", "arbitrary")),
    )(hb, hb, hb, hf, hf, wg, b_gate.reshape(1, -1), norm.reshape(1, -1))


def _hgrn_kernel(q_ref, f_ref, v_ref, g_ref, lb_ref, norm_ref, o_ref, st_ref):
    @pl.when(pl.program_id(2) == 0)
    def _():
        st_ref[...] = jnp.zeros_like(st_ref)

    lb = lb_ref[...]
    f = lb + (1.0 - lb) * _sigmoid(f_ref[...])
    o = _gated_linear_tile(q_ref[...], 1.0 - f, v_ref[...].astype(BF16), jnp.log(f), st_ref)
    o_ref[...] = _head_rms_gate(o, norm_ref[...], g_ref[...]).astype(o_ref.dtype)


def _hgrn(h, lb, norm, batch, seq):
    m = h.shape[0]
    tt = min(SEQ_TILE, seq)
    nt = seq // tt
    row = lambda b, hh, i: b * nt + i
    return pl.pallas_call(
        _hgrn_kernel,
        name="hgrn",
        grid=(batch, HGRN_HEADS // 2, nt),
        in_specs=[pl.BlockSpec((tt, LANES), lambda b, hh, i: (row(b, hh, i), hh)),
                  pl.BlockSpec((tt, LANES), lambda b, hh, i: (row(b, hh, i), 2 + hh)),
                  pl.BlockSpec((tt, 2 * GLA_DV), lambda b, hh, i: (row(b, hh, i), 2 + hh)),
                  pl.BlockSpec((tt, 2 * GLA_DV), lambda b, hh, i: (row(b, hh, i), 4 + hh)),
                  pl.BlockSpec((1, LANES), lambda b, hh, i: (0, hh)),
                  pl.BlockSpec((1, 2 * GLA_DV), lambda b, hh, i: (0, hh))],
        out_specs=pl.BlockSpec((tt, 2 * GLA_DV), lambda b, hh, i: (row(b, hh, i), hh)),
        out_shape=jax.ShapeDtypeStruct((m, HGRN_HEADS * GLA_DV), BF16),
        scratch_shapes=[pltpu.VMEM((2 * GLA_DV, LANES), F32)],
        compiler_params=_cparams(("parallel", "parallel", "arbitrary")),
    )(h, h, h, h, lb.reshape(1, -1), norm.reshape(1, -1))


def _rotate(x, cos, sin_signed):
    lane = lax.broadcasted_iota(jnp.int32, x.shape, 1)
    half = GLA_DK // 2
    swapped = jnp.where(lane % GLA_DK < half, pltpu.roll(x, LANES - half, 1), pltpu.roll(x, half, 1))
    return x * cos + swapped * sin_signed


def _ret_kernel(q_ref, k_ref, v_ref, g_ref, cos_ref, sin_ref, dmask_ref, zeta_ref, xi_ref, gch_ref,
                norm_ref, o_ref, st_ref):
    @pl.when(pl.program_id(2) == 0)
    def _():
        st_ref[...] = jnp.zeros_like(st_ref)

    t = q_ref.shape[0]
    cos, sin = cos_ref[...], sin_ref[...]
    q = _rotate(q_ref[...], cos, sin)
    k = _rotate(k_ref[...], cos, sin) * (GLA_DK ** -0.5)
    v = v_ref[...].astype(BF16)
    nc = t // CHUNK
    zeta = jnp.broadcast_to(zeta_ref[0][None], (nc, CHUNK, LANES)).reshape(t, LANES)
    xi = jnp.broadcast_to(xi_ref[0][None], (nc, CHUNK, LANES)).reshape(t, LANES)
    q_b, k_b = q.astype(BF16), k.astype(BF16)
    q_in = (q * xi).astype(BF16)
    k_out = (k * zeta).astype(BF16)
    dmask = dmask_ref[0]
    g_chunk = gch_ref[0]
    e = lax.broadcasted_iota(jnp.int32, st_ref.shape, 0)
    d = lax.broadcasted_iota(jnp.int32, st_ref.shape, 1)
    same_head = (e < GLA_DV) == (d < GLA_DK)
    outs = []
    for c in range(nc):
        rows = slice(c * CHUNK, (c + 1) * CHUNK)
        a = lax.dot_general(q_b[rows], _stack_heads(k_b[rows], GLA_DK), NT_DIMS,
                            preferred_element_type=F32) * dmask
        vc = v[rows]
        o = jnp.dot(a.astype(BF16), _stack_heads(vc, GLA_DV), preferred_element_type=F32)
        st = st_ref[...]
        o = o + lax.dot_general(q_in[rows], st.astype(BF16), NT_DIMS, preferred_element_type=F32)
        upd = lax.dot_general(vc, k_out[rows], TN_DIMS, preferred_element_type=F32)
        st_ref[...] = st * g_chunk + jnp.where(same_head, upd, 0.0)
        outs.append(o)
    o = jnp.concatenate(outs, axis=0)
    parts = []
    for hd in range(2):
        oh = o[:, hd * GLA_DV:(hd + 1) * GLA_DV]
        oh = oh - jnp.mean(oh, axis=-1, keepdims=True)
        parts.append(oh * lax.rsqrt(jnp.mean(oh * oh, axis=-1, keepdims=True) + HEAD_NORM_EPS))
    o_ref[...] = (jnp.concatenate(parts, axis=1) * norm_ref[...] * _silu(g_ref[...])).astype(o_ref.dtype)


def _retention_tables(seq):
    half = GLA_DK // 2
    inv = ROPE_BASE ** (-jnp.arange(half, dtype=F32) / half)
    ang = jnp.arange(seq, dtype=F32)[:, None] * inv[None, :]
    cos, sin = jnp.cos(ang), jnp.sin(ang)
    cos_t = jnp.tile(cos, (1, 4))
    sin_t = jnp.tile(jnp.concatenate([-sin, sin], axis=1), (1, 2))
    log_g = jnp.log1p(-jnp.exp2(-5.0 - jnp.arange(RET_HEADS, dtype=F32)))
    pos = jnp.arange(CHUNK, dtype=F32)
    rel = pos[:, None] - pos[None, :]
    dmask = jnp.where(rel >= 0, jnp.exp(jnp.maximum(rel, 0.0)[None] * log_g[:, None, None]), 0.0)
    zeta = jnp.exp((CHUNK - 1 - pos)[None, :] * log_g[:, None])
    xi = jnp.exp((pos + 1.0)[None, :] * log_g[:, None])
    g_chunk = jnp.exp(CHUNK * log_g)
    hp_n = RET_HEADS // 2
    pair = lambda x: x.reshape(hp_n, 2, *x.shape[1:])
    dmask_p = jnp.concatenate([pair(dmask)[:, 0], pair(dmask)[:, 1]], axis=-1)
    lanes = lambda x: jnp.repeat(pair(x), GLA_DK, axis=1)
    zeta_p = jnp.swapaxes(lanes(zeta), 1, 2)
    xi_p = jnp.swapaxes(lanes(xi), 1, 2)
    gch_p = lanes(g_chunk)[:, None, :]
    return cos_t, sin_t, dmask_p, zeta_p, xi_p, gch_p


def _retention(h, norm, batch, seq):
    m = h.shape[0]
    tt = min(SEQ_TILE, seq)
    nt = seq // tt
    cos_t, sin_t, dmask, zeta, xi, gch = _retention_tables(seq)
    row = lambda b, hh, i: b * nt + i
    const3 = lambda shape: pl.BlockSpec((1,) + shape, lambda b, hh, i: (hh, 0, 0))
    return pl.pallas_call(
        _ret_kernel,
        name="retention",
        grid=(batch, RET_HEADS // 2, nt),
        in_specs=[pl.BlockSpec((tt, LANES), lambda b, hh, i: (row(b, hh, i), 12 + hh)),
                  pl.BlockSpec((tt, LANES), lambda b, hh, i: (row(b, hh, i), 14 + hh)),
                  pl.BlockSpec((tt, 2 * GLA_DV), lambda b, hh, i: (row(b, hh, i), 8 + hh)),
                  pl.BlockSpec((tt, 2 * GLA_DV), lambda b, hh, i: (row(b, hh, i), 10 + hh)),
                  pl.BlockSpec((tt, LANES), lambda b, hh, i: (i, 0)),
                  pl.BlockSpec((tt, LANES), lambda b, hh, i: (i, 0)),
                  const3((CHUNK, 2 * CHUNK)), const3((CHUNK, LANES)), const3((CHUNK, LANES)),
                  const3((1, LANES)),
                  pl.BlockSpec((1, 2 * GLA_DV), lambda b, hh, i: (0, hh))],
        out_specs=pl.BlockSpec((tt, 2 * GLA_DV), lambda b, hh, i: (row(b, hh, i), hh)),
        out_shape=jax.ShapeDtypeStruct((m, RET_HEADS * GLA_DV), BF16),
        scratch_shapes=[pltpu.VMEM((2 * GLA_DV, LANES), F32)],
        compiler_params=_cparams(("parallel", "parallel", "arbitrary")),
    )(h, h, h, h, cos_t, sin_t, dmask, zeta, xi, gch, norm.reshape(1, -1))


def _outproj_kernel(oa_ref, ob_ref, x_ref, wa_ref, wb_ref, g_ref, b_ref, o_ref, *, alpha):
    y = jnp.dot(oa_ref[...], wa_ref[...], preferred_element_type=F32)
    y = y + jnp.dot(ob_ref[...], wb_ref[...], preferred_element_type=F32)
    o_ref[...] = _layer_norm(alpha * x_ref[...] + y, g_ref[...], b_ref[...])


def _outproj_ln(oa, ob, x, w_out, g, b, alpha):
    m, d = x.shape
    tm = min(ROW_TILE, m)
    wa = w_out[:oa.shape[1]].astype(BF16)
    wb = w_out[oa.shape[1]:].astype(BF16)
    full = lambda a: pl.BlockSpec(a.shape, lambda i: (0, 0))
    rows = lambda a: pl.BlockSpec((tm, a.shape[1]), lambda i: (i, 0))
    g2, b2 = g.reshape(1, -1), b.reshape(1, -1)
    return pl.pallas_call(
        functools.partial(_outproj_kernel, alpha=alpha),
        name="out_proj_ln",
        grid=(m // tm,),
        in_specs=[rows(oa), rows(ob), rows(x), full(wa), full(wb), full(g2), full(b2)],
        out_specs=rows(x),
        out_shape=jax.ShapeDtypeStruct((m, d), F32),
        compiler_params=_cparams(("parallel",)),
    )(oa, ob, x, wa, wb, g2, b2)


def _ln_ple(x, ffn, p, g, b, wg, wp, alpha):
    y = _layer_norm(alpha * x + ffn, g, b)
    gate = _sigmoid(jnp.dot(y.astype(BF16), wg, preferred_element_type=F32))
    return y + gate * jnp.dot(p.astype(BF16), wp, preferred_element_type=F32)


def _swiglu_kernel(x_ref, p_ref, w1_ref, w3_ref, w2_ref, g_ref, b_ref, wg_ref, wp_ref, o_ref,
                   xb_ref, acc_ref, *, alpha):
    f = pl.program_id(1)

    @pl.when(f == 0)
    def _():
        xb_ref[...] = x_ref[...].astype(BF16)
        acc_ref[...] = jnp.zeros_like(acc_ref)

    xb = xb_ref[...]
    h1 = jnp.dot(xb, w1_ref[...], preferred_element_type=F32)
    h3 = jnp.dot(xb, w3_ref[...], preferred_element_type=F32)
    acc_ref[...] += jnp.dot((_silu(h1) * h3).astype(BF16), w2_ref[...], preferred_element_type=F32)

    @pl.when(f == pl.num_programs(1) - 1)
    def _():
        o_ref[...] = _ln_ple(x_ref[...], acc_ref[...], p_ref[...], g_ref[...], b_ref[...],
                             wg_ref[...], wp_ref[...], alpha)


def _swiglu_ln_ple(x, p, w1, w3, w2, g, b, wg, wp, alpha):
    m, d = x.shape
    tm = min(ROW_TILE, m)
    pad = FFN_F_PAD - w1.shape[1]
    w1p = jnp.pad(w1.astype(BF16), ((0, 0), (0, pad)))
    w3p = jnp.pad(w3.astype(BF16), ((0, 0), (0, pad)))
    w2p = jnp.pad(w2.astype(BF16), ((0, pad), (0, 0)))
    nf = FFN_F_PAD // FFN_F_TILE
    g2, b2 = g.reshape(1, -1), b.reshape(1, -1)
    wgb, wpb = wg.astype(BF16), wp.astype(BF16)
    full = lambda a: pl.BlockSpec(a.shape, lambda i, f: (0, 0))
    rows = lambda a: pl.BlockSpec((tm, a.shape[1]), lambda i, f: (i, 0))
    return pl.pallas_call(
        functools.partial(_swiglu_kernel, alpha=alpha),
        name="swiglu_ln_ple",
        grid=(m // tm, nf),
        in_specs=[rows(x), rows(p),
                  pl.BlockSpec((d, FFN_F_TILE), lambda i, f: (0, f)),
                  pl.BlockSpec((d, FFN_F_TILE), lambda i, f: (0, f)),
                  pl.BlockSpec((FFN_F_TILE, d), lambda i, f: (f, 0)),
                  full(g2), full(b2), full(wgb), full(wpb)],
        out_specs=rows(x),
        out_shape=jax.ShapeDtypeStruct((m, d), F32),
        scratch_shapes=[pltpu.VMEM((tm, d), BF16), pltpu.VMEM((tm, d), F32)],
        compiler_params=_cparams(("parallel", "arbitrary")),
    )(x, p, w1p, w3p, w2p, g2, b2, wgb, wpb)


def _router_kernel(x_ref, w_ref, b_ref, e_ref, g_ref):
    logits = jnp.dot(x_ref[...], w_ref[...], preferred_element_type=F32,
                     precision=lax.Precision.HIGHEST) + b_ref[...]
    lane = lax.broadcasted_iota(jnp.int32, logits.shape, 1)
    lane_f = lane.astype(F32)
    logits = jnp.where(lane < N_EXPERTS, logits, -jnp.inf)
    m1 = jnp.max(logits, axis=1, keepdims=True)
    i1 = jnp.min(jnp.where(logits == m1, lane_f, float(LANES)), axis=1, keepdims=True)
    rest = jnp.where(lane_f == i1, -jnp.inf, logits)
    m2 = jnp.max(rest, axis=1, keepdims=True)
    i2 = jnp.min(jnp.where(rest == m2, lane_f, float(LANES)), axis=1, keepdims=True)
    z = jnp.exp(m2 - m1)
    g1 = 1.0 / (1.0 + z)
    e_ref[...] = jnp.where(lane == 0, i1, jnp.where(lane == 1, i2, 0.0)).astype(jnp.int32)
    g_ref[...] = jnp.where(lane == 0, g1, jnp.where(lane == 1, z * g1, 0.0))


def _router(x, w_router, b_router):
    m, d = x.shape
    tm = min(ROW_TILE, m)
    w = jnp.zeros((d, LANES), F32).at[:, :N_EXPERTS].set(w_router)
    b = jnp.zeros((1, LANES), F32).at[0, :N_EXPERTS].set(b_router)
    e, g = pl.pallas_call(
        _router_kernel,
        name="router",
        grid=(m // tm,),
        in_specs=[pl.BlockSpec((tm, d), lambda i: (i, 0)),
                  pl.BlockSpec((d, LANES), lambda i: (0, 0)),
                  pl.BlockSpec((1, LANES), lambda i: (0, 0))],
        out_specs=[pl.BlockSpec((tm, LANES), lambda i: (i, 0)),
                   pl.BlockSpec((tm, LANES), lambda i: (i, 0))],
        out_shape=[jax.ShapeDtypeStruct((m, LANES), jnp.int32),
                   jax.ShapeDtypeStruct((m, LANES), F32)],
        compiler_params=_cparams(("parallel",)),
    )(x, w, b)
    return e[:, :TOP_K], g[:, :TOP_K]


def _experts_kernel(be_ref, x_ref, w1_ref, w3_ref, w2_ref, o_ref, acc_ref):
    f = pl.program_id(1)

    @pl.when(f == 0)
    def _():
        acc_ref[...] = jnp.zeros_like(acc_ref)

    xb = x_ref[...]
    h1 = jnp.dot(xb, w1_ref[0], preferred_element_type=F32)
    h3 = jnp.dot(xb, w3_ref[0], preferred_element_type=F32)
    acc_ref[...] += jnp.dot((_silu(h1) * h3).astype(BF16), w2_ref[0], preferred_element_type=F32)

    @pl.when(f == pl.num_programs(1) - 1)
    def _():
        o_ref[...] = acc_ref[...]


def _experts(xg, blk_e, w1, w3, w2):
    r, d = xg.shape
    fe = w1.shape[2]
    nf = fe // MOE_F_TILE
    grid_spec = pltpu.PrefetchScalarGridSpec(
        num_scalar_prefetch=1,
        grid=(r // MOE_ROW_TILE, nf),
        in_specs=[pl.BlockSpec((MOE_ROW_TILE, d), lambda i, f, be: (i, 0)),
                  pl.BlockSpec((1, d, MOE_F_TILE), lambda i, f, be: (be[i], 0, f)),
                  pl.BlockSpec((1, d, MOE_F_TILE), lambda i, f, be: (be[i], 0, f)),
                  pl.BlockSpec((1, MOE_F_TILE, d), lambda i, f, be: (be[i], f, 0))],
        out_specs=pl.BlockSpec((MOE_ROW_TILE, d), lambda i, f, be: (i, 0)),
        scratch_shapes=[pltpu.VMEM((MOE_ROW_TILE, d), F32)],
    )
    return pl.pallas_call(
        _experts_kernel,
        name="experts",
        grid_spec=grid_spec,
        out_shape=jax.ShapeDtypeStruct((r, d), F32),
        compiler_params=_cparams(("parallel", "arbitrary")),
    )(blk_e, xg, w1, w3, w2)


def _combine_kernel(x_ref, y0_ref, y1_ref, gt_ref, p_ref, g_ref, b_ref, wg_ref, wp_ref, o_ref, *, alpha):
    gt = gt_ref[...]
    ffn = y0_ref[...] * gt[:, 0:1] + y1_ref[...] * gt[:, 1:2]
    o_ref[...] = _ln_ple(x_ref[...], ffn, p_ref[...], g_ref[...], b_ref[...], wg_ref[...], wp_ref[...],
                         alpha)


def _combine_ln_ple(x, y0, y1, gates, p, g, b, wg, wp, alpha):
    m, d = x.shape
    tm = min(ROW_TILE, m)
    g2, b2 = g.reshape(1, -1), b.reshape(1, -1)
    wgb, wpb = wg.astype(BF16), wp.astype(BF16)
    gt = jnp.zeros((m, LANES), F32).at[:, :TOP_K].set(gates)
    full = lambda a: pl.BlockSpec(a.shape, lambda i: (0, 0))
    rows = lambda a: pl.BlockSpec((tm, a.shape[1]), lambda i: (i, 0))
    return pl.pallas_call(
        functools.partial(_combine_kernel, alpha=alpha),
        name="combine_ln_ple",
        grid=(m // tm,),
        in_specs=[rows(x), rows(y0), rows(y1), rows(gt), rows(p), full(g2), full(b2), full(wgb), full(wpb)],
        out_specs=rows(x),
        out_shape=jax.ShapeDtypeStruct((m, d), F32),
        compiler_params=_cparams(("parallel",)),
    )(x, y0, y1, gt, p, g2, b2, wgb, wpb)


def _moe_ln_ple(x, p, w_router, b_router, w1, w3, w2, g, b, wg, wp, alpha):
    m, d = x.shape
    top_e, gates = _router(x, w_router, b_router)
    n_asg = m * TOP_K
    e_flat = top_e.reshape(-1)
    order = jnp.argsort(e_flat)
    e_s = e_flat[order]
    tok_s = (order // TOP_K).astype(jnp.int32)
    counts = jnp.zeros((N_EXPERTS,), jnp.int32).at[e_flat].add(1)
    starts = jnp.cumsum(counts) - counts
    padded = (counts + MOE_ROW_TILE - 1) // MOE_ROW_TILE * MOE_ROW_TILE
    p_ends = jnp.cumsum(padded)
    p_starts = p_ends - padded
    dest = p_starts[e_s] + jnp.arange(n_asg, dtype=jnp.int32) - starts[e_s]
    n_rows = (n_asg // MOE_ROW_TILE + N_EXPERTS) * MOE_ROW_TILE
    tok_buf = jnp.zeros((n_rows,), jnp.int32).at[dest].set(tok_s)
    pos = jnp.zeros((n_asg,), jnp.int32).at[order].set(dest).reshape(m, TOP_K)
    blk_start = jnp.arange(n_rows // MOE_ROW_TILE, dtype=jnp.int32) * MOE_ROW_TILE
    blk_e = jnp.minimum(jnp.searchsorted(p_ends, blk_start, side='right'), N_EXPERTS - 1).astype(jnp.int32)
    xg = x.astype(BF16)[tok_buf]
    y = _experts(xg, blk_e, w1.astype(BF16), w3.astype(BF16), w2.astype(BF16))
    return _combine_ln_ple(x, y[pos[:, 0]], y[pos[:, 1]], gates, p, g, b, wg, wp, alpha)


def kernel(x, p, rel_bias, even_w_in, gla_w_gate, gla_b_gate, gla_norm, even_w_out, odd_w_in, hgrn_gamma,
           hgrn_norm, ret_norm, odd_w_out, ln_mix_g, ln_mix_b, ln_ffn_g, ln_ffn_b, ffn_w1, ffn_w3, ffn_w2,
           router_w, router_b, expert_w1, expert_w3, expert_w2, ple_w_gate, ple_w_proj):
    batch, seq, d = x.shape
    depth = p.shape[0]
    m = batch * seq
    alpha = (2.0 * depth) ** 0.25
    lb_p = jax.nn.softmax(hgrn_gamma.astype(F32), axis=0)
    lb_all = jnp.cumsum(lb_p, axis=0) - lb_p[0]
    xf = x.reshape(m, d)
    pf = p.reshape(depth, m, PLE_DIM)
    n_mq = 3 * MOBA_HEADS * MOBA_DH + 2 * GLA_HEADS * GLA_DK + GLA_HEADS * GLA_DV
    for i in range(depth):
        j = i // 2
        if i % 2 == 0:
            w_in = even_w_in[j]
            n_qk = 2 * MOBA_HEADS * MOBA_DH
            n_v = MOBA_HEADS * MOBA_DH
            w_b = jnp.concatenate([w_in[:, :n_qk], w_in[:, n_qk + n_v:n_mq]], axis=1).astype(BF16)
            w_vt = w_in[:, n_qk:n_qk + n_v].T.astype(BF16)
            w_f = jnp.concatenate([w_in[:, n_mq + GLA_RANK:], w_in[:, n_mq:n_mq + GLA_RANK],
                                   jnp.zeros((d, LANES - GLA_RANK), F32)], axis=1).astype(BF16)
            hb = _proj(xf, w_b, BF16)
            hf = _proj(xf, w_f, F32)
            vt = _proj_t(xf, w_vt)
            o_a = _moba(hb, vt, rel_bias, batch, seq)
            o_b = _gla(hb, hf, gla_w_gate[j], gla_b_gate[j], gla_norm[j], batch, seq)
            xf = _outproj_ln(o_a, o_b, xf, even_w_out[j], ln_mix_g[i], ln_mix_b[i], alpha)
            xf = _swiglu_ln_ple(xf, pf[i], ffn_w1[j], ffn_w3[j], ffn_w2[j], ln_ffn_g[i], ln_ffn_b[i],
                                ple_w_gate[i], ple_w_proj[i], alpha)
        else:
            h = _proj(xf, odd_w_in[j].astype(BF16), F32)
            o_c = _hgrn(h, lb_all[i], hgrn_norm[j], batch, seq)
            o_d = _retention(h, ret_norm[j], batch, seq)
            xf = _outproj_ln(o_c, o_d, xf, odd_w_out[j], ln_mix_g[i], ln_mix_b[i], alpha)
            xf = _moe_ln_ple(xf, pf[i], router_w[j], router_b[j], expert_w1[j], expert_w3[j], expert_w2[j],
                             ln_ffn_g[i], ln_ffn_b[i], ple_w_gate[i], ple_w_proj[i], alpha)
    return xf.reshape(batch, seq, d)
```

```python
import functools
import math

import numpy as np
import jax
import jax.numpy as jnp
from jax import lax
from jax.experimental import pallas as pl
from jax.experimental.pallas import tpu as pltpu

D_MODEL = 1024
PLE_DIM = 256
LN_EPS = 1e-5
HEAD_NORM_EPS = 1e-6
MOBA_HEADS, MOBA_DH, MOBA_BLOCK, MOBA_TOPK = 8, 64, 256, 3
REL_BUCKETS, REL_MAX_DIST = 32, 128
GLA_HEADS, GLA_DK, GLA_DV, GLA_RANK, GLA_TAU = 4, 64, 128, 16, 16.0
HGRN_HEADS = 4
RET_HEADS = 4
ROPE_BASE = 10000.0
CHUNK = 64
FFN_DENSE = 2752
N_EXPERTS, TOP_K, FFN_EXPERT = 8, 2, 3584

LANES = 128
VMEM_LIMIT = 56 * 1024 * 1024
ROW_TILE = 512
SEQ_TILE = 512
MOE_ROW_TILE = 512
MOE_F_TILE = 512
FFN_F_PAD = 2816
FFN_F_TILE = 1408
NEG = -1e30
LOG2E = math.log2(math.e)
FAR_UNROLL = 4

BF16 = jnp.bfloat16
F32 = jnp.float32
NT_DIMS = (((1,), (1,)), ((), ()))
TN_DIMS = (((0,), (0,)), ((), ()))


def _cparams(sem):
    return pltpu.CompilerParams(dimension_semantics=sem, vmem_limit_bytes=VMEM_LIMIT)


def _sigmoid(x):
    return 1.0 / (1.0 + jnp.exp(-x))


def _silu(x):
    return x * _sigmoid(x)


def _layer_norm(y, g, b):
    yc = y - jnp.mean(y, axis=-1, keepdims=True)
    return yc * lax.rsqrt(jnp.mean(yc * yc, axis=-1, keepdims=True) + LN_EPS) * g + b


def _proj_kernel(x_ref, w_ref, o_ref):
    o_ref[...] = jnp.dot(x_ref[...].astype(BF16), w_ref[...],
                         preferred_element_type=F32).astype(o_ref.dtype)


def _proj(x, w, out_dtype):
    m, d = x.shape
    n = w.shape[1]
    tm = min(ROW_TILE, m)
    return pl.pallas_call(
        _proj_kernel,
        name="in_proj",
        grid=(m // tm,),
        in_specs=[pl.BlockSpec((tm, d), lambda i: (i, 0)),
                  pl.BlockSpec((d, n), lambda i: (0, 0))],
        out_specs=pl.BlockSpec((tm, n), lambda i: (i, 0)),
        out_shape=jax.ShapeDtypeStruct((m, n), out_dtype),
        compiler_params=_cparams(("parallel",)),
    )(x, w)


def _moba_kernel(c_far_ref, q_ref, k_ref, vt_ref, bias_ref, o_ref, kmean_ref, s_ref, smax_ref, *,
                 nb_pad, nb_gate):
    hp = pl.program_id(1)
    qi = pl.program_id(2)
    tq = MOBA_BLOCK

    @pl.when(qi == 0)
    def _():
        kmean_ref[...] = jnp.zeros_like(kmean_ref)

    k_own = k_ref[pl.ds(pl.multiple_of(qi * tq, tq), tq), :]
    q_tb = q_ref[...].astype(F32).T.astype(BF16)
    kmean = kmean_ref[0:nb_gate, :]
    kmean_ref[pl.ds(qi, 1), :] = jnp.mean(k_own.astype(F32), axis=0, keepdims=True)

    km_hi = kmean.astype(BF16)
    km_lo = (kmean - km_hi.astype(F32)).astype(BF16)
    kdim = lax.broadcasted_iota(jnp.int32, (nb_gate, LANES), 1)
    zero_k = jnp.zeros_like(km_hi)
    gate_lhs = jnp.concatenate(
        [jnp.concatenate([jnp.where((kdim < MOBA_DH) == (hd == 0), part, zero_k) for part in (km_hi, km_lo)],
                         axis=1) for hd in range(2)], axis=0)
    gate = jnp.dot(gate_lhs, jnp.concatenate([q_tb, q_tb], axis=0), preferred_element_type=F32)

    dim = lax.broadcasted_iota(jnp.int32, (LANES, tq), 0)
    blk_t = lax.broadcasted_iota(jnp.int32, (nb_gate, tq), 0)
    blk_tf = blk_t.astype(F32)
    blk = lax.broadcasted_iota(jnp.int32, (tq, nb_pad), 1)
    qs, qa = [], []
    for hd in range(2):
        g = jnp.where(blk_t < qi, gate[hd * nb_gate:(hd + 1) * nb_gate], -jnp.inf)
        sel = jnp.zeros((nb_gate, tq), jnp.bool_)
        for _ in range(MOBA_TOPK):
            gmax = jnp.max(g, axis=0, keepdims=True)
            idx = jnp.min(jnp.where(g == gmax, blk_tf, float(nb_pad)), axis=0, keepdims=True)
            pick = (blk_tf == idx) & (gmax > -jnp.inf)
            sel = sel | pick
            g = jnp.where(pick, -jnp.inf, g)
        qh_t = jnp.where((dim < MOBA_DH) == (hd == 0), q_tb, jnp.zeros_like(q_tb))
        rows = [qh_t, jnp.where(sel, 0.0, NEG).astype(BF16)]
        if nb_pad > nb_gate:
            rows.append(jnp.zeros((nb_pad - nb_gate, tq), BF16))
        qs.append(qh_t)
        qa.append(jnp.concatenate(rows, axis=0))

    def v_t(j, hd):
        return vt_ref[j, hd * MOBA_DH:(hd + 1) * MOBA_DH, :]

    def scores_into(slot, j):
        onehot = jnp.where(blk == j, 1.0, 0.0).astype(BF16)
        ka = jnp.concatenate([k_ref[pl.ds(pl.multiple_of(j * tq, tq), tq), :], onehot], axis=1)
        for hd in range(2):
            s = jnp.dot(ka, qa[hd], preferred_element_type=F32)
            s_ref[slot, hd] = s
            smax_ref[slot, hd] = jnp.max(s, axis=0, keepdims=True)

    def update(state, s, smax, shift, vt):
        m, l, acc = state
        m_new = jnp.maximum(m, smax + shift)
        alpha = jnp.exp2(m - m_new)
        p = jnp.exp2(s - (m_new - shift))
        l = alpha * l + jnp.sum(p, axis=0, keepdims=True)
        acc = alpha * acc + jnp.dot(vt, p.astype(BF16), preferred_element_type=F32)
        return m_new, l, acc

    s_own = [jnp.dot(k_own, qs[hd], preferred_element_type=F32) + bias_ref[hd, 0] for hd in range(2)]
    scores_into(0, 0)
    state = []
    for hd in range(2):
        m0 = jnp.max(s_own[hd], axis=0, keepdims=True)
        p = jnp.exp2(s_own[hd] - m0)
        l0 = jnp.sum(p, axis=0, keepdims=True)
        state.append((m0, l0, jnp.dot(v_t(qi, hd), p.astype(BF16), preferred_element_type=F32)))

    n_far = qi - 1

    last = pl.num_programs(2) - 1

    def far_group(jj, st):
        a = FAR_UNROLL * jj
        for i in range(FAR_UNROLL):
            scores_into((i + 1) % 2, jnp.minimum(a + i + 1, last))
            shift = [jnp.where(a + i < n_far, c_far_ref[2 * hp + hd], NEG) for hd in range(2)]
            st = tuple(update(st[hd], s_ref[i % 2, hd], smax_ref[i % 2, hd], shift[hd],
                              v_t(jnp.minimum(a + i, last), hd)) for hd in range(2))
        return st

    state = lax.fori_loop(0, (n_far + FAR_UNROLL - 1) // FAR_UNROLL, far_group, tuple(state))
    has_adj = qi >= 1
    j_adj = jnp.maximum(qi - 1, 0)
    scores_into(0, j_adj)
    out = []
    for hd in range(2):
        s = s_ref[0, hd] + bias_ref[hd, 1]
        _, l, acc = update(state[hd], s, jnp.max(s, axis=0, keepdims=True), jnp.where(has_adj, 0.0, NEG),
                           v_t(j_adj, hd))
        out.append(acc / l)
    o_ref[...] = jnp.concatenate(out, axis=0).T.astype(o_ref.dtype)


def _t5_bucket(dist):
    n = jnp.maximum(dist, 0)
    max_exact = REL_BUCKETS // 2
    nf = jnp.maximum(n, 1).astype(F32)
    large = max_exact + (jnp.log(nf / max_exact) / math.log(REL_MAX_DIST / max_exact)
                         * (REL_BUCKETS - max_exact)).astype(jnp.int32)
    large = jnp.minimum(large, REL_BUCKETS - 1)
    return jnp.where(n < max_exact, n, large)


def _moba_bias_tables(rel_bias):
    tbl = rel_bias.astype(F32).T
    kpos = jnp.arange(MOBA_BLOCK)[:, None]
    qpos = jnp.arange(MOBA_BLOCK)[None, :]
    d_own = qpos - kpos
    own = jnp.where(d_own >= 0, tbl[:, _t5_bucket(d_own)], NEG)
    adj = tbl[:, _t5_bucket(d_own + MOBA_BLOCK)]
    return jnp.stack([own, adj], axis=1) * LOG2E, tbl[:, REL_BUCKETS - 1] * LOG2E


def _proj_t_kernel(x_ref, wt_ref, o_ref):
    o_ref[0] = lax.dot_general(wt_ref[...], x_ref[...].astype(BF16), NT_DIMS,
                               preferred_element_type=F32).astype(o_ref.dtype)


def _proj_t(x, wt):
    m, d = x.shape
    n = wt.shape[0]
    return pl.pallas_call(
        _proj_t_kernel,
        name="in_proj_t",
        grid=(m // MOBA_BLOCK,),
        in_specs=[pl.BlockSpec((MOBA_BLOCK, d), lambda i: (i, 0)),
                  pl.BlockSpec((n, d), lambda i: (0, 0))],
        out_specs=pl.BlockSpec((1, n, MOBA_BLOCK), lambda i: (i, 0, 0)),
        out_shape=jax.ShapeDtypeStruct((m // MOBA_BLOCK, n, MOBA_BLOCK), BF16),
        compiler_params=_cparams(("parallel",)),
    )(x, wt)


def _moba(hb, vt, rel_bias, batch, seq):
    m = hb.shape[0]
    nq = seq // MOBA_BLOCK
    nb_pad = LANES * (-(-nq // LANES))
    bias, c_far = _moba_bias_tables(rel_bias)
    hp_n = MOBA_HEADS // 2
    nb_gate = min(nb_pad, 16 * (-(-nq // 16)))
    kern = functools.partial(_moba_kernel, nb_pad=nb_pad, nb_gate=nb_gate)
    return pl.pallas_call(
        kern,
        name="moba",
        grid=(batch, hp_n, nq),
        in_specs=[pl.BlockSpec(memory_space=pltpu.SMEM),
                  pl.BlockSpec((MOBA_BLOCK, LANES), lambda b, h, i: (b * nq + i, h)),
                  pl.BlockSpec((seq, LANES), lambda b, h, i: (b, hp_n + h)),
                  pl.BlockSpec((nq, LANES, MOBA_BLOCK), lambda b, h, i: (b, h, 0)),
                  pl.BlockSpec((2, 2, MOBA_BLOCK, MOBA_BLOCK), lambda b, h, i: (h, 0, 0, 0))],
        out_specs=pl.BlockSpec((MOBA_BLOCK, LANES), lambda b, h, i: (b * nq + i, h)),
        out_shape=jax.ShapeDtypeStruct((m, MOBA_HEADS * MOBA_DH), BF16),
        scratch_shapes=[pltpu.VMEM((nb_pad, LANES), F32),
                        pltpu.VMEM((2, 2, MOBA_BLOCK, MOBA_BLOCK), F32),
                        pltpu.VMEM((2, 2, 1, MOBA_BLOCK), F32)],
        compiler_params=_cparams(("parallel", "parallel", "arbitrary")),
    )(c_far, hb, hb, vt, bias)


LEVELS = (32, 16, 8, 4, 2, 1)


def _chunk_cumsum(la):
    t = la.shape[0]
    row = lax.broadcasted_iota(jnp.int32, la.shape, 0) % CHUNK
    b = la
    sh = 1
    while sh < CHUNK:
        b = b + jnp.where(row >= sh, pltpu.roll(b, sh, 0), 0.0)
        sh *= 2
    return b


def _level_reference(b, h):
    t, w = b.shape
    if 2 * h >= 8:
        b3 = b.reshape(t // (2 * h), 2 * h, w)
        return jnp.broadcast_to(b3[:, h - 1:h, :], b3.shape).reshape(t, w)
    row = lax.broadcasted_iota(jnp.int32, b.shape, 0)
    if h == 2:
        r4 = row % 4
        return jnp.where(r4 == 0, pltpu.roll(b, t - 1, 0),
                         jnp.where(r4 == 1, b,
                                   jnp.where(r4 == 2, pltpu.roll(b, 1, 0), pltpu.roll(b, 2, 0))))
    return jnp.where(row % 2 == 1, pltpu.roll(b, 1, 0), b)


def _pair_masks():
    t = lax.broadcasted_iota(jnp.int32, (CHUNK, 2 * CHUNK), 0)
    s = lax.broadcasted_iota(jnp.int32, (CHUNK, 2 * CHUNK), 1) % CHUNK
    masks = []
    for h in LEVELS:
        masks.append((t // (2 * h) == s // (2 * h)) & ((t // h) % 2 == 1) & ((s // h) % 2 == 0))
    masks.append(t == s)
    return masks


def _stack_heads(x, split):
    lane = lax.broadcasted_iota(jnp.int32, x.shape, 1)
    z = jnp.zeros_like(x)
    return jnp.concatenate([jnp.where(lane < split, x, z), jnp.where(lane >= split, x, z)], axis=0)


def _gated_linear_tile(q, k, v, la, st_ref):
    t = q.shape[0]
    b = _chunk_cumsum(la)
    q_lv, k_lv = [], []
    for h in LEVELS:
        w = jnp.exp(-jnp.abs(b - _level_reference(b, h)))
        q_lv.append((q * w).astype(BF16))
        k_lv.append((k * w).astype(BF16))
    q_lv.append(q.astype(BF16))
    k_lv.append(k.astype(BF16))
    b3 = b.reshape(t // CHUNK, CHUNK, LANES)
    b_last = jnp.broadcast_to(b3[:, CHUNK - 1:CHUNK, :], b3.shape).reshape(t, LANES)
    q_in = (q * jnp.exp(b)).astype(BF16)
    k_out = (k * jnp.exp(b_last - b)).astype(BF16)
    masks = _pair_masks()
    e = lax.broadcasted_iota(jnp.int32, st_ref.shape, 0)
    d = lax.broadcasted_iota(jnp.int32, st_ref.shape, 1)
    same_head = (e < GLA_DV) == (d < GLA_DK)
    outs = []
    for c in range(t // CHUNK):
        rows = slice(c * CHUNK, (c + 1) * CHUNK)
        a = jnp.zeros((CHUNK, 2 * CHUNK), F32)
        for lv in range(len(masks)):
            p = lax.dot_general(q_lv[lv][rows], _stack_heads(k_lv[lv][rows], GLA_DK), NT_DIMS,
                                preferred_element_type=F32)
            a = a + jnp.where(masks[lv], p, 0.0)
        vc = v[rows]
        o = jnp.dot(a.astype(BF16), _stack_heads(vc, GLA_DV), preferred_element_type=F32)
        st = st_ref[...]
        o = o + lax.dot_general(q_in[rows], st.astype(BF16), NT_DIMS, preferred_element_type=F32)
        upd = lax.dot_general(vc, k_out[rows], TN_DIMS, preferred_element_type=F32)
        st_ref[...] = st * jnp.exp(b[(c + 1) * CHUNK - 1:(c + 1) * CHUNK, :]) + jnp.where(same_head, upd, 0.0)
        outs.append(o)
    return jnp.concatenate(outs, axis=0)


def _head_rms_gate(o, norm, gate):
    parts = []
    for hd in range(2):
        oh = o[:, hd * GLA_DV:(hd + 1) * GLA_DV]
        parts.append(oh * lax.rsqrt(jnp.mean(oh * oh, axis=-1, keepdims=True) + HEAD_NORM_EPS))
    return jnp.concatenate(parts, axis=1) * norm * _silu(gate)


def _log_sigmoid(x):
    return jnp.minimum(x, 0.0) - jnp.log1p(jnp.exp(-jnp.abs(x)))


def _gla_kernel(q_ref, k_ref, v_ref, ga_ref, gr_ref, wg_ref, bg_ref, norm_ref, o_ref, st_ref):
    @pl.when(pl.program_id(2) == 0)
    def _():
        st_ref[...] = jnp.zeros_like(st_ref)

    gate_in = jnp.dot(ga_ref[...].astype(BF16), wg_ref[...], preferred_element_type=F32) + bg_ref[...]
    la = _log_sigmoid(gate_in) / GLA_TAU
    q = q_ref[...].astype(F32) * (GLA_DK ** -0.5)
    o = _gated_linear_tile(q, k_ref[...].astype(F32), v_ref[...], la, st_ref)
    o_ref[...] = _head_rms_gate(o, norm_ref[...], gr_ref[...]).astype(o_ref.dtype)


def _gla(hb, hf, w_gate, b_gate, norm, batch, seq):
    m = hb.shape[0]
    tt = min(SEQ_TILE, seq)
    nt = seq // tt
    hp_n = GLA_HEADS // 2
    wg = jnp.zeros((LANES, GLA_HEADS * GLA_DK), BF16).at[:GLA_RANK].set(w_gate.astype(BF16))
    row = lambda b, h, i: b * nt + i
    return pl.pallas_call(
        _gla_kernel,
        name="gla",
        grid=(batch, hp_n, nt),
        in_specs=[pl.BlockSpec((tt, LANES), lambda b, h, i: (row(b, h, i), 8 + h)),
                  pl.BlockSpec((tt, LANES), lambda b, h, i: (row(b, h, i), 10 + h)),
                  pl.BlockSpec((tt, 2 * GLA_DV), lambda b, h, i: (row(b, h, i), 6 + h)),
                  pl.BlockSpec((tt, LANES), lambda b, h, i: (row(b, h, i), 4)),
                  pl.BlockSpec((tt, 2 * GLA_DV), lambda b, h, i: (row(b, h, i), h)),
                  pl.BlockSpec((LANES, LANES), lambda b, h, i: (0, h)),
                  pl.BlockSpec((1, LANES), lambda b, h, i: (0, h)),
                  pl.BlockSpec((1, 2 * GLA_DV), lambda b, h, i: (0, h))],
        out_specs=pl.BlockSpec((tt, 2 * GLA_DV), lambda b, h, i: (row(b, h, i), h)),
        out_shape=jax.ShapeDtypeStruct((m, GLA_HEADS * GLA_DV), BF16),
        scratch_shapes=[pltpu.VMEM((2 * GLA_DV, LANES), F32)],
        compiler_params=_cparams(("parallel", "parallel", "arbitrary")),
    )(hb, hb, hb, hf, hf, wg, b_gate.reshape(1, -1), norm.reshape(1, -1))


def _hgrn_kernel(q_ref, f_ref, v_ref, g_ref, lb_ref, norm_ref, o_ref, st_ref):
    @pl.when(pl.program_id(2) == 0)
    def _():
        st_ref[...] = jnp.zeros_like(st_ref)

    lb = lb_ref[...]
    f = lb + (1.0 - lb) * _sigmoid(f_ref[...])
    o = _gated_linear_tile(q_ref[...], 1.0 - f, v_ref[...].astype(BF16), jnp.log(f), st_ref)
    o_ref[...] = _head_rms_gate(o, norm_ref[...], g_ref[...]).astype(o_ref.dtype)


def _hgrn(h, lb, norm, batch, seq):
    m = h.shape[0]
    tt = min(SEQ_TILE, seq)
    nt = seq // tt
    row = lambda b, hh, i: b * nt + i
    return pl.pallas_call(
        _hgrn_kernel,
        name="hgrn",
        grid=(batch, HGRN_HEADS // 2, nt),
        in_specs=[pl.BlockSpec((tt, LANES), lambda b, hh, i: (row(b, hh, i), hh)),
                  pl.BlockSpec((tt, LANES), lambda b, hh, i: (row(b, hh, i), 2 + hh)),
                  pl.BlockSpec((tt, 2 * GLA_DV), lambda b, hh, i: (row(b, hh, i), 2 + hh)),
                  pl.BlockSpec((tt, 2 * GLA_DV), lambda b, hh, i: (row(b, hh, i), 4 + hh)),
                  pl.BlockSpec((1, LANES), lambda b, hh, i: (0, hh)),
                  pl.BlockSpec((1, 2 * GLA_DV), lambda b, hh, i: (0, hh))],
        out_specs=pl.BlockSpec((tt, 2 * GLA_DV), lambda b, hh, i: (row(b, hh, i), hh)),
        out_shape=jax.ShapeDtypeStruct((m, HGRN_HEADS * GLA_DV), BF16),
        scratch_shapes=[pltpu.VMEM((2 * GLA_DV, LANES), F32)],
        compiler_params=_cparams(("parallel", "parallel", "arbitrary")),
    )(h, h, h, h, lb.reshape(1, -1), norm.reshape(1, -1))


def _rotate(x, cos, sin_signed):
    lane = lax.broadcasted_iota(jnp.int32, x.shape, 1)
    half = GLA_DK // 2
    swapped = jnp.where(lane % GLA_DK < half, pltpu.roll(x, LANES - half, 1), pltpu.roll(x, half, 1))
    return x * cos + swapped * sin_signed


def _ret_kernel(q_ref, k_ref, v_ref, g_ref, cos_ref, sin_ref, dmask_ref, zeta_ref, xi_ref, gch_ref,
                norm_ref, o_ref, st_ref):
    @pl.when(pl.program_id(2) == 0)
    def _():
        st_ref[...] = jnp.zeros_like(st_ref)

    t = q_ref.shape[0]
    cos, sin = cos_ref[...], sin_ref[...]
    q = _rotate(q_ref[...], cos, sin)
    k = _rotate(k_ref[...], cos, sin) * (GLA_DK ** -0.5)
    v = v_ref[...].astype(BF16)
    nc = t // CHUNK
    zeta = jnp.broadcast_to(zeta_ref[0][None], (nc, CHUNK, LANES)).reshape(t, LANES)
    xi = jnp.broadcast_to(xi_ref[0][None], (nc, CHUNK, LANES)).reshape(t, LANES)
    q_b, k_b = q.astype(BF16), k.astype(BF16)
    q_in = (q * xi).astype(BF16)
    k_out = (k * zeta).astype(BF16)
    dmask = dmask_ref[0]
    g_chunk = gch_ref[0]
    e = lax.broadcasted_iota(jnp.int32, st_ref.shape, 0)
    d = lax.broadcasted_iota(jnp.int32, st_ref.shape, 1)
    same_head = (e < GLA_DV) == (d < GLA_DK)
    outs = []
    for c in range(nc):
        rows = slice(c * CHUNK, (c + 1) * CHUNK)
        a = lax.dot_general(q_b[rows], _stack_heads(k_b[rows], GLA_DK), NT_DIMS,
                            preferred_element_type=F32) * dmask
        vc = v[rows]
        o = jnp.dot(a.astype(BF16), _stack_heads(vc, GLA_DV), preferred_element_type=F32)
        st = st_ref[...]
        o = o + lax.dot_general(q_in[rows], st.astype(BF16), NT_DIMS, preferred_element_type=F32)
        upd = lax.dot_general(vc, k_out[rows], TN_DIMS, preferred_element_type=F32)
        st_ref[...] = st * g_chunk + jnp.where(same_head, upd, 0.0)
        outs.append(o)
    o = jnp.concatenate(outs, axis=0)
    parts = []
    for hd in range(2):
        oh = o[:, hd * GLA_DV:(hd + 1) * GLA_DV]
        oh = oh - jnp.mean(oh, axis=-1, keepdims=True)
        parts.append(oh * lax.rsqrt(jnp.mean(oh * oh, axis=-1, keepdims=True) + HEAD_NORM_EPS))
    o_ref[...] = (jnp.concatenate(parts, axis=1) * norm_ref[...] * _silu(g_ref[...])).astype(o_ref.dtype)


def _retention_tables(seq):
    half = GLA_DK // 2
    inv = ROPE_BASE ** (-jnp.arange(half, dtype=F32) / half)
    ang = jnp.arange(seq, dtype=F32)[:, None] * inv[None, :]
    cos, sin = jnp.cos(ang), jnp.sin(ang)
    cos_t = jnp.tile(cos, (1, 4))
    sin_t = jnp.tile(jnp.concatenate([-sin, sin], axis=1), (1, 2))
    log_g = jnp.log1p(-jnp.exp2(-5.0 - jnp.arange(RET_HEADS, dtype=F32)))
    pos = jnp.arange(CHUNK, dtype=F32)
    rel = pos[:, None] - pos[None, :]
    dmask = jnp.where(rel >= 0, jnp.exp(jnp.maximum(rel, 0.0)[None] * log_g[:, None, None]), 0.0)
    zeta = jnp.exp((CHUNK - 1 - pos)[None, :] * log_g[:, None])
    xi = jnp.exp((pos + 1.0)[None, :] * log_g[:, None])
    g_chunk = jnp.exp(CHUNK * log_g)
    hp_n = RET_HEADS // 2
    pair = lambda x: x.reshape(hp_n, 2, *x.shape[1:])
    dmask_p = jnp.concatenate([pair(dmask)[:, 0], pair(dmask)[:, 1]], axis=-1)
    lanes = lambda x: jnp.repeat(pair(x), GLA_DK, axis=1)
    zeta_p = jnp.swapaxes(lanes(zeta), 1, 2)
    xi_p = jnp.swapaxes(lanes(xi), 1, 2)
    gch_p = lanes(g_chunk)[:, None, :]
    return cos_t, sin_t, dmask_p, zeta_p, xi_p, gch_p


def _retention(h, norm, batch, seq):
    m = h.shape[0]
    tt = min(SEQ_TILE, seq)
    nt = seq // tt
    cos_t, sin_t, dmask, zeta, xi, gch = _retention_tables(seq)
    row = lambda b, hh, i: b * nt + i
    const3 = lambda shape: pl.BlockSpec((1,) + shape, lambda b, hh, i: (hh, 0, 0))
    return pl.pallas_call(
        _ret_kernel,
        name="retention",
        grid=(batch, RET_HEADS // 2, nt),
        in_specs=[pl.BlockSpec((tt, LANES), lambda b, hh, i: (row(b, hh, i), 12 + hh)),
                  pl.BlockSpec((tt, LANES), lambda b, hh, i: (row(b, hh, i), 14 + hh)),
                  pl.BlockSpec((tt, 2 * GLA_DV), lambda b, hh, i: (row(b, hh, i), 8 + hh)),
                  pl.BlockSpec((tt, 2 * GLA_DV), lambda b, hh, i: (row(b, hh, i), 10 + hh)),
                  pl.BlockSpec((tt, LANES), lambda b, hh, i: (i, 0)),
                  pl.BlockSpec((tt, LANES), lambda b, hh, i: (i, 0)),
                  const3((CHUNK, 2 * CHUNK)), const3((CHUNK, LANES)), const3((CHUNK, LANES)),
                  const3((1, LANES)),
                  pl.BlockSpec((1, 2 * GLA_DV), lambda b, hh, i: (0, hh))],
        out_specs=pl.BlockSpec((tt, 2 * GLA_DV), lambda b, hh, i: (row(b, hh, i), hh)),
        out_shape=jax.ShapeDtypeStruct((m, RET_HEADS * GLA_DV), BF16),
        scratch_shapes=[pltpu.VMEM((2 * GLA_DV, LANES), F32)],
        compiler_params=_cparams(("parallel", "parallel", "arbitrary")),
    )(h, h, h, h, cos_t, sin_t, dmask, zeta, xi, gch, norm.reshape(1, -1))


def _outproj_kernel(oa_ref, ob_ref, x_ref, wa_ref, wb_ref, g_ref, b_ref, o_ref, *, alpha):
    y = jnp.dot(oa_ref[...], wa_ref[...], preferred_element_type=F32)
    y = y + jnp.dot(ob_ref[...], wb_ref[...], preferred_element_type=F32)
    o_ref[...] = _layer_norm(alpha * x_ref[...] + y, g_ref[...], b_ref[...])


def _outproj_ln(oa, ob, x, w_out, g, b, alpha):
    m, d = x.shape
    tm = min(ROW_TILE, m)
    wa = w_out[:oa.shape[1]].astype(BF16)
    wb = w_out[oa.shape[1]:].astype(BF16)
    full = lambda a: pl.BlockSpec(a.shape, lambda i: (0, 0))
    rows = lambda a: pl.BlockSpec((tm, a.shape[1]), lambda i: (i, 0))
    g2, b2 = g.reshape(1, -1), b.reshape(1, -1)
    return pl.pallas_call(
        functools.partial(_outproj_kernel, alpha=alpha),
        name="out_proj_ln",
        grid=(m // tm,),
        in_specs=[rows(oa), rows(ob), rows(x), full(wa), full(wb), full(g2), full(b2)],
        out_specs=rows(x),
        out_shape=jax.ShapeDtypeStruct((m, d), F32),
        compiler_params=_cparams(("parallel",)),
    )(oa, ob, x, wa, wb, g2, b2)


def _ln_ple(x, ffn, p, g, b, wg, wp, alpha):
    y = _layer_norm(alpha * x + ffn, g, b)
    gate = _sigmoid(jnp.dot(y.astype(BF16), wg, preferred_element_type=F32))
    return y + gate * jnp.dot(p.astype(BF16), wp, preferred_element_type=F32)


def _swiglu_kernel(x_ref, p_ref, w1_ref, w3_ref, w2_ref, g_ref, b_ref, wg_ref, wp_ref, o_ref,
                   xb_ref, acc_ref, *, alpha):
    f = pl.program_id(1)

    @pl.when(f == 0)
    def _():
        xb_ref[...] = x_ref[...].astype(BF16)
        acc_ref[...] = jnp.zeros_like(acc_ref)

    xb = xb_ref[...]
    h1 = jnp.dot(xb, w1_ref[...], preferred_element_type=F32)
    h3 = jnp.dot(xb, w3_ref[...], preferred_element_type=F32)
    acc_ref[...] += jnp.dot((_silu(h1) * h3).astype(BF16), w2_ref[...], preferred_element_type=F32)

    @pl.when(f == pl.num_programs(1) - 1)
    def _():
        o_ref[...] = _ln_ple(x_ref[...], acc_ref[...], p_ref[...], g_ref[...], b_ref[...],
                             wg_ref[...], wp_ref[...], alpha)


def _swiglu_ln_ple(x, p, w1, w3, w2, g, b, wg, wp, alpha):
    m, d = x.shape
    tm = min(ROW_TILE, m)
    pad = FFN_F_PAD - w1.shape[1]
    w1p = jnp.pad(w1.astype(BF16), ((0, 0), (0, pad)))
    w3p = jnp.pad(w3.astype(BF16), ((0, 0), (0, pad)))
    w2p = jnp.pad(w2.astype(BF16), ((0, pad), (0, 0)))
    nf = FFN_F_PAD // FFN_F_TILE
    g2, b2 = g.reshape(1, -1), b.reshape(1, -1)
    wgb, wpb = wg.astype(BF16), wp.astype(BF16)
    full = lambda a: pl.BlockSpec(a.shape, lambda i, f: (0, 0))
    rows = lambda a: pl.BlockSpec((tm, a.shape[1]), lambda i, f: (i, 0))
    return pl.pallas_call(
        functools.partial(_swiglu_kernel, alpha=alpha),
        name="swiglu_ln_ple",
        grid=(m // tm, nf),
        in_specs=[rows(x), rows(p),
                  pl.BlockSpec((d, FFN_F_TILE), lambda i, f: (0, f)),
                  pl.BlockSpec((d, FFN_F_TILE), lambda i, f: (0, f)),
                  pl.BlockSpec((FFN_F_TILE, d), lambda i, f: (f, 0)),
                  full(g2), full(b2), full(wgb), full(wpb)],
        out_specs=rows(x),
        out_shape=jax.ShapeDtypeStruct((m, d), F32),
        scratch_shapes=[pltpu.VMEM((tm, d), BF16), pltpu.VMEM((tm, d), F32)],
        compiler_params=_cparams(("parallel", "arbitrary")),
    )(x, p, w1p, w3p, w2p, g2, b2, wgb, wpb)


def _router_kernel(x_ref, w_ref, b_ref, e_ref, g_ref):
    logits = jnp.dot(x_ref[...], w_ref[...], preferred_element_type=F32,
                     precision=lax.Precision.HIGHEST) + b_ref[...]
    lane = lax.broadcasted_iota(jnp.int32, logits.shape, 1)
    lane_f = lane.astype(F32)
    logits = jnp.where(lane < N_EXPERTS, logits, -jnp.inf)
    m1 = jnp.max(logits, axis=1, keepdims=True)
    i1 = jnp.min(jnp.where(logits == m1, lane_f, float(LANES)), axis=1, keepdims=True)
    rest = jnp.where(lane_f == i1, -jnp.inf, logits)
    m2 = jnp.max(rest, axis=1, keepdims=True)
    i2 = jnp.min(jnp.where(rest == m2, lane_f, float(LANES)), axis=1, keepdims=True)
    z = jnp.exp(m2 - m1)
    g1 = 1.0 / (1.0 + z)
    e_ref[...] = jnp.where(lane == 0, i1, jnp.where(lane == 1, i2, 0.0)).astype(jnp.int32)
    g_ref[...] = jnp.where(lane == 0, g1, jnp.where(lane == 1, z * g1, 0.0))


def _router(x, w_router, b_router):
    m, d = x.shape
    tm = min(ROW_TILE, m)
    w = jnp.zeros((d, LANES), F32).at[:, :N_EXPERTS].set(w_router)
    b = jnp.zeros((1, LANES), F32).at[0, :N_EXPERTS].set(b_router)
    e, g = pl.pallas_call(
        _router_kernel,
        name="router",
        grid=(m // tm,),
        in_specs=[pl.BlockSpec((tm, d), lambda i: (i, 0)),
                  pl.BlockSpec((d, LANES), lambda i: (0, 0)),
                  pl.BlockSpec((1, LANES), lambda i: (0, 0))],
        out_specs=[pl.BlockSpec((tm, LANES), lambda i: (i, 0)),
                   pl.BlockSpec((tm, LANES), lambda i: (i, 0))],
        out_shape=[jax.ShapeDtypeStruct((m, LANES), jnp.int32),
                   jax.ShapeDtypeStruct((m, LANES), F32)],
        compiler_params=_cparams(("parallel",)),
    )(x, w, b)
    return e[:, :TOP_K], g[:, :TOP_K]


def _experts_kernel(be_ref, x_ref, w1_ref, w3_ref, w2_ref, o_ref, acc_ref):
    f = pl.program_id(1)

    @pl.when(f == 0)
    def _():
        acc_ref[...] = jnp.zeros_like(acc_ref)

    xb = x_ref[...]
    h1 = jnp.dot(xb, w1_ref[0], preferred_element_type=F32)
    h3 = jnp.dot(xb, w3_ref[0], preferred_element_type=F32)
    acc_ref[...] += jnp.dot((_silu(h1) * h3).astype(BF16), w2_ref[0], preferred_element_type=F32)

    @pl.when(f == pl.num_programs(1) - 1)
    def _():
        o_ref[...] = acc_ref[...]


def _experts(xg, blk_e, w1, w3, w2):
    r, d = xg.shape
    fe = w1.shape[2]
    nf = fe // MOE_F_TILE
    grid_spec = pltpu.PrefetchScalarGridSpec(
        num_scalar_prefetch=1,
        grid=(r // MOE_ROW_TILE, nf),
        in_specs=[pl.BlockSpec((MOE_ROW_TILE, d), lambda i, f, be: (i, 0)),
                  pl.BlockSpec((1, d, MOE_F_TILE), lambda i, f, be: (be[i], 0, f)),
                  pl.BlockSpec((1, d, MOE_F_TILE), lambda i, f, be: (be[i], 0, f)),
                  pl.BlockSpec((1, MOE_F_TILE, d), lambda i, f, be: (be[i], f, 0))],
        out_specs=pl.BlockSpec((MOE_ROW_TILE, d), lambda i, f, be: (i, 0)),
        scratch_shapes=[pltpu.VMEM((MOE_ROW_TILE, d), F32)],
    )
    return pl.pallas_call(
        _experts_kernel,
        name="experts",
        grid_spec=grid_spec,
        out_shape=jax.ShapeDtypeStruct((r, d), F32),
        compiler_params=_cparams(("parallel", "arbitrary")),
    )(blk_e, xg, w1, w3, w2)


def _combine_kernel(x_ref, y0_ref, y1_ref, gt_ref, p_ref, g_ref, b_ref, wg_ref, wp_ref, o_ref, *, alpha):
    gt = gt_ref[...]
    ffn = y0_ref[...] * gt[:, 0:1] + y1_ref[...] * gt[:, 1:2]
    o_ref[...] = _ln_ple(x_ref[...], ffn, p_ref[...], g_ref[...], b_ref[...], wg_ref[...], wp_ref[...],
                         alpha)


def _combine_ln_ple(x, y0, y1, gates, p, g, b, wg, wp, alpha):
    m, d = x.shape
    tm = min(ROW_TILE, m)
    g2, b2 = g.reshape(1, -1), b.reshape(1, -1)
    wgb, wpb = wg.astype(BF16), wp.astype(BF16)
    gt = jnp.zeros((m, LANES), F32).at[:, :TOP_K].set(gates)
    full = lambda a: pl.BlockSpec(a.shape, lambda i: (0, 0))
    rows = lambda a: pl.BlockSpec((tm, a.shape[1]), lambda i: (i, 0))
    return pl.pallas_call(
        functools.partial(_combine_kernel, alpha=alpha),
        name="combine_ln_ple",
        grid=(m // tm,),
        in_specs=[rows(x), rows(y0), rows(y1), rows(gt), rows(p), full(g2), full(b2), full(wgb), full(wpb)],
        out_specs=rows(x),
        out_shape=jax.ShapeDtypeStruct((m, d), F32),
        compiler_params=_cparams(("parallel",)),
    )(x, y0, y1, gt, p, g2, b2, wgb, wpb)


def _moe_ln_ple(x, p, w_router, b_router, w1, w3, w2, g, b, wg, wp, alpha):
    m, d = x.shape
    top_e, gates = _router(x, w_router, b_router)
    n_asg = m * TOP_K
    e_flat = top_e.reshape(-1)
    order = jnp.argsort(e_flat)
    e_s = e_flat[order]
    tok_s = (order // TOP_K).astype(jnp.int32)
    counts = jnp.zeros((N_EXPERTS,), jnp.int32).at[e_flat].add(1)
    starts = jnp.cumsum(counts) - counts
    padded = (counts + MOE_ROW_TILE - 1) // MOE_ROW_TILE * MOE_ROW_TILE
    p_ends = jnp.cumsum(padded)
    p_starts = p_ends - padded
    dest = p_starts[e_s] + jnp.arange(n_asg, dtype=jnp.int32) - starts[e_s]
    n_rows = (n_asg // MOE_ROW_TILE + N_EXPERTS) * MOE_ROW_TILE
    tok_buf = jnp.zeros((n_rows,), jnp.int32).at[dest].set(tok_s)
    pos = jnp.zeros((n_asg,), jnp.int32).at[order].set(dest).reshape(m, TOP_K)
    blk_start = jnp.arange(n_rows // MOE_ROW_TILE, dtype=jnp.int32) * MOE_ROW_TILE
    blk_e = jnp.minimum(jnp.searchsorted(p_ends, blk_start, side='right'), N_EXPERTS - 1).astype(jnp.int32)
    xg = x.astype(BF16)[tok_buf]
    y = _experts(xg, blk_e, w1.astype(BF16), w3.astype(BF16), w2.astype(BF16))
    return _combine_ln_ple(x, y[pos[:, 0]], y[pos[:, 1]], gates, p, g, b, wg, wp, alpha)


def kernel(x, p, rel_bias, even_w_in, gla_w_gate, gla_b_gate, gla_norm, even_w_out, odd_w_in, hgrn_gamma,
           hgrn_norm, ret_norm, odd_w_out, ln_mix_g, ln_mix_b, ln_ffn_g, ln_ffn_b, ffn_w1, ffn_w3, ffn_w2,
           router_w, router_b, expert_w1, expert_w3, expert_w2, ple_w_gate, ple_w_proj):
    batch, seq, d = x.shape
    depth = p.shape[0]
    m = batch * seq
    alpha = (2.0 * depth) ** 0.25
    lb_p = jax.nn.softmax(hgrn_gamma.astype(F32), axis=0)
    lb_all = jnp.cumsum(lb_p, axis=0) - lb_p[0]
    xf = x.reshape(m, d)
    pf = p.reshape(depth, m, PLE_DIM)
    n_mq = 3 * MOBA_HEADS * MOBA_DH + 2 * GLA_HEADS * GLA_DK + GLA_HEADS * GLA_DV
    for i in range(depth):
        j = i // 2
        if i % 2 == 0:
            w_in = even_w_in[j]
            n_qk = 2 * MOBA_HEADS * MOBA_DH
            n_v = MOBA_HEADS * MOBA_DH
            w_q = w_in[:, :n_v] * (MOBA_DH ** -0.5 * LOG2E)
            w_b = jnp.concatenate([w_q, w_in[:, n_v:n_qk], w_in[:, n_qk + n_v:n_mq]], axis=1).astype(BF16)
            w_vt = w_in[:, n_qk:n_qk + n_v].T.astype(BF16)
            w_f = jnp.concatenate([w_in[:, n_mq + GLA_RANK:], w_in[:, n_mq:n_mq + GLA_RANK],
                                   jnp.zeros((d, LANES - GLA_RANK), F32)], axis=1).astype(BF16)
            hb = _proj(xf, w_b, BF16)
            hf = _proj(xf, w_f, F32)
            vt = _proj_t(xf, w_vt)
            o_a = _moba(hb, vt, rel_bias, batch, seq)
            o_b = _gla(hb, hf, gla_w_gate[j], gla_b_gate[j], gla_norm[j], batch, seq)
            xf = _outproj_ln(o_a, o_b, xf, even_w_out[j], ln_mix_g[i], ln_mix_b[i], alpha)
            xf = _swiglu_ln_ple(xf, pf[i], ffn_w1[j], ffn_w3[j], ffn_w2[j], ln_ffn_g[i], ln_ffn_b[i],
                                ple_w_gate[i], ple_w_proj[i], alpha)
        else:
            h = _proj(xf, odd_w_in[j].astype(BF16), F32)
            o_c = _hgrn(h, lb_all[i], hgrn_norm[j], batch, seq)
            o_d = _retention(h, ret_norm[j], batch, seq)
            xf = _outproj_ln(o_c, o_d, xf, odd_w_out[j], ln_mix_g[i], ln_mix_b[i], alpha)
            xf = _moe_ln_ple(xf, pf[i], router_w[j], router_b[j], expert_w1[j], expert_w3[j], expert_w2[j],
                             ln_ffn_g[i], ln_ffn_b[i], ple_w_gate[i], ple_w_proj[i], alpha)
    return xf.reshape(batch, seq, d)
```

```python
import functools
import math

import numpy as np
import jax
import jax.numpy as jnp
from jax import lax
from jax.experimental import pallas as pl
from jax.experimental.pallas import tpu as pltpu

D_MODEL = 1024
PLE_DIM = 256
LN_EPS = 1e-5
HEAD_NORM_EPS = 1e-6
MOBA_HEADS, MOBA_DH, MOBA_BLOCK, MOBA_TOPK = 8, 64, 256, 3
REL_BUCKETS, REL_MAX_DIST = 32, 128
GLA_HEADS, GLA_DK, GLA_DV, GLA_RANK, GLA_TAU = 4, 64, 128, 16, 16.0
HGRN_HEADS = 4
RET_HEADS = 4
ROPE_BASE = 10000.0
CHUNK = 64
FFN_DENSE = 2752
N_EXPERTS, TOP_K, FFN_EXPERT = 8, 2, 3584

LANES = 128
VMEM_LIMIT = 56 * 1024 * 1024
ROW_TILE = 512
SEQ_TILE = 512
MOE_ROW_TILE = 512
MOE_F_TILE = 1792
FFN_F_PAD = 2816
FFN_F_TILE = 1408
SWIGLU_ROWS = 256
NEG = -1e30
LOG2E = math.log2(math.e)
FAR_UNROLL = 4

BF16 = jnp.bfloat16
F32 = jnp.float32
NT_DIMS = (((1,), (1,)), ((), ()))
TN_DIMS = (((0,), (0,)), ((), ()))


def _cparams(sem):
    return pltpu.CompilerParams(dimension_semantics=sem, vmem_limit_bytes=VMEM_LIMIT)


def _sigmoid(x):
    return 1.0 / (1.0 + jnp.exp(-x))


def _silu(x):
    return x * _sigmoid(x)


def _layer_norm(y, g, b):
    yc = y - jnp.mean(y, axis=-1, keepdims=True)
    return yc * lax.rsqrt(jnp.mean(yc * yc, axis=-1, keepdims=True) + LN_EPS) * g + b


def _proj_kernel(x_ref, w_ref, o_ref):
    o_ref[...] = jnp.dot(x_ref[...].astype(BF16), w_ref[...],
                         preferred_element_type=F32).astype(o_ref.dtype)


def _proj(x, w, out_dtype):
    m, d = x.shape
    n = w.shape[1]
    tm = min(ROW_TILE, m)
    return pl.pallas_call(
        _proj_kernel,
        name="in_proj",
        grid=(m // tm,),
        in_specs=[pl.BlockSpec((tm, d), lambda i: (i, 0)),
                  pl.BlockSpec((d, n), lambda i: (0, 0))],
        out_specs=pl.BlockSpec((tm, n), lambda i: (i, 0)),
        out_shape=jax.ShapeDtypeStruct((m, n), out_dtype),
        compiler_params=_cparams(("parallel",)),
    )(x, w)


def _moba_kernel(c_far_ref, q_ref, k_ref, vt_ref, bias_ref, o_ref, kmean_ref, s_ref, smax_ref, *,
                 nb_pad, nb_gate):
    hp = pl.program_id(1)
    qi = pl.program_id(2)
    tq = MOBA_BLOCK

    @pl.when(qi == 0)
    def _():
        kmean_ref[...] = jnp.zeros_like(kmean_ref)

    k_own = k_ref[pl.ds(pl.multiple_of(qi * tq, tq), tq), :]
    q_tb = q_ref[...].astype(F32).T.astype(BF16)
    kmean = kmean_ref[0:nb_gate, :]
    kmean_ref[pl.ds(qi, 1), :] = jnp.mean(k_own.astype(F32), axis=0, keepdims=True)

    km_hi = kmean.astype(BF16)
    km_lo = (kmean - km_hi.astype(F32)).astype(BF16)
    kdim = lax.broadcasted_iota(jnp.int32, (nb_gate, LANES), 1)
    zero_k = jnp.zeros_like(km_hi)
    gate_lhs = jnp.concatenate(
        [jnp.concatenate([jnp.where((kdim < MOBA_DH) == (hd == 0), part, zero_k) for part in (km_hi, km_lo)],
                         axis=1) for hd in range(2)], axis=0)
    gate = jnp.dot(gate_lhs, jnp.concatenate([q_tb, q_tb], axis=0), preferred_element_type=F32)

    dim = lax.broadcasted_iota(jnp.int32, (LANES, tq), 0)
    blk_t = lax.broadcasted_iota(jnp.int32, (nb_gate, tq), 0)
    blk_tf = blk_t.astype(F32)
    blk = lax.broadcasted_iota(jnp.int32, (tq, nb_pad), 1)
    qs, qa = [], []
    for hd in range(2):
        g = jnp.where(blk_t < qi, gate[hd * nb_gate:(hd + 1) * nb_gate], -jnp.inf)
        sel = jnp.zeros((nb_gate, tq), jnp.bool_)
        for _ in range(MOBA_TOPK):
            gmax = jnp.max(g, axis=0, keepdims=True)
            idx = jnp.min(jnp.where(g == gmax, blk_tf, float(nb_pad)), axis=0, keepdims=True)
            pick = (blk_tf == idx) & (gmax > -jnp.inf)
            sel = sel | pick
            g = jnp.where(pick, -jnp.inf, g)
        qh_t = jnp.where((dim < MOBA_DH) == (hd == 0), q_tb, jnp.zeros_like(q_tb))
        rows = [qh_t, jnp.where(sel, 0.0, NEG).astype(BF16)]
        if nb_pad > nb_gate:
            rows.append(jnp.zeros((nb_pad - nb_gate, tq), BF16))
        qs.append(qh_t)
        qa.append(jnp.concatenate(rows, axis=0))

    def v_t(j, hd):
        return vt_ref[j, hd * MOBA_DH:(hd + 1) * MOBA_DH, :]

    def scores_into(slot, j):
        onehot = jnp.where(blk == j, 1.0, 0.0).astype(BF16)
        ka = jnp.concatenate([k_ref[pl.ds(pl.multiple_of(j * tq, tq), tq), :], onehot], axis=1)
        for hd in range(2):
            s = jnp.dot(ka, qa[hd], preferred_element_type=F32)
            s_ref[slot, hd] = s
            smax_ref[slot, hd] = jnp.max(s, axis=0, keepdims=True)

    def update(state, s, smax, shift, vt):
        m, l, acc = state
        m_new = jnp.maximum(m, smax + shift)
        alpha = jnp.exp2(m - m_new)
        p = jnp.exp2(s - (m_new - shift))
        l = alpha * l + jnp.sum(p, axis=0, keepdims=True)
        acc = alpha * acc + jnp.dot(vt, p.astype(BF16), preferred_element_type=F32)
        return m_new, l, acc

    s_own = [jnp.dot(k_own, qs[hd], preferred_element_type=F32) + bias_ref[hd, 0] for hd in range(2)]
    scores_into(0, 0)
    state = []
    for hd in range(2):
        m0 = jnp.max(s_own[hd], axis=0, keepdims=True)
        p = jnp.exp2(s_own[hd] - m0)
        l0 = jnp.sum(p, axis=0, keepdims=True)
        state.append((m0, l0, jnp.dot(v_t(qi, hd), p.astype(BF16), preferred_element_type=F32)))

    n_far = qi - 1

    last = pl.num_programs(2) - 1

    def far_group(jj, st):
        a = FAR_UNROLL * jj
        for i in range(FAR_UNROLL):
            scores_into((i + 1) % 2, jnp.minimum(a + i + 1, last))
            shift = [jnp.where(a + i < n_far, c_far_ref[2 * hp + hd], NEG) for hd in range(2)]
            st = tuple(update(st[hd], s_ref[i % 2, hd], smax_ref[i % 2, hd], shift[hd],
                              v_t(jnp.minimum(a + i, last), hd)) for hd in range(2))
        return st

    state = lax.fori_loop(0, (n_far + FAR_UNROLL - 1) // FAR_UNROLL, far_group, tuple(state))
    has_adj = qi >= 1
    j_adj = jnp.maximum(qi - 1, 0)
    scores_into(0, j_adj)
    out = []
    for hd in range(2):
        s = s_ref[0, hd] + bias_ref[hd, 1]
        _, l, acc = update(state[hd], s, jnp.max(s, axis=0, keepdims=True), jnp.where(has_adj, 0.0, NEG),
                           v_t(j_adj, hd))
        out.append(acc / l)
    o_ref[...] = jnp.concatenate(out, axis=0).T.astype(o_ref.dtype)


def _t5_bucket(dist):
    n = jnp.maximum(dist, 0)
    max_exact = REL_BUCKETS // 2
    nf = jnp.maximum(n, 1).astype(F32)
    large = max_exact + (jnp.log(nf / max_exact) / math.log(REL_MAX_DIST / max_exact)
                         * (REL_BUCKETS - max_exact)).astype(jnp.int32)
    large = jnp.minimum(large, REL_BUCKETS - 1)
    return jnp.where(n < max_exact, n, large)


def _moba_bias_tables(rel_bias):
    tbl = rel_bias.astype(F32).T
    kpos = jnp.arange(MOBA_BLOCK)[:, None]
    qpos = jnp.arange(MOBA_BLOCK)[None, :]
    d_own = qpos - kpos
    own = jnp.where(d_own >= 0, tbl[:, _t5_bucket(d_own)], NEG)
    adj = tbl[:, _t5_bucket(d_own + MOBA_BLOCK)]
    return jnp.stack([own, adj], axis=1) * LOG2E, tbl[:, REL_BUCKETS - 1] * LOG2E


def _proj_t_kernel(x_ref, wt_ref, o_ref):
    o_ref[0] = lax.dot_general(wt_ref[...], x_ref[...].astype(BF16), NT_DIMS,
                               preferred_element_type=F32).astype(o_ref.dtype)


def _proj_t(x, wt):
    m, d = x.shape
    n = wt.shape[0]
    return pl.pallas_call(
        _proj_t_kernel,
        name="in_proj_t",
        grid=(m // MOBA_BLOCK,),
        in_specs=[pl.BlockSpec((MOBA_BLOCK, d), lambda i: (i, 0)),
                  pl.BlockSpec((n, d), lambda i: (0, 0))],
        out_specs=pl.BlockSpec((1, n, MOBA_BLOCK), lambda i: (i, 0, 0)),
        out_shape=jax.ShapeDtypeStruct((m // MOBA_BLOCK, n, MOBA_BLOCK), BF16),
        compiler_params=_cparams(("parallel",)),
    )(x, wt)


def _moba(hb, vt, rel_bias, batch, seq):
    m = hb.shape[0]
    nq = seq // MOBA_BLOCK
    nb_pad = LANES * (-(-nq // LANES))
    bias, c_far = _moba_bias_tables(rel_bias)
    hp_n = MOBA_HEADS // 2
    nb_gate = min(nb_pad, 16 * (-(-nq // 16)))
    kern = functools.partial(_moba_kernel, nb_pad=nb_pad, nb_gate=nb_gate)
    return pl.pallas_call(
        kern,
        name="moba",
        grid=(batch, hp_n, nq),
        in_specs=[pl.BlockSpec(memory_space=pltpu.SMEM),
                  pl.BlockSpec((MOBA_BLOCK, LANES), lambda b, h, i: (b * nq + i, h)),
                  pl.BlockSpec((seq, LANES), lambda b, h, i: (b, hp_n + h)),
                  pl.BlockSpec((nq, LANES, MOBA_BLOCK), lambda b, h, i: (b, h, 0)),
                  pl.BlockSpec((2, 2, MOBA_BLOCK, MOBA_BLOCK), lambda b, h, i: (h, 0, 0, 0))],
        out_specs=pl.BlockSpec((MOBA_BLOCK, LANES), lambda b, h, i: (b * nq + i, h)),
        out_shape=jax.ShapeDtypeStruct((m, MOBA_HEADS * MOBA_DH), BF16),
        scratch_shapes=[pltpu.VMEM((nb_pad, LANES), F32),
                        pltpu.VMEM((2, 2, MOBA_BLOCK, MOBA_BLOCK), F32),
                        pltpu.VMEM((2, 2, 1, MOBA_BLOCK), F32)],
        compiler_params=_cparams(("parallel", "parallel", "arbitrary")),
    )(c_far, hb, hb, vt, bias)


LEVELS = (32, 16, 8, 4, 2, 1)


def _chunk_cumsum(la):
    t = la.shape[0]
    row = lax.broadcasted_iota(jnp.int32, la.shape, 0) % CHUNK
    b = la
    sh = 1
    while sh < CHUNK:
        b = b + jnp.where(row >= sh, pltpu.roll(b, sh, 0), 0.0)
        sh *= 2
    return b


def _level_reference(b, h):
    t, w = b.shape
    if 2 * h >= 8:
        b3 = b.reshape(t // (2 * h), 2 * h, w)
        return jnp.broadcast_to(b3[:, h - 1:h, :], b3.shape).reshape(t, w)
    row = lax.broadcasted_iota(jnp.int32, b.shape, 0)
    if h == 2:
        r4 = row % 4
        return jnp.where(r4 == 0, pltpu.roll(b, t - 1, 0),
                         jnp.where(r4 == 1, b,
                                   jnp.where(r4 == 2, pltpu.roll(b, 1, 0), pltpu.roll(b, 2, 0))))
    return jnp.where(row % 2 == 1, pltpu.roll(b, 1, 0), b)


def _pair_masks():
    t = lax.broadcasted_iota(jnp.int32, (CHUNK, 2 * CHUNK), 0)
    s = lax.broadcasted_iota(jnp.int32, (CHUNK, 2 * CHUNK), 1) % CHUNK
    masks = []
    for h in LEVELS:
        masks.append((t // (2 * h) == s // (2 * h)) & ((t // h) % 2 == 1) & ((s // h) % 2 == 0))
    masks.append(t == s)
    return masks


def _stack_heads(x, split):
    lane = lax.broadcasted_iota(jnp.int32, x.shape, 1)
    z = jnp.zeros_like(x)
    return jnp.concatenate([jnp.where(lane < split, x, z), jnp.where(lane >= split, x, z)], axis=0)


def _gated_linear_tile(q, k, v, la, st_ref):
    t = q.shape[0]
    b = _chunk_cumsum(la)
    q_lv, k_lv = [], []
    for h in LEVELS:
        w = jnp.exp(-jnp.abs(b - _level_reference(b, h)))
        q_lv.append((q * w).astype(BF16))
        k_lv.append((k * w).astype(BF16))
    q_lv.append(q.astype(BF16))
    k_lv.append(k.astype(BF16))
    b3 = b.reshape(t // CHUNK, CHUNK, LANES)
    b_last = jnp.broadcast_to(b3[:, CHUNK - 1:CHUNK, :], b3.shape).reshape(t, LANES)
    q_in = (q * jnp.exp(b)).astype(BF16)
    k_out = (k * jnp.exp(b_last - b)).astype(BF16)
    masks = _pair_masks()
    e = lax.broadcasted_iota(jnp.int32, st_ref.shape, 0)
    d = lax.broadcasted_iota(jnp.int32, st_ref.shape, 1)
    same_head = (e < GLA_DV) == (d < GLA_DK)
    outs = []
    for c in range(t // CHUNK):
        rows = slice(c * CHUNK, (c + 1) * CHUNK)
        a = jnp.zeros((CHUNK, 2 * CHUNK), F32)
        for lv in range(len(masks)):
            p = lax.dot_general(q_lv[lv][rows], _stack_heads(k_lv[lv][rows], GLA_DK), NT_DIMS,
                                preferred_element_type=F32)
            a = a + jnp.where(masks[lv], p, 0.0)
        vc = v[rows]
        o = jnp.dot(a.astype(BF16), _stack_heads(vc, GLA_DV), preferred_element_type=F32)
        st = st_ref[...]
        o = o + lax.dot_general(q_in[rows], st.astype(BF16), NT_DIMS, preferred_element_type=F32)
        upd = lax.dot_general(vc, k_out[rows], TN_DIMS, preferred_element_type=F32)
        st_ref[...] = st * jnp.exp(b[(c + 1) * CHUNK - 1:(c + 1) * CHUNK, :]) + jnp.where(same_head, upd, 0.0)
        outs.append(o)
    return jnp.concatenate(outs, axis=0)


def _head_rms_gate(o, norm, gate):
    parts = []
    for hd in range(2):
        oh = o[:, hd * GLA_DV:(hd + 1) * GLA_DV]
        parts.append(oh * lax.rsqrt(jnp.mean(oh * oh, axis=-1, keepdims=True) + HEAD_NORM_EPS))
    return jnp.concatenate(parts, axis=1) * norm * _silu(gate)


def _log_sigmoid(x):
    return jnp.minimum(x, 0.0) - jnp.log1p(jnp.exp(-jnp.abs(x)))


def _gla_kernel(q_ref, k_ref, v_ref, ga_ref, gr_ref, wg_ref, bg_ref, norm_ref, o_ref, st_ref):
    @pl.when(pl.program_id(2) == 0)
    def _():
        st_ref[...] = jnp.zeros_like(st_ref)

    gate_in = jnp.dot(ga_ref[...].astype(BF16), wg_ref[...], preferred_element_type=F32) + bg_ref[...]
    la = _log_sigmoid(gate_in) / GLA_TAU
    q = q_ref[...].astype(F32) * (GLA_DK ** -0.5)
    o = _gated_linear_tile(q, k_ref[...].astype(F32), v_ref[...], la, st_ref)
    o_ref[...] = _head_rms_gate(o, norm_ref[...], gr_ref[...]).astype(o_ref.dtype)


def _gla(hb, hf, w_gate, b_gate, norm, batch, seq):
    m = hb.shape[0]
    tt = min(SEQ_TILE, seq)
    nt = seq // tt
    hp_n = GLA_HEADS // 2
    wg = jnp.zeros((LANES, GLA_HEADS * GLA_DK), BF16).at[:GLA_RANK].set(w_gate.astype(BF16))
    row = lambda b, h, i: b * nt + i
    return pl.pallas_call(
        _gla_kernel,
        name="gla",
        grid=(batch, hp_n, nt),
        in_specs=[pl.BlockSpec((tt, LANES), lambda b, h, i: (row(b, h, i), 8 + h)),
                  pl.BlockSpec((tt, LANES), lambda b, h, i: (row(b, h, i), 10 + h)),
                  pl.BlockSpec((tt, 2 * GLA_DV), lambda b, h, i: (row(b, h, i), 6 + h)),
                  pl.BlockSpec((tt, LANES), lambda b, h, i: (row(b, h, i), 4)),
                  pl.BlockSpec((tt, 2 * GLA_DV), lambda b, h, i: (row(b, h, i), h)),
                  pl.BlockSpec((LANES, LANES), lambda b, h, i: (0, h)),
                  pl.BlockSpec((1, LANES), lambda b, h, i: (0, h)),
                  pl.BlockSpec((1, 2 * GLA_DV), lambda b, h, i: (0, h))],
        out_specs=pl.BlockSpec((tt, 2 * GLA_DV), lambda b, h, i: (row(b, h, i), h)),
        out_shape=jax.ShapeDtypeStruct((m, GLA_HEADS * GLA_DV), BF16),
        scratch_shapes=[pltpu.VMEM((2 * GLA_DV, LANES), F32)],
        compiler_params=_cparams(("parallel", "parallel", "arbitrary")),
    )(hb, hb, hb, hf, hf, wg, b_gate.reshape(1, -1), norm.reshape(1, -1))


def _hgrn_kernel(q_ref, f_ref, v_ref, g_ref, lb_ref, norm_ref, o_ref, st_ref):
    @pl.when(pl.program_id(2) == 0)
    def _():
        st_ref[...] = jnp.zeros_like(st_ref)

    lb = lb_ref[...]
    f = lb + (1.0 - lb) * _sigmoid(f_ref[...])
    o = _gated_linear_tile(q_ref[...], 1.0 - f, v_ref[...].astype(BF16), jnp.log(f), st_ref)
    o_ref[...] = _head_rms_gate(o, norm_ref[...], g_ref[...]).astype(o_ref.dtype)


def _hgrn(h, lb, norm, batch, seq):
    m = h.shape[0]
    tt = min(SEQ_TILE, seq)
    nt = seq // tt
    row = lambda b, hh, i: b * nt + i
    return pl.pallas_call(
        _hgrn_kernel,
        name="hgrn",
        grid=(batch, HGRN_HEADS // 2, nt),
        in_specs=[pl.BlockSpec((tt, LANES), lambda b, hh, i: (row(b, hh, i), hh)),
                  pl.BlockSpec((tt, LANES), lambda b, hh, i: (row(b, hh, i), 2 + hh)),
                  pl.BlockSpec((tt, 2 * GLA_DV), lambda b, hh, i: (row(b, hh, i), 2 + hh)),
                  pl.BlockSpec((tt, 2 * GLA_DV), lambda b, hh, i: (row(b, hh, i), 4 + hh)),
                  pl.BlockSpec((1, LANES), lambda b, hh, i: (0, hh)),
                  pl.BlockSpec((1, 2 * GLA_DV), lambda b, hh, i: (0, hh))],
        out_specs=pl.BlockSpec((tt, 2 * GLA_DV), lambda b, hh, i: (row(b, hh, i), hh)),
        out_shape=jax.ShapeDtypeStruct((m, HGRN_HEADS * GLA_DV), BF16),
        scratch_shapes=[pltpu.VMEM((2 * GLA_DV, LANES), F32)],
        compiler_params=_cparams(("parallel", "parallel", "arbitrary")),
    )(h, h, h, h, lb.reshape(1, -1), norm.reshape(1, -1))


def _rotate(x, cos, sin_signed):
    lane = lax.broadcasted_iota(jnp.int32, x.shape, 1)
    half = GLA_DK // 2
    swapped = jnp.where(lane % GLA_DK < half, pltpu.roll(x, LANES - half, 1), pltpu.roll(x, half, 1))
    return x * cos + swapped * sin_signed


def _ret_kernel(q_ref, k_ref, v_ref, g_ref, cos_ref, sin_ref, dmask_ref, zeta_ref, xi_ref, gch_ref,
                norm_ref, o_ref, st_ref):
    @pl.when(pl.program_id(2) == 0)
    def _():
        st_ref[...] = jnp.zeros_like(st_ref)

    t = q_ref.shape[0]
    cos, sin = cos_ref[...], sin_ref[...]
    q = _rotate(q_ref[...], cos, sin)
    k = _rotate(k_ref[...], cos, sin) * (GLA_DK ** -0.5)
    v = v_ref[...].astype(BF16)
    nc = t // CHUNK
    zeta = jnp.broadcast_to(zeta_ref[0][None], (nc, CHUNK, LANES)).reshape(t, LANES)
    xi = jnp.broadcast_to(xi_ref[0][None], (nc, CHUNK, LANES)).reshape(t, LANES)
    q_b, k_b = q.astype(BF16), k.astype(BF16)
    q_in = (q * xi).astype(BF16)
    k_out = (k * zeta).astype(BF16)
    dmask = dmask_ref[0]
    g_chunk = gch_ref[0]
    e = lax.broadcasted_iota(jnp.int32, st_ref.shape, 0)
    d = lax.broadcasted_iota(jnp.int32, st_ref.shape, 1)
    same_head = (e < GLA_DV) == (d < GLA_DK)
    outs = []
    for c in range(nc):
        rows = slice(c * CHUNK, (c + 1) * CHUNK)
        a = lax.dot_general(q_b[rows], _stack_heads(k_b[rows], GLA_DK), NT_DIMS,
                            preferred_element_type=F32) * dmask
        vc = v[rows]
        o = jnp.dot(a.astype(BF16), _stack_heads(vc, GLA_DV), preferred_element_type=F32)
        st = st_ref[...]
        o = o + lax.dot_general(q_in[rows], st.astype(BF16), NT_DIMS, preferred_element_type=F32)
        upd = lax.dot_general(vc, k_out[rows], TN_DIMS, preferred_element_type=F32)
        st_ref[...] = st * g_chunk + jnp.where(same_head, upd, 0.0)
        outs.append(o)
    o = jnp.concatenate(outs, axis=0)
    parts = []
    for hd in range(2):
        oh = o[:, hd * GLA_DV:(hd + 1) * GLA_DV]
        oh = oh - jnp.mean(oh, axis=-1, keepdims=True)
        parts.append(oh * lax.rsqrt(jnp.mean(oh * oh, axis=-1, keepdims=True) + HEAD_NORM_EPS))
    o_ref[...] = (jnp.concatenate(parts, axis=1) * norm_ref[...] * _silu(g_ref[...])).astype(o_ref.dtype)


def _retention_tables(seq):
    half = GLA_DK // 2
    inv = ROPE_BASE ** (-jnp.arange(half, dtype=F32) / half)
    ang = jnp.arange(seq, dtype=F32)[:, None] * inv[None, :]
    cos, sin = jnp.cos(ang), jnp.sin(ang)
    cos_t = jnp.tile(cos, (1, 4))
    sin_t = jnp.tile(jnp.concatenate([-sin, sin], axis=1), (1, 2))
    log_g = jnp.log1p(-jnp.exp2(-5.0 - jnp.arange(RET_HEADS, dtype=F32)))
    pos = jnp.arange(CHUNK, dtype=F32)
    rel = pos[:, None] - pos[None, :]
    dmask = jnp.where(rel >= 0, jnp.exp(jnp.maximum(rel, 0.0)[None] * log_g[:, None, None]), 0.0)
    zeta = jnp.exp((CHUNK - 1 - pos)[None, :] * log_g[:, None])
    xi = jnp.exp((pos + 1.0)[None, :] * log_g[:, None])
    g_chunk = jnp.exp(CHUNK * log_g)
    hp_n = RET_HEADS // 2
    pair = lambda x: x.reshape(hp_n, 2, *x.shape[1:])
    dmask_p = jnp.concatenate([pair(dmask)[:, 0], pair(dmask)[:, 1]], axis=-1)
    lanes = lambda x: jnp.repeat(pair(x), GLA_DK, axis=1)
    zeta_p = jnp.swapaxes(lanes(zeta), 1, 2)
    xi_p = jnp.swapaxes(lanes(xi), 1, 2)
    gch_p = lanes(g_chunk)[:, None, :]
    return cos_t, sin_t, dmask_p, zeta_p, xi_p, gch_p


def _retention(h, norm, batch, seq):
    m = h.shape[0]
    tt = min(SEQ_TILE, seq)
    nt = seq // tt
    cos_t, sin_t, dmask, zeta, xi, gch = _retention_tables(seq)
    row = lambda b, hh, i: b * nt + i
    const3 = lambda shape: pl.BlockSpec((1,) + shape, lambda b, hh, i: (hh, 0, 0))
    return pl.pallas_call(
        _ret_kernel,
        name="retention",
        grid=(batch, RET_HEADS // 2, nt),
        in_specs=[pl.BlockSpec((tt, LANES), lambda b, hh, i: (row(b, hh, i), 12 + hh)),
                  pl.BlockSpec((tt, LANES), lambda b, hh, i: (row(b, hh, i), 14 + hh)),
                  pl.BlockSpec((tt, 2 * GLA_DV), lambda b, hh, i: (row(b, hh, i), 8 + hh)),
                  pl.BlockSpec((tt, 2 * GLA_DV), lambda b, hh, i: (row(b, hh, i), 10 + hh)),
                  pl.BlockSpec((tt, LANES), lambda b, hh, i: (i, 0)),
                  pl.BlockSpec((tt, LANES), lambda b, hh, i: (i, 0)),
                  const3((CHUNK, 2 * CHUNK)), const3((CHUNK, LANES)), const3((CHUNK, LANES)),
                  const3((1, LANES)),
                  pl.BlockSpec((1, 2 * GLA_DV), lambda b, hh, i: (0, hh))],
        out_specs=pl.BlockSpec((tt, 2 * GLA_DV), lambda b, hh, i: (row(b, hh, i), hh)),
        out_shape=jax.ShapeDtypeStruct((m, RET_HEADS * GLA_DV), BF16),
        scratch_shapes=[pltpu.VMEM((2 * GLA_DV, LANES), F32)],
        compiler_params=_cparams(("parallel", "parallel", "arbitrary")),
    )(h, h, h, h, cos_t, sin_t, dmask, zeta, xi, gch, norm.reshape(1, -1))


def _outproj_kernel(oa_ref, ob_ref, x_ref, wa_ref, wb_ref, g_ref, b_ref, o_ref, *, alpha):
    y = jnp.dot(oa_ref[...], wa_ref[...], preferred_element_type=F32)
    y = y + jnp.dot(ob_ref[...], wb_ref[...], preferred_element_type=F32)
    o_ref[...] = _layer_norm(alpha * x_ref[...] + y, g_ref[...], b_ref[...])


def _outproj_ln(oa, ob, x, w_out, g, b, alpha):
    m, d = x.shape
    tm = min(ROW_TILE, m)
    wa = w_out[:oa.shape[1]].astype(BF16)
    wb = w_out[oa.shape[1]:].astype(BF16)
    full = lambda a: pl.BlockSpec(a.shape, lambda i: (0, 0))
    rows = lambda a: pl.BlockSpec((tm, a.shape[1]), lambda i: (i, 0))
    g2, b2 = g.reshape(1, -1), b.reshape(1, -1)
    return pl.pallas_call(
        functools.partial(_outproj_kernel, alpha=alpha),
        name="out_proj_ln",
        grid=(m // tm,),
        in_specs=[rows(oa), rows(ob), rows(x), full(wa), full(wb), full(g2), full(b2)],
        out_specs=rows(x),
        out_shape=jax.ShapeDtypeStruct((m, d), F32),
        compiler_params=_cparams(("parallel",)),
    )(oa, ob, x, wa, wb, g2, b2)


def _ln_ple(x, ffn, p, g, b, wg, wp, alpha):
    y = _layer_norm(alpha * x + ffn, g, b)
    gate = _sigmoid(jnp.dot(y.astype(BF16), wg, preferred_element_type=F32))
    return y + gate * jnp.dot(p.astype(BF16), wp, preferred_element_type=F32)


def _swiglu_accumulate(xb_ref, w1, w3, w2, acc_ref):
    n = xb_ref.shape[0] // SWIGLU_ROWS
    h = []
    for c in range(n):
        xc = xb_ref[c * SWIGLU_ROWS:(c + 1) * SWIGLU_ROWS, :]
        h.append((jnp.dot(xc, w1, preferred_element_type=F32), jnp.dot(xc, w3, preferred_element_type=F32)))
    for c in range(n):
        h1, h3 = h[c]
        acc_ref[c * SWIGLU_ROWS:(c + 1) * SWIGLU_ROWS, :] += jnp.dot(
            (_silu(h1) * h3).astype(BF16), w2, preferred_element_type=F32)


def _swiglu_kernel(x_ref, p_ref, w1_ref, w3_ref, w2_ref, g_ref, b_ref, wg_ref, wp_ref, o_ref,
                   xb_ref, acc_ref, *, alpha):
    f = pl.program_id(1)

    @pl.when(f == 0)
    def _():
        xb_ref[...] = x_ref[...].astype(BF16)
        acc_ref[...] = jnp.zeros_like(acc_ref)

    _swiglu_accumulate(xb_ref, w1_ref[...], w3_ref[...], w2_ref[...], acc_ref)

    @pl.when(f == pl.num_programs(1) - 1)
    def _():
        o_ref[...] = _ln_ple(x_ref[...], acc_ref[...], p_ref[...], g_ref[...], b_ref[...],
                             wg_ref[...], wp_ref[...], alpha)


def _swiglu_ln_ple(x, p, w1, w3, w2, g, b, wg, wp, alpha):
    m, d = x.shape
    tm = min(ROW_TILE, m)
    pad = FFN_F_PAD - w1.shape[1]
    w1p = jnp.pad(w1.astype(BF16), ((0, 0), (0, pad)))
    w3p = jnp.pad(w3.astype(BF16), ((0, 0), (0, pad)))
    w2p = jnp.pad(w2.astype(BF16), ((0, pad), (0, 0)))
    nf = FFN_F_PAD // FFN_F_TILE
    g2, b2 = g.reshape(1, -1), b.reshape(1, -1)
    wgb, wpb = wg.astype(BF16), wp.astype(BF16)
    full = lambda a: pl.BlockSpec(a.shape, lambda i, f: (0, 0))
    rows = lambda a: pl.BlockSpec((tm, a.shape[1]), lambda i, f: (i, 0))
    return pl.pallas_call(
        functools.partial(_swiglu_kernel, alpha=alpha),
        name="swiglu_ln_ple",
        grid=(m // tm, nf),
        in_specs=[rows(x), rows(p),
                  pl.BlockSpec((d, FFN_F_TILE), lambda i, f: (0, f)),
                  pl.BlockSpec((d, FFN_F_TILE), lambda i, f: (0, f)),
                  pl.BlockSpec((FFN_F_TILE, d), lambda i, f: (f, 0)),
                  full(g2), full(b2), full(wgb), full(wpb)],
        out_specs=rows(x),
        out_shape=jax.ShapeDtypeStruct((m, d), F32),
        scratch_shapes=[pltpu.VMEM((tm, d), BF16), pltpu.VMEM((tm, d), F32)],
        compiler_params=_cparams(("parallel", "arbitrary")),
    )(x, p, w1p, w3p, w2p, g2, b2, wgb, wpb)


def _router_kernel(x_ref, w_ref, b_ref, e_ref, g_ref):
    logits = jnp.dot(x_ref[...], w_ref[...], preferred_element_type=F32,
                     precision=lax.Precision.HIGHEST) + b_ref[...]
    lane = lax.broadcasted_iota(jnp.int32, logits.shape, 1)
    lane_f = lane.astype(F32)
    logits = jnp.where(lane < N_EXPERTS, logits, -jnp.inf)
    m1 = jnp.max(logits, axis=1, keepdims=True)
    i1 = jnp.min(jnp.where(logits == m1, lane_f, float(LANES)), axis=1, keepdims=True)
    rest = jnp.where(lane_f == i1, -jnp.inf, logits)
    m2 = jnp.max(rest, axis=1, keepdims=True)
    i2 = jnp.min(jnp.where(rest == m2, lane_f, float(LANES)), axis=1, keepdims=True)
    z = jnp.exp(m2 - m1)
    g1 = 1.0 / (1.0 + z)
    e_ref[...] = jnp.where(lane == 0, i1, jnp.where(lane == 1, i2, 0.0)).astype(jnp.int32)
    g_ref[...] = jnp.where(lane == 0, g1, jnp.where(lane == 1, z * g1, 0.0))


def _router(x, w_router, b_router):
    m, d = x.shape
    tm = min(ROW_TILE, m)
    w = jnp.zeros((d, LANES), F32).at[:, :N_EXPERTS].set(w_router)
    b = jnp.zeros((1, LANES), F32).at[0, :N_EXPERTS].set(b_router)
    e, g = pl.pallas_call(
        _router_kernel,
        name="router",
        grid=(m // tm,),
        in_specs=[pl.BlockSpec((tm, d), lambda i: (i, 0)),
                  pl.BlockSpec((d, LANES), lambda i: (0, 0)),
                  pl.BlockSpec((1, LANES), lambda i: (0, 0))],
        out_specs=[pl.BlockSpec((tm, LANES), lambda i: (i, 0)),
                   pl.BlockSpec((tm, LANES), lambda i: (i, 0))],
        out_shape=[jax.ShapeDtypeStruct((m, LANES), jnp.int32),
                   jax.ShapeDtypeStruct((m, LANES), F32)],
        compiler_params=_cparams(("parallel",)),
    )(x, w, b)
    return e[:, :TOP_K], g[:, :TOP_K]


def _experts_kernel(be_ref, x_ref, w1_ref, w3_ref, w2_ref, o_ref, acc_ref):
    f = pl.program_id(1)

    @pl.when(f == 0)
    def _():
        acc_ref[...] = jnp.zeros_like(acc_ref)

    _swiglu_accumulate(x_ref, w1_ref[0], w3_ref[0], w2_ref[0], acc_ref)

    @pl.when(f == pl.num_programs(1) - 1)
    def _():
        o_ref[...] = acc_ref[...]


def _experts(xg, blk_e, w1, w3, w2):
    r, d = xg.shape
    fe = w1.shape[2]
    nf = fe // MOE_F_TILE
    grid_spec = pltpu.PrefetchScalarGridSpec(
        num_scalar_prefetch=1,
        grid=(r // MOE_ROW_TILE, nf),
        in_specs=[pl.BlockSpec((MOE_ROW_TILE, d), lambda i, f, be: (i, 0)),
                  pl.BlockSpec((1, d, MOE_F_TILE), lambda i, f, be: (be[i], 0, f)),
                  pl.BlockSpec((1, d, MOE_F_TILE), lambda i, f, be: (be[i], 0, f)),
                  pl.BlockSpec((1, MOE_F_TILE, d), lambda i, f, be: (be[i], f, 0))],
        out_specs=pl.BlockSpec((MOE_ROW_TILE, d), lambda i, f, be: (i, 0)),
        scratch_shapes=[pltpu.VMEM((MOE_ROW_TILE, d), F32)],
    )
    return pl.pallas_call(
        _experts_kernel,
        name="experts",
        grid_spec=grid_spec,
        out_shape=jax.ShapeDtypeStruct((r, d), F32),
        compiler_params=_cparams(("parallel", "arbitrary")),
    )(blk_e, xg, w1, w3, w2)


def _combine_kernel(x_ref, y0_ref, y1_ref, gt_ref, p_ref, g_ref, b_ref, wg_ref, wp_ref, o_ref, *, alpha):
    gt = gt_ref[...]
    ffn = y0_ref[...] * gt[:, 0:1] + y1_ref[...] * gt[:, 1:2]
    o_ref[...] = _ln_ple(x_ref[...], ffn, p_ref[...], g_ref[...], b_ref[...], wg_ref[...], wp_ref[...],
                         alpha)


def _combine_ln_ple(x, y0, y1, gates, p, g, b, wg, wp, alpha):
    m, d = x.shape
    tm = min(ROW_TILE, m)
    g2, b2 = g.reshape(1, -1), b.reshape(1, -1)
    wgb, wpb = wg.astype(BF16), wp.astype(BF16)
    gt = jnp.zeros((m, LANES), F32).at[:, :TOP_K].set(gates)
    full = lambda a: pl.BlockSpec(a.shape, lambda i: (0, 0))
    rows = lambda a: pl.BlockSpec((tm, a.shape[1]), lambda i: (i, 0))
    return pl.pallas_call(
        functools.partial(_combine_kernel, alpha=alpha),
        name="combine_ln_ple",
        grid=(m // tm,),
        in_specs=[rows(x), rows(y0), rows(y1), rows(gt), rows(p), full(g2), full(b2), full(wgb), full(wpb)],
        out_specs=rows(x),
        out_shape=jax.ShapeDtypeStruct((m, d), F32),
        compiler_params=_cparams(("parallel",)),
    )(x, y0, y1, gt, p, g2, b2, wgb, wpb)


def _moe_ln_ple(x, p, w_router, b_router, w1, w3, w2, g, b, wg, wp, alpha):
    m, d = x.shape
    top_e, gates = _router(x, w_router, b_router)
    n_asg = m * TOP_K
    e_flat = top_e.reshape(-1)
    onehot = (jnp.arange(N_EXPERTS, dtype=jnp.int32)[:, None] == e_flat[None, :]).astype(jnp.int32)
    csum = jnp.cumsum(onehot, axis=1)
    rank = jnp.sum(csum * onehot, axis=0) - 1
    counts = csum[:, -1]
    starts = jnp.cumsum(counts) - counts
    padded = (counts + MOE_ROW_TILE - 1) // MOE_ROW_TILE * MOE_ROW_TILE
    p_ends = jnp.cumsum(padded)
    p_starts = p_ends - padded
    pos = (p_starts[e_flat] + rank).reshape(m, TOP_K)
    n_rows = (n_asg // MOE_ROW_TILE + N_EXPERTS) * MOE_ROW_TILE
    blk_start = jnp.arange(n_rows // MOE_ROW_TILE, dtype=jnp.int32) * MOE_ROW_TILE
    blk_e = jnp.minimum(jnp.searchsorted(p_ends, blk_start, side='right'), N_EXPERTS - 1).astype(jnp.int32)
    tok_s = (jnp.argsort(e_flat) // TOP_K).astype(jnp.int32)
    row = jnp.arange(n_rows, dtype=jnp.int32)
    row_e = jnp.repeat(blk_e, MOE_ROW_TILE)
    local = row - p_starts[row_e]
    tok_buf = jnp.where(local < counts[row_e], tok_s[jnp.minimum(starts[row_e] + local, n_asg - 1)], 0)
    xg = x.astype(BF16)[tok_buf]
    y = _experts(xg, blk_e, w1.astype(BF16), w3.astype(BF16), w2.astype(BF16))
    return _combine_ln_ple(x, y[pos[:, 0]], y[pos[:, 1]], gates, p, g, b, wg, wp, alpha)


def kernel(x, p, rel_bias, even_w_in, gla_w_gate, gla_b_gate, gla_norm, even_w_out, odd_w_in, hgrn_gamma,
           hgrn_norm, ret_norm, odd_w_out, ln_mix_g, ln_mix_b, ln_ffn_g, ln_ffn_b, ffn_w1, ffn_w3, ffn_w2,
           router_w, router_b, expert_w1, expert_w3, expert_w2, ple_w_gate, ple_w_proj):
    batch, seq, d = x.shape
    depth = p.shape[0]
    m = batch * seq
    alpha = (2.0 * depth) ** 0.25
    lb_p = jax.nn.softmax(hgrn_gamma.astype(F32), axis=0)
    lb_all = jnp.cumsum(lb_p, axis=0) - lb_p[0]
    xf = x.reshape(m, d)
    pf = p.reshape(depth, m, PLE_DIM)
    n_mq = 3 * MOBA_HEADS * MOBA_DH + 2 * GLA_HEADS * GLA_DK + GLA_HEADS * GLA_DV
    for i in range(depth):
        j = i // 2
        if i % 2 == 0:
            w_in = even_w_in[j]
            n_qk = 2 * MOBA_HEADS * MOBA_DH
            n_v = MOBA_HEADS * MOBA_DH
            w_q = w_in[:, :n_v] * (MOBA_DH ** -0.5 * LOG2E)
            w_b = jnp.concatenate([w_q, w_in[:, n_v:n_qk], w_in[:, n_qk + n_v:n_mq]], axis=1).astype(BF16)
            w_vt = w_in[:, n_qk:n_qk + n_v].T.astype(BF16)
            w_f = jnp.concatenate([w_in[:, n_mq + GLA_RANK:], w_in[:, n_mq:n_mq + GLA_RANK],
                                   jnp.zeros((d, LANES - GLA_RANK), F32)], axis=1).astype(BF16)
            hb = _proj(xf, w_b, BF16)
            hf = _proj(xf, w_f, F32)
            vt = _proj_t(xf, w_vt)
            o_a = _moba(hb, vt, rel_bias, batch, seq)
            o_b = _gla(hb, hf, gla_w_gate[j], gla_b_gate[j], gla_norm[j], batch, seq)
            xf = _outproj_ln(o_a, o_b, xf, even_w_out[j], ln_mix_g[i], ln_mix_b[i], alpha)
            xf = _swiglu_ln_ple(xf, pf[i], ffn_w1[j], ffn_w3[j], ffn_w2[j], ln_ffn_g[i], ln_ffn_b[i],
                                ple_w_gate[i], ple_w_proj[i], alpha)
        else:
            h = _proj(xf, odd_w_in[j].astype(BF16), F32)
            o_c = _hgrn(h, lb_all[i], hgrn_norm[j], batch, seq)
            o_d = _retention(h, ret_norm[j], batch, seq)
            xf = _outproj_ln(o_c, o_d, xf, odd_w_out[j], ln_mix_g[i], ln_mix_b[i], alpha)
            xf = _moe_ln_ple(xf, pf[i], router_w[j], router_b[j], expert_w1[j], expert_w3[j], expert_w2[j],
                             ln_ffn_g[i], ln_ffn_b[i], ple_w_gate[i], ple_w_proj[i], alpha)
    return xf.reshape(batch, seq, d)
```

```python
import functools
import math

import numpy as np
import jax
import jax.numpy as jnp
from jax import lax
from jax.experimental import pallas as pl
from jax.experimental.pallas import tpu as pltpu

D_MODEL = 1024
PLE_DIM = 256
LN_EPS = 1e-5
HEAD_NORM_EPS = 1e-6
MOBA_HEADS, MOBA_DH, MOBA_BLOCK, MOBA_TOPK = 8, 64, 256, 3
REL_BUCKETS, REL_MAX_DIST = 32, 128
GLA_HEADS, GLA_DK, GLA_DV, GLA_RANK, GLA_TAU = 4, 64, 128, 16, 16.0
HGRN_HEADS = 4
RET_HEADS = 4
ROPE_BASE = 10000.0
CHUNK = 64
FFN_DENSE = 2752
N_EXPERTS, TOP_K, FFN_EXPERT = 8, 2, 3584

LANES = 128
VMEM_LIMIT = 56 * 1024 * 1024
ROW_TILE = 512
SEQ_TILE = 512
MOE_ROW_TILE = 512
MOE_F_TILE = 1792
FFN_F_PAD = 2816
FFN_F_TILE = 1408
SWIGLU_ROWS = 256
NEG = -1e30
LOG2E = math.log2(math.e)
FAR_UNROLL = 4

BF16 = jnp.bfloat16
F32 = jnp.float32
NT_DIMS = (((1,), (1,)), ((), ()))
TN_DIMS = (((0,), (0,)), ((), ()))


def _cparams(sem):
    return pltpu.CompilerParams(dimension_semantics=sem, vmem_limit_bytes=VMEM_LIMIT)


def _sigmoid(x):
    return 1.0 / (1.0 + jnp.exp(-x))


def _silu(x):
    return x * _sigmoid(x)


def _layer_norm(y, g, b):
    yc = y - jnp.mean(y, axis=-1, keepdims=True)
    return yc * lax.rsqrt(jnp.mean(yc * yc, axis=-1, keepdims=True) + LN_EPS) * g + b


def _proj_kernel(x_ref, w_ref, o_ref):
    o_ref[...] = jnp.dot(x_ref[...].astype(BF16), w_ref[...],
                         preferred_element_type=F32).astype(o_ref.dtype)


def _proj(x, w, out_dtype):
    m, d = x.shape
    n = w.shape[1]
    tm = min(ROW_TILE, m)
    return pl.pallas_call(
        _proj_kernel,
        name="in_proj",
        grid=(m // tm,),
        in_specs=[pl.BlockSpec((tm, d), lambda i: (i, 0)),
                  pl.BlockSpec((d, n), lambda i: (0, 0))],
        out_specs=pl.BlockSpec((tm, n), lambda i: (i, 0)),
        out_shape=jax.ShapeDtypeStruct((m, n), out_dtype),
        compiler_params=_cparams(("parallel",)),
    )(x, w)


def _moba_kernel(c_far_ref, q_ref, k_ref, vt_ref, bias_ref, o_ref, kmean_ref, s_ref, smax_ref, *,
                 nb_pad, nb_gate):
    hp = pl.program_id(1)
    qi = pl.program_id(2)
    tq = MOBA_BLOCK

    @pl.when(qi == 0)
    def _():
        kmean_ref[...] = jnp.zeros_like(kmean_ref)

    k_own = k_ref[pl.ds(pl.multiple_of(qi * tq, tq), tq), :]
    q_tb = q_ref[...].astype(F32).T.astype(BF16)
    kmean = kmean_ref[0:nb_gate, :]
    kmean_ref[pl.ds(qi, 1), :] = jnp.mean(k_own.astype(F32), axis=0, keepdims=True)

    km_hi = kmean.astype(BF16)
    km_lo = (kmean - km_hi.astype(F32)).astype(BF16)
    kdim = lax.broadcasted_iota(jnp.int32, (nb_gate, LANES), 1)
    zero_k = jnp.zeros_like(km_hi)
    gate_lhs = jnp.concatenate(
        [jnp.concatenate([jnp.where((kdim < MOBA_DH) == (hd == 0), part, zero_k) for part in (km_hi, km_lo)],
                         axis=1) for hd in range(2)], axis=0)
    gate = jnp.dot(gate_lhs, jnp.concatenate([q_tb, q_tb], axis=0), preferred_element_type=F32)

    dim = lax.broadcasted_iota(jnp.int32, (LANES, tq), 0)
    blk_t = lax.broadcasted_iota(jnp.int32, (nb_gate, tq), 0)
    blk_tf = blk_t.astype(F32)
    blk = lax.broadcasted_iota(jnp.int32, (tq, nb_pad), 1)
    qs, qa = [], []
    for hd in range(2):
        g = jnp.where(blk_t < qi, gate[hd * nb_gate:(hd + 1) * nb_gate], -jnp.inf)
        sel = jnp.zeros((nb_gate, tq), jnp.bool_)
        for _ in range(MOBA_TOPK):
            gmax = jnp.max(g, axis=0, keepdims=True)
            idx = jnp.min(jnp.where(g == gmax, blk_tf, float(nb_pad)), axis=0, keepdims=True)
            pick = (blk_tf == idx) & (gmax > -jnp.inf)
            sel = sel | pick
            g = jnp.where(pick, -jnp.inf, g)
        qh_t = jnp.where((dim < MOBA_DH) == (hd == 0), q_tb, jnp.zeros_like(q_tb))
        rows = [qh_t, jnp.where(sel, 0.0, NEG).astype(BF16)]
        if nb_pad > nb_gate:
            rows.append(jnp.zeros((nb_pad - nb_gate, tq), BF16))
        qs.append(qh_t)
        qa.append(jnp.concatenate(rows, axis=0))

    def v_t(j, hd):
        return vt_ref[j, hd * MOBA_DH:(hd + 1) * MOBA_DH, :]

    def scores_into(slot, j):
        onehot = jnp.where(blk == j, 1.0, 0.0).astype(BF16)
        ka = jnp.concatenate([k_ref[pl.ds(pl.multiple_of(j * tq, tq), tq), :], onehot], axis=1)
        for hd in range(2):
            s = jnp.dot(ka, qa[hd], preferred_element_type=F32)
            s_ref[slot, hd] = s
            smax_ref[slot, hd] = jnp.max(s, axis=0, keepdims=True)

    def update(state, s, smax, shift, vt):
        m, l, acc = state
        m_new = jnp.maximum(m, smax + shift)
        alpha = jnp.exp2(m - m_new)
        p = jnp.exp2(s - (m_new - shift))
        l = alpha * l + jnp.sum(p, axis=0, keepdims=True)
        acc = alpha * acc + jnp.dot(vt, p.astype(BF16), preferred_element_type=F32)
        return m_new, l, acc

    s_own = [jnp.dot(k_own, qs[hd], preferred_element_type=F32) + bias_ref[hd, 0] for hd in range(2)]
    scores_into(0, 0)
    state = []
    for hd in range(2):
        m0 = jnp.max(s_own[hd], axis=0, keepdims=True)
        p = jnp.exp2(s_own[hd] - m0)
        l0 = jnp.sum(p, axis=0, keepdims=True)
        state.append((m0, l0, jnp.dot(v_t(qi, hd), p.astype(BF16), preferred_element_type=F32)))

    n_far = qi - 1

    last = pl.num_programs(2) - 1

    def far_group(jj, st):
        a = FAR_UNROLL * jj
        for i in range(FAR_UNROLL):
            scores_into((i + 1) % 2, jnp.minimum(a + i + 1, last))
            shift = [jnp.where(a + i < n_far, c_far_ref[2 * hp + hd], NEG) for hd in range(2)]
            st = tuple(update(st[hd], s_ref[i % 2, hd], smax_ref[i % 2, hd], shift[hd],
                              v_t(jnp.minimum(a + i, last), hd)) for hd in range(2))
        return st

    state = lax.fori_loop(0, (n_far + FAR_UNROLL - 1) // FAR_UNROLL, far_group, tuple(state))
    has_adj = qi >= 1
    j_adj = jnp.maximum(qi - 1, 0)
    scores_into(0, j_adj)
    out = []
    for hd in range(2):
        s = s_ref[0, hd] + bias_ref[hd, 1]
        _, l, acc = update(state[hd], s, jnp.max(s, axis=0, keepdims=True), jnp.where(has_adj, 0.0, NEG),
                           v_t(j_adj, hd))
        out.append(acc / l)
    o_ref[...] = jnp.concatenate(out, axis=0).T.astype(o_ref.dtype)


def _t5_bucket(dist):
    n = jnp.maximum(dist, 0)
    max_exact = REL_BUCKETS // 2
    nf = jnp.maximum(n, 1).astype(F32)
    large = max_exact + (jnp.log(nf / max_exact) / math.log(REL_MAX_DIST / max_exact)
                         * (REL_BUCKETS - max_exact)).astype(jnp.int32)
    large = jnp.minimum(large, REL_BUCKETS - 1)
    return jnp.where(n < max_exact, n, large)


def _moba_bias_tables(rel_bias):
    tbl = rel_bias.astype(F32).T
    kpos = jnp.arange(MOBA_BLOCK)[:, None]
    qpos = jnp.arange(MOBA_BLOCK)[None, :]
    d_own = qpos - kpos
    own = jnp.where(d_own >= 0, tbl[:, _t5_bucket(d_own)], NEG)
    adj = tbl[:, _t5_bucket(d_own + MOBA_BLOCK)]
    return jnp.stack([own, adj], axis=1) * LOG2E, tbl[:, REL_BUCKETS - 1] * LOG2E


def _proj_even_kernel(x_ref, wb_ref, wf_ref, wvt_ref, hb_ref, hf_ref, vt_ref):
    xb = x_ref[...].astype(BF16)
    hb_ref[...] = jnp.dot(xb, wb_ref[...], preferred_element_type=F32).astype(hb_ref.dtype)
    hf_ref[...] = jnp.dot(xb, wf_ref[...], preferred_element_type=F32)
    for c in range(vt_ref.shape[0]):
        xc = xb[c * MOBA_BLOCK:(c + 1) * MOBA_BLOCK]
        vt_ref[c] = lax.dot_general(wvt_ref[...], xc, NT_DIMS, preferred_element_type=F32).astype(vt_ref.dtype)


def _proj_even(x, wb, wf, wvt):
    m, d = x.shape
    tm = min(ROW_TILE, m)
    nv = tm // MOBA_BLOCK
    full = lambda a: pl.BlockSpec(a.shape, lambda i: (0, 0))
    return pl.pallas_call(
        _proj_even_kernel,
        name="in_proj_even",
        grid=(m // tm,),
        in_specs=[pl.BlockSpec((tm, d), lambda i: (i, 0)), full(wb), full(wf), full(wvt)],
        out_specs=[pl.BlockSpec((tm, wb.shape[1]), lambda i: (i, 0)),
                   pl.BlockSpec((tm, wf.shape[1]), lambda i: (i, 0)),
                   pl.BlockSpec((nv, wvt.shape[0], MOBA_BLOCK), lambda i: (i, 0, 0))],
        out_shape=[jax.ShapeDtypeStruct((m, wb.shape[1]), BF16),
                   jax.ShapeDtypeStruct((m, wf.shape[1]), F32),
                   jax.ShapeDtypeStruct((m // MOBA_BLOCK, wvt.shape[0], MOBA_BLOCK), BF16)],
        compiler_params=_cparams(("parallel",)),
    )(x, wb, wf, wvt)


def _moba(hb, vt, rel_bias, batch, seq):
    m = hb.shape[0]
    nq = seq // MOBA_BLOCK
    nb_pad = LANES * (-(-nq // LANES))
    bias, c_far = _moba_bias_tables(rel_bias)
    hp_n = MOBA_HEADS // 2
    nb_gate = min(nb_pad, 16 * (-(-nq // 16)))
    kern = functools.partial(_moba_kernel, nb_pad=nb_pad, nb_gate=nb_gate)
    return pl.pallas_call(
        kern,
        name="moba",
        grid=(batch, hp_n, nq),
        in_specs=[pl.BlockSpec(memory_space=pltpu.SMEM),
                  pl.BlockSpec((MOBA_BLOCK, LANES), lambda b, h, i: (b * nq + i, h)),
                  pl.BlockSpec((seq, LANES), lambda b, h, i: (b, hp_n + h)),
                  pl.BlockSpec((nq, LANES, MOBA_BLOCK), lambda b, h, i: (b, h, 0)),
                  pl.BlockSpec((2, 2, MOBA_BLOCK, MOBA_BLOCK), lambda b, h, i: (h, 0, 0, 0))],
        out_specs=pl.BlockSpec((MOBA_BLOCK, LANES), lambda b, h, i: (b * nq + i, h)),
        out_shape=jax.ShapeDtypeStruct((m, MOBA_HEADS * MOBA_DH), BF16),
        scratch_shapes=[pltpu.VMEM((nb_pad, LANES), F32),
                        pltpu.VMEM((2, 2, MOBA_BLOCK, MOBA_BLOCK), F32),
                        pltpu.VMEM((2, 2, 1, MOBA_BLOCK), F32)],
        compiler_params=_cparams(("parallel", "parallel", "arbitrary")),
    )(c_far, hb, hb, vt, bias)


LEVELS = (32, 16, 8, 4, 2, 1)


def _chunk_cumsum(la):
    t = la.shape[0]
    row = lax.broadcasted_iota(jnp.int32, la.shape, 0) % CHUNK
    b = la
    sh = 1
    while sh < CHUNK:
        b = b + jnp.where(row >= sh, pltpu.roll(b, sh, 0), 0.0)
        sh *= 2
    return b


def _level_reference(b, h):
    t, w = b.shape
    if 2 * h >= 8:
        b3 = b.reshape(t // (2 * h), 2 * h, w)
        return jnp.broadcast_to(b3[:, h - 1:h, :], b3.shape).reshape(t, w)
    row = lax.broadcasted_iota(jnp.int32, b.shape, 0)
    if h == 2:
        r4 = row % 4
        return jnp.where(r4 == 0, pltpu.roll(b, t - 1, 0),
                         jnp.where(r4 == 1, b,
                                   jnp.where(r4 == 2, pltpu.roll(b, 1, 0), pltpu.roll(b, 2, 0))))
    return jnp.where(row % 2 == 1, pltpu.roll(b, 1, 0), b)


def _pair_masks():
    t = lax.broadcasted_iota(jnp.int32, (CHUNK, 2 * CHUNK), 0)
    s = lax.broadcasted_iota(jnp.int32, (CHUNK, 2 * CHUNK), 1) % CHUNK
    masks = []
    for h in LEVELS:
        masks.append((t // (2 * h) == s // (2 * h)) & ((t // h) % 2 == 1) & ((s // h) % 2 == 0))
    masks.append(t == s)
    return masks


def _stack_heads(x, split):
    lane = lax.broadcasted_iota(jnp.int32, x.shape, 1)
    z = jnp.zeros_like(x)
    return jnp.concatenate([jnp.where(lane < split, x, z), jnp.where(lane >= split, x, z)], axis=0)


def _gated_linear_tile(q, k, v, la, st_ref):
    t = q.shape[0]
    b = _chunk_cumsum(la)
    q_lv, k_lv = [], []
    for h in LEVELS:
        w = jnp.exp(-jnp.abs(b - _level_reference(b, h)))
        q_lv.append((q * w).astype(BF16))
        k_lv.append((k * w).astype(BF16))
    q_lv.append(q.astype(BF16))
    k_lv.append(k.astype(BF16))
    b3 = b.reshape(t // CHUNK, CHUNK, LANES)
    b_last = jnp.broadcast_to(b3[:, CHUNK - 1:CHUNK, :], b3.shape).reshape(t, LANES)
    q_in = (q * jnp.exp(b)).astype(BF16)
    k_out = (k * jnp.exp(b_last - b)).astype(BF16)
    masks = _pair_masks()
    e = lax.broadcasted_iota(jnp.int32, st_ref.shape, 0)
    d = lax.broadcasted_iota(jnp.int32, st_ref.shape, 1)
    same_head = (e < GLA_DV) == (d < GLA_DK)
    outs = []
    for c in range(t // CHUNK):
        rows = slice(c * CHUNK, (c + 1) * CHUNK)
        a = jnp.zeros((CHUNK, 2 * CHUNK), F32)
        for lv in range(len(masks)):
            p = lax.dot_general(q_lv[lv][rows], _stack_heads(k_lv[lv][rows], GLA_DK), NT_DIMS,
                                preferred_element_type=F32)
            a = a + jnp.where(masks[lv], p, 0.0)
        vc = v[rows]
        o = jnp.dot(a.astype(BF16), _stack_heads(vc, GLA_DV), preferred_element_type=F32)
        st = st_ref[...]
        o = o + lax.dot_general(q_in[rows], st.astype(BF16), NT_DIMS, preferred_element_type=F32)
        upd = lax.dot_general(vc, k_out[rows], TN_DIMS, preferred_element_type=F32)
        st_ref[...] = st * jnp.exp(b[(c + 1) * CHUNK - 1:(c + 1) * CHUNK, :]) + jnp.where(same_head, upd, 0.0)
        outs.append(o)
    return jnp.concatenate(outs, axis=0)


def _head_rms_gate(o, norm, gate):
    parts = []
    for hd in range(2):
        oh = o[:, hd * GLA_DV:(hd + 1) * GLA_DV]
        parts.append(oh * lax.rsqrt(jnp.mean(oh * oh, axis=-1, keepdims=True) + HEAD_NORM_EPS))
    return jnp.concatenate(parts, axis=1) * norm * _silu(gate)


def _log_sigmoid(x):
    return jnp.minimum(x, 0.0) - jnp.log1p(jnp.exp(-jnp.abs(x)))


def _gla_kernel(q_ref, k_ref, v_ref, ga_ref, gr_ref, wg_ref, bg_ref, norm_ref, o_ref, st_ref):
    @pl.when(pl.program_id(2) == 0)
    def _():
        st_ref[...] = jnp.zeros_like(st_ref)

    gate_in = jnp.dot(ga_ref[...].astype(BF16), wg_ref[...], preferred_element_type=F32) + bg_ref[...]
    la = _log_sigmoid(gate_in) / GLA_TAU
    q = q_ref[...].astype(F32) * (GLA_DK ** -0.5)
    o = _gated_linear_tile(q, k_ref[...].astype(F32), v_ref[...], la, st_ref)
    o_ref[...] = _head_rms_gate(o, norm_ref[...], gr_ref[...]).astype(o_ref.dtype)


def _gla(hb, hf, w_gate, b_gate, norm, batch, seq):
    m = hb.shape[0]
    tt = min(SEQ_TILE, seq)
    nt = seq // tt
    hp_n = GLA_HEADS // 2
    wg = jnp.zeros((LANES, GLA_HEADS * GLA_DK), BF16).at[:GLA_RANK].set(w_gate.astype(BF16))
    row = lambda b, h, i: b * nt + i
    return pl.pallas_call(
        _gla_kernel,
        name="gla",
        grid=(batch, hp_n, nt),
        in_specs=[pl.BlockSpec((tt, LANES), lambda b, h, i: (row(b, h, i), 8 + h)),
                  pl.BlockSpec((tt, LANES), lambda b, h, i: (row(b, h, i), 10 + h)),
                  pl.BlockSpec((tt, 2 * GLA_DV), lambda b, h, i: (row(b, h, i), 6 + h)),
                  pl.BlockSpec((tt, LANES), lambda b, h, i: (row(b, h, i), 4)),
                  pl.BlockSpec((tt, 2 * GLA_DV), lambda b, h, i: (row(b, h, i), h)),
                  pl.BlockSpec((LANES, LANES), lambda b, h, i: (0, h)),
                  pl.BlockSpec((1, LANES), lambda b, h, i: (0, h)),
                  pl.BlockSpec((1, 2 * GLA_DV), lambda b, h, i: (0, h))],
        out_specs=pl.BlockSpec((tt, 2 * GLA_DV), lambda b, h, i: (row(b, h, i), h)),
        out_shape=jax.ShapeDtypeStruct((m, GLA_HEADS * GLA_DV), BF16),
        scratch_shapes=[pltpu.VMEM((2 * GLA_DV, LANES), F32)],
        compiler_params=_cparams(("parallel", "parallel", "arbitrary")),
    )(hb, hb, hb, hf, hf, wg, b_gate.reshape(1, -1), norm.reshape(1, -1))


def _hgrn_kernel(q_ref, f_ref, v_ref, g_ref, lb_ref, norm_ref, o_ref, st_ref):
    @pl.when(pl.program_id(2) == 0)
    def _():
        st_ref[...] = jnp.zeros_like(st_ref)

    lb = lb_ref[...]
    f = lb + (1.0 - lb) * _sigmoid(f_ref[...])
    o = _gated_linear_tile(q_ref[...], 1.0 - f, v_ref[...].astype(BF16), jnp.log(f), st_ref)
    o_ref[...] = _head_rms_gate(o, norm_ref[...], g_ref[...]).astype(o_ref.dtype)


def _hgrn(h, lb, norm, batch, seq):
    m = h.shape[0]
    tt = min(SEQ_TILE, seq)
    nt = seq // tt
    row = lambda b, hh, i: b * nt + i
    return pl.pallas_call(
        _hgrn_kernel,
        name="hgrn",
        grid=(batch, HGRN_HEADS // 2, nt),
        in_specs=[pl.BlockSpec((tt, LANES), lambda b, hh, i: (row(b, hh, i), hh)),
                  pl.BlockSpec((tt, LANES), lambda b, hh, i: (row(b, hh, i), 2 + hh)),
                  pl.BlockSpec((tt, 2 * GLA_DV), lambda b, hh, i: (row(b, hh, i), 2 + hh)),
                  pl.BlockSpec((tt, 2 * GLA_DV), lambda b, hh, i: (row(b, hh, i), 4 + hh)),
                  pl.BlockSpec((1, LANES), lambda b, hh, i: (0, hh)),
                  pl.BlockSpec((1, 2 * GLA_DV), lambda b, hh, i: (0, hh))],
        out_specs=pl.BlockSpec((tt, 2 * GLA_DV), lambda b, hh, i: (row(b, hh, i), hh)),
        out_shape=jax.ShapeDtypeStruct((m, HGRN_HEADS * GLA_DV), BF16),
        scratch_shapes=[pltpu.VMEM((2 * GLA_DV, LANES), F32)],
        compiler_params=_cparams(("parallel", "parallel", "arbitrary")),
    )(h, h, h, h, lb.reshape(1, -1), norm.reshape(1, -1))


def _rotate(x, cos, sin_signed):
    lane = lax.broadcasted_iota(jnp.int32, x.shape, 1)
    half = GLA_DK // 2
    swapped = jnp.where(lane % GLA_DK < half, pltpu.roll(x, LANES - half, 1), pltpu.roll(x, half, 1))
    return x * cos + swapped * sin_signed


def _ret_kernel(q_ref, k_ref, v_ref, g_ref, cos_ref, sin_ref, dmask_ref, zeta_ref, xi_ref, gch_ref,
                norm_ref, o_ref, st_ref):
    @pl.when(pl.program_id(2) == 0)
    def _():
        st_ref[...] = jnp.zeros_like(st_ref)

    t = q_ref.shape[0]
    cos, sin = cos_ref[...], sin_ref[...]
    q = _rotate(q_ref[...], cos, sin)
    k = _rotate(k_ref[...], cos, sin) * (GLA_DK ** -0.5)
    v = v_ref[...].astype(BF16)
    nc = t // CHUNK
    zeta = jnp.broadcast_to(zeta_ref[0][None], (nc, CHUNK, LANES)).reshape(t, LANES)
    xi = jnp.broadcast_to(xi_ref[0][None], (nc, CHUNK, LANES)).reshape(t, LANES)
    q_b, k_b = q.astype(BF16), k.astype(BF16)
    q_in = (q * xi).astype(BF16)
    k_out = (k * zeta).astype(BF16)
    dmask = dmask_ref[0]
    g_chunk = gch_ref[0]
    e = lax.broadcasted_iota(jnp.int32, st_ref.shape, 0)
    d = lax.broadcasted_iota(jnp.int32, st_ref.shape, 1)
    same_head = (e < GLA_DV) == (d < GLA_DK)
    outs = []
    for c in range(nc):
        rows = slice(c * CHUNK, (c + 1) * CHUNK)
        a = lax.dot_general(q_b[rows], _stack_heads(k_b[rows], GLA_DK), NT_DIMS,
                            preferred_element_type=F32) * dmask
        vc = v[rows]
        o = jnp.dot(a.astype(BF16), _stack_heads(vc, GLA_DV), preferred_element_type=F32)
        st = st_ref[...]
        o = o + lax.dot_general(q_in[rows], st.astype(BF16), NT_DIMS, preferred_element_type=F32)
        upd = lax.dot_general(vc, k_out[rows], TN_DIMS, preferred_element_type=F32)
        st_ref[...] = st * g_chunk + jnp.where(same_head, upd, 0.0)
        outs.append(o)
    o = jnp.concatenate(outs, axis=0)
    parts = []
    for hd in range(2):
        oh = o[:, hd * GLA_DV:(hd + 1) * GLA_DV]
        oh = oh - jnp.mean(oh, axis=-1, keepdims=True)
        parts.append(oh * lax.rsqrt(jnp.mean(oh * oh, axis=-1, keepdims=True) + HEAD_NORM_EPS))
    o_ref[...] = (jnp.concatenate(parts, axis=1) * norm_ref[...] * _silu(g_ref[...])).astype(o_ref.dtype)


def _retention_tables(seq):
    half = GLA_DK // 2
    inv = ROPE_BASE ** (-jnp.arange(half, dtype=F32) / half)
    ang = jnp.arange(seq, dtype=F32)[:, None] * inv[None, :]
    cos, sin = jnp.cos(ang), jnp.sin(ang)
    cos_t = jnp.tile(cos, (1, 4))
    sin_t = jnp.tile(jnp.concatenate([-sin, sin], axis=1), (1, 2))
    log_g = jnp.log1p(-jnp.exp2(-5.0 - jnp.arange(RET_HEADS, dtype=F32)))
    pos = jnp.arange(CHUNK, dtype=F32)
    rel = pos[:, None] - pos[None, :]
    dmask = jnp.where(rel >= 0, jnp.exp(jnp.maximum(rel, 0.0)[None] * log_g[:, None, None]), 0.0)
    zeta = jnp.exp((CHUNK - 1 - pos)[None, :] * log_g[:, None])
    xi = jnp.exp((pos + 1.0)[None, :] * log_g[:, None])
    g_chunk = jnp.exp(CHUNK * log_g)
    hp_n = RET_HEADS // 2
    pair = lambda x: x.reshape(hp_n, 2, *x.shape[1:])
    dmask_p = jnp.concatenate([pair(dmask)[:, 0], pair(dmask)[:, 1]], axis=-1)
    lanes = lambda x: jnp.repeat(pair(x), GLA_DK, axis=1)
    zeta_p = jnp.swapaxes(lanes(zeta), 1, 2)
    xi_p = jnp.swapaxes(lanes(xi), 1, 2)
    gch_p = lanes(g_chunk)[:, None, :]
    return cos_t, sin_t, dmask_p, zeta_p, xi_p, gch_p


def _retention(h, norm, batch, seq):
    m = h.shape[0]
    tt = min(SEQ_TILE, seq)
    nt = seq // tt
    cos_t, sin_t, dmask, zeta, xi, gch = _retention_tables(seq)
    row = lambda b, hh, i: b * nt + i
    const3 = lambda shape: pl.BlockSpec((1,) + shape, lambda b, hh, i: (hh, 0, 0))
    return pl.pallas_call(
        _ret_kernel,
        name="retention",
        grid=(batch, RET_HEADS // 2, nt),
        in_specs=[pl.BlockSpec((tt, LANES), lambda b, hh, i: (row(b, hh, i), 12 + hh)),
                  pl.BlockSpec((tt, LANES), lambda b, hh, i: (row(b, hh, i), 14 + hh)),
                  pl.BlockSpec((tt, 2 * GLA_DV), lambda b, hh, i: (row(b, hh, i), 8 + hh)),
                  pl.BlockSpec((tt, 2 * GLA_DV), lambda b, hh, i: (row(b, hh, i), 10 + hh)),
                  pl.BlockSpec((tt, LANES), lambda b, hh, i: (i, 0)),
                  pl.BlockSpec((tt, LANES), lambda b, hh, i: (i, 0)),
                  const3((CHUNK, 2 * CHUNK)), const3((CHUNK, LANES)), const3((CHUNK, LANES)),
                  const3((1, LANES)),
                  pl.BlockSpec((1, 2 * GLA_DV), lambda b, hh, i: (0, hh))],
        out_specs=pl.BlockSpec((tt, 2 * GLA_DV), lambda b, hh, i: (row(b, hh, i), hh)),
        out_shape=jax.ShapeDtypeStruct((m, RET_HEADS * GLA_DV), BF16),
        scratch_shapes=[pltpu.VMEM((2 * GLA_DV, LANES), F32)],
        compiler_params=_cparams(("parallel", "parallel", "arbitrary")),
    )(h, h, h, h, cos_t, sin_t, dmask, zeta, xi, gch, norm.reshape(1, -1))


def _route(xn, w, b):
    logits = jnp.dot(xn, w, preferred_element_type=F32, precision=lax.Precision.HIGHEST) + b
    lane = lax.broadcasted_iota(jnp.int32, logits.shape, 1)
    lane_f = lane.astype(F32)
    logits = jnp.where(lane < N_EXPERTS, logits, -jnp.inf)
    m1 = jnp.max(logits, axis=1, keepdims=True)
    i1 = jnp.min(jnp.where(logits == m1, lane_f, float(LANES)), axis=1, keepdims=True)
    rest = jnp.where(lane_f == i1, -jnp.inf, logits)
    m2 = jnp.max(rest, axis=1, keepdims=True)
    i2 = jnp.min(jnp.where(rest == m2, lane_f, float(LANES)), axis=1, keepdims=True)
    z = jnp.exp(m2 - m1)
    g1 = 1.0 / (1.0 + z)
    experts = jnp.where(lane == 0, i1, jnp.where(lane == 1, i2, 0.0)).astype(jnp.int32)
    gates = jnp.where(lane == 0, g1, jnp.where(lane == 1, z * g1, 0.0))
    return experts, gates


def _outproj_kernel(oa_ref, ob_ref, x_ref, wa_ref, wb_ref, g_ref, b_ref, *rest, alpha):
    y = jnp.dot(oa_ref[...], wa_ref[...], preferred_element_type=F32)
    y = y + jnp.dot(ob_ref[...], wb_ref[...], preferred_element_type=F32)
    xn = _layer_norm(alpha * x_ref[...] + y, g_ref[...], b_ref[...])
    if len(rest) == 1:
        rest[0][...] = xn
    else:
        wr_ref, br_ref, o_ref, e_ref, gt_ref = rest
        o_ref[...] = xn
        e_ref[...], gt_ref[...] = _route(xn, wr_ref[...], br_ref[...])


def _outproj_ln(oa, ob, x, w_out, g, b, alpha, router=None):
    m, d = x.shape
    tm = min(ROW_TILE, m)
    wa = w_out[:oa.shape[1]].astype(BF16)
    wb = w_out[oa.shape[1]:].astype(BF16)
    full = lambda a: pl.BlockSpec(a.shape, lambda i: (0, 0))
    rows = lambda a: pl.BlockSpec((tm, a.shape[1]), lambda i: (i, 0))
    args = [oa, ob, x, wa, wb, g.reshape(1, -1), b.reshape(1, -1)]
    in_specs = [rows(oa), rows(ob), rows(x)] + [full(a) for a in args[3:]]
    out_specs = [rows(x)]
    out_shape = [jax.ShapeDtypeStruct((m, d), F32)]
    if router is not None:
        wr = jnp.zeros((d, LANES), F32).at[:, :N_EXPERTS].set(router[0])
        br = jnp.zeros((1, LANES), F32).at[0, :N_EXPERTS].set(router[1])
        args += [wr, br]
        in_specs += [full(wr), full(br)]
        lanes = pl.BlockSpec((tm, LANES), lambda i: (i, 0))
        out_specs += [lanes, lanes]
        out_shape += [jax.ShapeDtypeStruct((m, LANES), jnp.int32), jax.ShapeDtypeStruct((m, LANES), F32)]
    out = pl.pallas_call(
        functools.partial(_outproj_kernel, alpha=alpha),
        name="out_proj_ln",
        grid=(m // tm,),
        in_specs=in_specs,
        out_specs=out_specs,
        out_shape=out_shape,
        compiler_params=_cparams(("parallel",)),
    )(*args)
    if router is None:
        return out[0]
    return out[0], out[1][:, :TOP_K], out[2][:, :TOP_K]


def _ln_ple(x, ffn, p, g, b, wg, wp, alpha):
    y = _layer_norm(alpha * x + ffn, g, b)
    gate = _sigmoid(jnp.dot(y.astype(BF16), wg, preferred_element_type=F32))
    return y + gate * jnp.dot(p.astype(BF16), wp, preferred_element_type=F32)


def _swiglu_chunks(xb_ref, w1, w3, w2):
    n = xb_ref.shape[0] // SWIGLU_ROWS
    h = []
    for c in range(n):
        xc = xb_ref[c * SWIGLU_ROWS:(c + 1) * SWIGLU_ROWS, :]
        h.append((jnp.dot(xc, w1, preferred_element_type=F32), jnp.dot(xc, w3, preferred_element_type=F32)))
    return [jnp.dot((_silu(h1) * h3).astype(BF16), w2, preferred_element_type=F32) for h1, h3 in h]


def _accumulate_steps(parts, acc_ref, finish, n_steps):
    assert n_steps > 1
    f = pl.program_id(1)
    last = n_steps - 1
    chunk = lambda c: slice(c * SWIGLU_ROWS, (c + 1) * SWIGLU_ROWS)

    @pl.when(f == 0)
    def _():
        for c, part in enumerate(parts):
            acc_ref[chunk(c), :] = part

    @pl.when((f > 0) & (f < last))
    def _():
        for c, part in enumerate(parts):
            acc_ref[chunk(c), :] += part

    @pl.when(f == last)
    def _():
        finish(jnp.concatenate([acc_ref[chunk(c), :] + part for c, part in enumerate(parts)], axis=0))


def _swiglu_kernel(x_ref, p_ref, w1_ref, w3_ref, w2_ref, g_ref, b_ref, wg_ref, wp_ref, o_ref,
                   xb_ref, acc_ref, *, alpha, n_steps):
    @pl.when(pl.program_id(1) == 0)
    def _():
        xb_ref[...] = x_ref[...].astype(BF16)

    def finish(ffn):
        o_ref[...] = _ln_ple(x_ref[...], ffn, p_ref[...], g_ref[...], b_ref[...], wg_ref[...], wp_ref[...],
                             alpha)

    _accumulate_steps(_swiglu_chunks(xb_ref, w1_ref[...], w3_ref[...], w2_ref[...]), acc_ref, finish,
                      n_steps)


def _swiglu_ln_ple(x, p, w1, w3, w2, g, b, wg, wp, alpha):
    m, d = x.shape
    tm = min(ROW_TILE, m)
    pad = FFN_F_PAD - w1.shape[1]
    w1p = jnp.pad(w1.astype(BF16), ((0, 0), (0, pad)))
    w3p = jnp.pad(w3.astype(BF16), ((0, 0), (0, pad)))
    w2p = jnp.pad(w2.astype(BF16), ((0, pad), (0, 0)))
    nf = FFN_F_PAD // FFN_F_TILE
    g2, b2 = g.reshape(1, -1), b.reshape(1, -1)
    wgb, wpb = wg.astype(BF16), wp.astype(BF16)
    full = lambda a: pl.BlockSpec(a.shape, lambda i, f: (0, 0))
    rows = lambda a: pl.BlockSpec((tm, a.shape[1]), lambda i, f: (i, 0))
    return pl.pallas_call(
        functools.partial(_swiglu_kernel, alpha=alpha, n_steps=nf),
        name="swiglu_ln_ple",
        grid=(m // tm, nf),
        in_specs=[rows(x), rows(p),
                  pl.BlockSpec((d, FFN_F_TILE), lambda i, f: (0, f)),
                  pl.BlockSpec((d, FFN_F_TILE), lambda i, f: (0, f)),
                  pl.BlockSpec((FFN_F_TILE, d), lambda i, f: (f, 0)),
                  full(g2), full(b2), full(wgb), full(wpb)],
        out_specs=rows(x),
        out_shape=jax.ShapeDtypeStruct((m, d), F32),
        scratch_shapes=[pltpu.VMEM((tm, d), BF16), pltpu.VMEM((tm, d), F32)],
        compiler_params=_cparams(("parallel", "arbitrary")),
    )(x, p, w1p, w3p, w2p, g2, b2, wgb, wpb)


def _experts_kernel(be_ref, x_ref, w1_ref, w3_ref, w2_ref, o_ref, acc_ref, *, n_steps):
    def finish(y):
        o_ref[...] = y.astype(o_ref.dtype)

    _accumulate_steps(_swiglu_chunks(x_ref, w1_ref[0], w3_ref[0], w2_ref[0]), acc_ref, finish, n_steps)


def _experts(xg, blk_e, w1, w3, w2):
    r, d = xg.shape
    fe = w1.shape[2]
    nf = fe // MOE_F_TILE
    grid_spec = pltpu.PrefetchScalarGridSpec(
        num_scalar_prefetch=1,
        grid=(r // MOE_ROW_TILE, nf),
        in_specs=[pl.BlockSpec((MOE_ROW_TILE, d), lambda i, f, be: (i, 0)),
                  pl.BlockSpec((1, d, MOE_F_TILE), lambda i, f, be: (be[i], 0, f)),
                  pl.BlockSpec((1, d, MOE_F_TILE), lambda i, f, be: (be[i], 0, f)),
                  pl.BlockSpec((1, MOE_F_TILE, d), lambda i, f, be: (be[i], f, 0))],
        out_specs=pl.BlockSpec((MOE_ROW_TILE, d), lambda i, f, be: (i, 0)),
        scratch_shapes=[pltpu.VMEM((MOE_ROW_TILE, d), F32)],
    )
    return pl.pallas_call(
        functools.partial(_experts_kernel, n_steps=nf),
        name="experts",
        grid_spec=grid_spec,
        out_shape=jax.ShapeDtypeStruct((r, d), BF16),
        compiler_params=_cparams(("parallel", "arbitrary")),
    )(blk_e, xg, w1, w3, w2)


def _combine_kernel(x_ref, y0_ref, y1_ref, gt_ref, p_ref, g_ref, b_ref, wg_ref, wp_ref, o_ref, *, alpha):
    gt = gt_ref[...]
    ffn = y0_ref[...].astype(F32) * gt[:, 0:1] + y1_ref[...].astype(F32) * gt[:, 1:2]
    o_ref[...] = _ln_ple(x_ref[...], ffn, p_ref[...], g_ref[...], b_ref[...], wg_ref[...], wp_ref[...],
                         alpha)


def _combine_ln_ple(x, y0, y1, gates, p, g, b, wg, wp, alpha):
    m, d = x.shape
    tm = min(ROW_TILE, m)
    g2, b2 = g.reshape(1, -1), b.reshape(1, -1)
    wgb, wpb = wg.astype(BF16), wp.astype(BF16)
    gt = jnp.zeros((m, LANES), F32).at[:, :TOP_K].set(gates)
    full = lambda a: pl.BlockSpec(a.shape, lambda i: (0, 0))
    rows = lambda a: pl.BlockSpec((tm, a.shape[1]), lambda i: (i, 0))
    return pl.pallas_call(
        functools.partial(_combine_kernel, alpha=alpha),
        name="combine_ln_ple",
        grid=(m // tm,),
        in_specs=[rows(x), rows(y0), rows(y1), rows(gt), rows(p), full(g2), full(b2), full(wgb), full(wpb)],
        out_specs=rows(x),
        out_shape=jax.ShapeDtypeStruct((m, d), F32),
        compiler_params=_cparams(("parallel",)),
    )(x, y0, y1, gt, p, g2, b2, wgb, wpb)


def _moe_ln_ple(x, top_e, gates, p, w1, w3, w2, g, b, wg, wp, alpha):
    m, d = x.shape
    n_asg = m * TOP_K
    e_flat = top_e.reshape(-1)
    onehot = (jnp.arange(N_EXPERTS, dtype=jnp.int32)[:, None] == e_flat[None, :]).astype(jnp.int32)
    csum = jnp.cumsum(onehot, axis=1)
    rank = jnp.sum(csum * onehot, axis=0) - 1
    counts = csum[:, -1]
    starts = jnp.cumsum(counts) - counts
    padded = (counts + MOE_ROW_TILE - 1) // MOE_ROW_TILE * MOE_ROW_TILE
    p_ends = jnp.cumsum(padded)
    p_starts = p_ends - padded
    pos = (p_starts[e_flat] + rank).reshape(m, TOP_K)
    n_rows = (n_asg // MOE_ROW_TILE + N_EXPERTS) * MOE_ROW_TILE
    blk_start = jnp.arange(n_rows // MOE_ROW_TILE, dtype=jnp.int32) * MOE_ROW_TILE
    blk_e = jnp.minimum(jnp.searchsorted(p_ends, blk_start, side='right'), N_EXPERTS - 1).astype(jnp.int32)
    tok_s = (jnp.argsort(e_flat) // TOP_K).astype(jnp.int32)
    row = jnp.arange(n_rows, dtype=jnp.int32)
    row_e = jnp.repeat(blk_e, MOE_ROW_TILE)
    local = row - p_starts[row_e]
    tok_buf = jnp.where(local < counts[row_e], tok_s[jnp.minimum(starts[row_e] + local, n_asg - 1)], 0)
    xg = x.astype(BF16)[tok_buf]
    y = _experts(xg, blk_e, w1.astype(BF16), w3.astype(BF16), w2.astype(BF16))
    return _combine_ln_ple(x, y[pos[:, 0]], y[pos[:, 1]], gates, p, g, b, wg, wp, alpha)


def kernel(x, p, rel_bias, even_w_in, gla_w_gate, gla_b_gate, gla_norm, even_w_out, odd_w_in, hgrn_gamma,
           hgrn_norm, ret_norm, odd_w_out, ln_mix_g, ln_mix_b, ln_ffn_g, ln_ffn_b, ffn_w1, ffn_w3, ffn_w2,
           router_w, router_b, expert_w1, expert_w3, expert_w2, ple_w_gate, ple_w_proj):
    batch, seq, d = x.shape
    depth = p.shape[0]
    m = batch * seq
    alpha = (2.0 * depth) ** 0.25
    lb_p = jax.nn.softmax(hgrn_gamma.astype(F32), axis=0)
    lb_all = jnp.cumsum(lb_p, axis=0) - lb_p[0]
    xf = x.reshape(m, d)
    pf = p.reshape(depth, m, PLE_DIM)
    n_mq = 3 * MOBA_HEADS * MOBA_DH + 2 * GLA_HEADS * GLA_DK + GLA_HEADS * GLA_DV
    for i in range(depth):
        j = i // 2
        if i % 2 == 0:
            w_in = even_w_in[j]
            n_qk = 2 * MOBA_HEADS * MOBA_DH
            n_v = MOBA_HEADS * MOBA_DH
            w_q = w_in[:, :n_v] * (MOBA_DH ** -0.5 * LOG2E)
            w_b = jnp.concatenate([w_q, w_in[:, n_v:n_qk], w_in[:, n_qk + n_v:n_mq]], axis=1).astype(BF16)
            w_vt = w_in[:, n_qk:n_qk + n_v].T.astype(BF16)
            w_f = jnp.concatenate([w_in[:, n_mq + GLA_RANK:], w_in[:, n_mq:n_mq + GLA_RANK],
                                   jnp.zeros((d, LANES - GLA_RANK), F32)], axis=1).astype(BF16)
            hb, hf, vt = _proj_even(xf, w_b, w_f, w_vt)
            o_a = _moba(hb, vt, rel_bias, batch, seq)
            o_b = _gla(hb, hf, gla_w_gate[j], gla_b_gate[j], gla_norm[j], batch, seq)
            xf = _outproj_ln(o_a, o_b, xf, even_w_out[j], ln_mix_g[i], ln_mix_b[i], alpha)
            xf = _swiglu_ln_ple(xf, pf[i], ffn_w1[j], ffn_w3[j], ffn_w2[j], ln_ffn_g[i], ln_ffn_b[i],
                                ple_w_gate[i], ple_w_proj[i], alpha)
        else:
            h = _proj(xf, odd_w_in[j].astype(BF16), F32)
            o_c = _hgrn(h, lb_all[i], hgrn_norm[j], batch, seq)
            o_d = _retention(h, ret_norm[j], batch, seq)
            xf, top_e, gates = _outproj_ln(o_c, o_d, xf, odd_w_out[j], ln_mix_g[i], ln_mix_b[i], alpha,
                                           router=(router_w[j], router_b[j]))
            xf = _moe_ln_ple(xf, top_e, gates, pf[i], expert_w1[j], expert_w3[j], expert_w2[j],
                             ln_ffn_g[i], ln_ffn_b[i], ple_w_gate[i], ple_w_proj[i], alpha)
    return xf.reshape(batch, seq, d)
```

```python
import functools
import math

import numpy as np
import jax
import jax.numpy as jnp
from jax import lax
from jax.experimental import pallas as pl
from jax.experimental.pallas import tpu as pltpu

D_MODEL = 1024
PLE_DIM = 256
LN_EPS = 1e-5
HEAD_NORM_EPS = 1e-6
MOBA_HEADS, MOBA_DH, MOBA_BLOCK, MOBA_TOPK = 8, 64, 256, 3
REL_BUCKETS, REL_MAX_DIST = 32, 128
GLA_HEADS, GLA_DK, GLA_DV, GLA_RANK, GLA_TAU = 4, 64, 128, 16, 16.0
HGRN_HEADS = 4
RET_HEADS = 4
ROPE_BASE = 10000.0
CHUNK = 64
FFN_DENSE = 2752
N_EXPERTS, TOP_K, FFN_EXPERT = 8, 2, 3584

LANES = 128
VMEM_LIMIT = 56 * 1024 * 1024
ROW_TILE = 512
SEQ_TILE = 512
MOE_ROW_TILE = 512
MOE_F_TILE = 1792
FFN_F_PAD = 2816
FFN_F_TILE = 1408
SWIGLU_ROWS = 256
NEG = -1e30
LOG2E = math.log2(math.e)
FAR_UNROLL = 4
DENOM_ROWS = 16

BF16 = jnp.bfloat16
F32 = jnp.float32
NT_DIMS = (((1,), (1,)), ((), ()))
TN_DIMS = (((0,), (0,)), ((), ()))


def _cparams(sem):
    return pltpu.CompilerParams(dimension_semantics=sem, vmem_limit_bytes=VMEM_LIMIT)


def _sigmoid(x):
    return 1.0 / (1.0 + jnp.exp(-x))


def _silu(x):
    return x * _sigmoid(x)


def _layer_norm(y, g, b):
    yc = y - jnp.mean(y, axis=-1, keepdims=True)
    return yc * lax.rsqrt(jnp.mean(yc * yc, axis=-1, keepdims=True) + LN_EPS) * g + b


def _proj_kernel(x_ref, w_ref, o_ref):
    o_ref[...] = jnp.dot(x_ref[...].astype(BF16), w_ref[...],
                         preferred_element_type=F32).astype(o_ref.dtype)


def _proj(x, w, out_dtype):
    m, d = x.shape
    n = w.shape[1]
    tm = min(ROW_TILE, m)
    return pl.pallas_call(
        _proj_kernel,
        name="in_proj",
        grid=(m // tm,),
        in_specs=[pl.BlockSpec((tm, d), lambda i: (i, 0)),
                  pl.BlockSpec((d, n), lambda i: (0, 0))],
        out_specs=pl.BlockSpec((tm, n), lambda i: (i, 0)),
        out_shape=jax.ShapeDtypeStruct((m, n), out_dtype),
        compiler_params=_cparams(("parallel",)),
    )(x, w)


def _moba_kernel(c_far_ref, q_ref, k_ref, vt_ref, bias_ref, o_ref, kmean_ref, s_ref, smax_ref, *,
                 nb_pad, nb_gate):
    hp = pl.program_id(1)
    qi = pl.program_id(2)
    tq = MOBA_BLOCK

    @pl.when(qi == 0)
    def _():
        kmean_ref[...] = jnp.zeros_like(kmean_ref)

    k_own = k_ref[pl.ds(pl.multiple_of(qi * tq, tq), tq), :]
    q_tb = q_ref[...].astype(F32).T.astype(BF16)
    kmean = kmean_ref[0:nb_gate, :]
    kmean_ref[pl.ds(qi, 1), :] = jnp.mean(k_own.astype(F32), axis=0, keepdims=True)

    km_hi = kmean.astype(BF16)
    km_lo = (kmean - km_hi.astype(F32)).astype(BF16)
    kdim = lax.broadcasted_iota(jnp.int32, (nb_gate, LANES), 1)
    zero_k = jnp.zeros_like(km_hi)
    gate_lhs = jnp.concatenate(
        [jnp.concatenate([jnp.where((kdim < MOBA_DH) == (hd == 0), part, zero_k) for part in (km_hi, km_lo)],
                         axis=1) for hd in range(2)], axis=0)
    gate = jnp.dot(gate_lhs, jnp.concatenate([q_tb, q_tb], axis=0), preferred_element_type=F32)

    dim = lax.broadcasted_iota(jnp.int32, (LANES, tq), 0)
    blk_t = lax.broadcasted_iota(jnp.int32, (nb_gate, tq), 0)
    blk_tf = blk_t.astype(F32)
    blk = lax.broadcasted_iota(jnp.int32, (tq, nb_pad), 1)
    qs, qa = [], []
    for hd in range(2):
        g = jnp.where(blk_t < qi, gate[hd * nb_gate:(hd + 1) * nb_gate], -jnp.inf)
        sel = jnp.zeros((nb_gate, tq), jnp.bool_)
        for _ in range(MOBA_TOPK):
            gmax = jnp.max(g, axis=0, keepdims=True)
            idx = jnp.min(jnp.where(g == gmax, blk_tf, float(nb_pad)), axis=0, keepdims=True)
            pick = (blk_tf == idx) & (gmax > -jnp.inf)
            sel = sel | pick
            g = jnp.where(pick, -jnp.inf, g)
        qh_t = jnp.where((dim < MOBA_DH) == (hd == 0), q_tb, jnp.zeros_like(q_tb))
        rows = [qh_t, jnp.where(sel, 0.0, NEG).astype(BF16)]
        if nb_pad > nb_gate:
            rows.append(jnp.zeros((nb_pad - nb_gate, tq), BF16))
        qs.append(qh_t)
        qa.append(jnp.concatenate(rows, axis=0))

    ones_rows = jnp.ones((DENOM_ROWS, tq), BF16)

    def v_t(j, hd):
        return jnp.concatenate([vt_ref[j, hd * MOBA_DH:(hd + 1) * MOBA_DH, :], ones_rows], axis=0)

    def scores_into(slot, j):
        onehot = jnp.where(blk == j, 1.0, 0.0).astype(BF16)
        ka = jnp.concatenate([k_ref[pl.ds(pl.multiple_of(j * tq, tq), tq), :], onehot], axis=1)
        for hd in range(2):
            s = jnp.dot(ka, qa[hd], preferred_element_type=F32)
            s_ref[slot, hd] = s
            smax_ref[slot, hd] = jnp.max(s, axis=0, keepdims=True)

    def update(state, s, smax, shift, vt):
        m, acc = state
        m_new = jnp.maximum(m, smax + shift)
        p = jnp.exp2(s - (m_new - shift))
        acc = jnp.exp2(m - m_new) * acc + jnp.dot(vt, p.astype(BF16), preferred_element_type=F32)
        return m_new, acc

    s_own = [jnp.dot(k_own, qs[hd], preferred_element_type=F32) + bias_ref[hd, 0] for hd in range(2)]
    scores_into(0, 0)
    state = []
    for hd in range(2):
        m0 = jnp.max(s_own[hd], axis=0, keepdims=True)
        p = jnp.exp2(s_own[hd] - m0)
        state.append((m0, jnp.dot(v_t(qi, hd), p.astype(BF16), preferred_element_type=F32)))

    n_far = qi - 1

    last = pl.num_programs(2) - 1

    def far_group(jj, st):
        a = FAR_UNROLL * jj
        for i in range(FAR_UNROLL):
            scores_into((i + 1) % 2, jnp.minimum(a + i + 1, last))
            shift = [jnp.where(a + i < n_far, c_far_ref[2 * hp + hd], NEG) for hd in range(2)]
            st = tuple(update(st[hd], s_ref[i % 2, hd], smax_ref[i % 2, hd], shift[hd],
                              v_t(jnp.minimum(a + i, last), hd)) for hd in range(2))
        return st

    state = lax.fori_loop(0, (n_far + FAR_UNROLL - 1) // FAR_UNROLL, far_group, tuple(state))
    has_adj = qi >= 1
    j_adj = jnp.maximum(qi - 1, 0)
    scores_into(0, j_adj)
    out = []
    for hd in range(2):
        s = s_ref[0, hd] + bias_ref[hd, 1]
        _, acc = update(state[hd], s, jnp.max(s, axis=0, keepdims=True), jnp.where(has_adj, 0.0, NEG),
                        v_t(j_adj, hd))
        out.append(acc[:MOBA_DH] / acc[MOBA_DH:MOBA_DH + 1])
    o_ref[...] = jnp.concatenate(out, axis=0).T.astype(o_ref.dtype)


def _t5_bucket(dist):
    n = jnp.maximum(dist, 0)
    max_exact = REL_BUCKETS // 2
    nf = jnp.maximum(n, 1).astype(F32)
    large = max_exact + (jnp.log(nf / max_exact) / math.log(REL_MAX_DIST / max_exact)
                         * (REL_BUCKETS - max_exact)).astype(jnp.int32)
    large = jnp.minimum(large, REL_BUCKETS - 1)
    return jnp.where(n < max_exact, n, large)


def _moba_bias_tables(rel_bias):
    tbl = rel_bias.astype(F32).T
    kpos = jnp.arange(MOBA_BLOCK)[:, None]
    qpos = jnp.arange(MOBA_BLOCK)[None, :]
    d_own = qpos - kpos
    own = jnp.where(d_own >= 0, tbl[:, _t5_bucket(d_own)], NEG)
    adj = tbl[:, _t5_bucket(d_own + MOBA_BLOCK)]
    return jnp.stack([own, adj], axis=1) * LOG2E, tbl[:, REL_BUCKETS - 1] * LOG2E


def _proj_even_kernel(x_ref, wb_ref, wf_ref, wvt_ref, hb_ref, hf_ref, vt_ref):
    xb = x_ref[...].astype(BF16)
    hb_ref[...] = jnp.dot(xb, wb_ref[...], preferred_element_type=F32).astype(hb_ref.dtype)
    hf_ref[...] = jnp.dot(xb, wf_ref[...], preferred_element_type=F32)
    for c in range(vt_ref.shape[0]):
        xc = xb[c * MOBA_BLOCK:(c + 1) * MOBA_BLOCK]
        vt_ref[c] = lax.dot_general(wvt_ref[...], xc, NT_DIMS, preferred_element_type=F32).astype(vt_ref.dtype)


def _proj_even(x, wb, wf, wvt):
    m, d = x.shape
    tm = min(ROW_TILE, m)
    nv = tm // MOBA_BLOCK
    full = lambda a: pl.BlockSpec(a.shape, lambda i: (0, 0))
    return pl.pallas_call(
        _proj_even_kernel,
        name="in_proj_even",
        grid=(m // tm,),
        in_specs=[pl.BlockSpec((tm, d), lambda i: (i, 0)), full(wb), full(wf), full(wvt)],
        out_specs=[pl.BlockSpec((tm, wb.shape[1]), lambda i: (i, 0)),
                   pl.BlockSpec((tm, wf.shape[1]), lambda i: (i, 0)),
                   pl.BlockSpec((nv, wvt.shape[0], MOBA_BLOCK), lambda i: (i, 0, 0))],
        out_shape=[jax.ShapeDtypeStruct((m, wb.shape[1]), BF16),
                   jax.ShapeDtypeStruct((m, wf.shape[1]), F32),
                   jax.ShapeDtypeStruct((m // MOBA_BLOCK, wvt.shape[0], MOBA_BLOCK), BF16)],
        compiler_params=_cparams(("parallel",)),
    )(x, wb, wf, wvt)


def _moba(hb, vt, rel_bias, batch, seq):
    m = hb.shape[0]
    nq = seq // MOBA_BLOCK
    nb_pad = LANES * (-(-nq // LANES))
    bias, c_far = _moba_bias_tables(rel_bias)
    hp_n = MOBA_HEADS // 2
    nb_gate = min(nb_pad, 16 * (-(-nq // 16)))
    kern = functools.partial(_moba_kernel, nb_pad=nb_pad, nb_gate=nb_gate)
    return pl.pallas_call(
        kern,
        name="moba",
        grid=(batch, hp_n, nq),
        in_specs=[pl.BlockSpec(memory_space=pltpu.SMEM),
                  pl.BlockSpec((MOBA_BLOCK, LANES), lambda b, h, i: (b * nq + i, h)),
                  pl.BlockSpec((seq, LANES), lambda b, h, i: (b, hp_n + h)),
                  pl.BlockSpec((nq, LANES, MOBA_BLOCK), lambda b, h, i: (b, h, 0)),
                  pl.BlockSpec((2, 2, MOBA_BLOCK, MOBA_BLOCK), lambda b, h, i: (h, 0, 0, 0))],
        out_specs=pl.BlockSpec((MOBA_BLOCK, LANES), lambda b, h, i: (b * nq + i, h)),
        out_shape=jax.ShapeDtypeStruct((m, MOBA_HEADS * MOBA_DH), BF16),
        scratch_shapes=[pltpu.VMEM((nb_pad, LANES), F32),
                        pltpu.VMEM((2, 2, MOBA_BLOCK, MOBA_BLOCK), F32),
                        pltpu.VMEM((2, 2, 1, MOBA_BLOCK), F32)],
        compiler_params=_cparams(("parallel", "parallel", "arbitrary")),
    )(c_far, hb, hb, vt, bias)


LEVELS = (32, 16, 8, 4, 2, 1)


def _chunk_cumsum(la):
    t = la.shape[0]
    row = lax.broadcasted_iota(jnp.int32, la.shape, 0) % CHUNK
    b = la
    sh = 1
    while sh < CHUNK:
        b = b + jnp.where(row >= sh, pltpu.roll(b, sh, 0), 0.0)
        sh *= 2
    return b


def _level_reference(b, h):
    t, w = b.shape
    if 2 * h >= 8:
        b3 = b.reshape(t // (2 * h), 2 * h, w)
        return jnp.broadcast_to(b3[:, h - 1:h, :], b3.shape).reshape(t, w)
    row = lax.broadcasted_iota(jnp.int32, b.shape, 0)
    if h == 2:
        r4 = row % 4
        return jnp.where(r4 == 0, pltpu.roll(b, t - 1, 0),
                         jnp.where(r4 == 1, b,
                                   jnp.where(r4 == 2, pltpu.roll(b, 1, 0), pltpu.roll(b, 2, 0))))
    return jnp.where(row % 2 == 1, pltpu.roll(b, 1, 0), b)


def _pair_masks():
    t = lax.broadcasted_iota(jnp.int32, (CHUNK, 2 * CHUNK), 0)
    s = lax.broadcasted_iota(jnp.int32, (CHUNK, 2 * CHUNK), 1) % CHUNK
    masks = []
    for h in LEVELS:
        masks.append((t // (2 * h) == s // (2 * h)) & ((t // h) % 2 == 1) & ((s // h) % 2 == 0))
    masks.append(t == s)
    return masks


def _stack_heads(x, split):
    lane = lax.broadcasted_iota(jnp.int32, x.shape, 1)
    z = jnp.zeros_like(x)
    return jnp.concatenate([jnp.where(lane < split, x, z), jnp.where(lane >= split, x, z)], axis=0)


def _gated_linear_tile(q, k, v, la, st_ref):
    t = q.shape[0]
    b = _chunk_cumsum(la)
    q_lv, k_lv = [], []
    for h in LEVELS:
        w = jnp.exp(-jnp.abs(b - _level_reference(b, h)))
        q_lv.append((q * w).astype(BF16))
        k_lv.append((k * w).astype(BF16))
    q_lv.append(q.astype(BF16))
    k_lv.append(k.astype(BF16))
    b3 = b.reshape(t // CHUNK, CHUNK, LANES)
    b_last = jnp.broadcast_to(b3[:, CHUNK - 1:CHUNK, :], b3.shape).reshape(t, LANES)
    q_in = (q * jnp.exp(b)).astype(BF16)
    k_out = (k * jnp.exp(b_last - b)).astype(BF16)
    masks = _pair_masks()
    e = lax.broadcasted_iota(jnp.int32, st_ref.shape, 0)
    d = lax.broadcasted_iota(jnp.int32, st_ref.shape, 1)
    same_head = (e < GLA_DV) == (d < GLA_DK)
    outs = []
    for c in range(t // CHUNK):
        rows = slice(c * CHUNK, (c + 1) * CHUNK)
        a = jnp.zeros((CHUNK, 2 * CHUNK), F32)
        for lv in range(len(masks)):
            p = lax.dot_general(q_lv[lv][rows], _stack_heads(k_lv[lv][rows], GLA_DK), NT_DIMS,
                                preferred_element_type=F32)
            a = a + jnp.where(masks[lv], p, 0.0)
        vc = v[rows]
        o = jnp.dot(a.astype(BF16), _stack_heads(vc, GLA_DV), preferred_element_type=F32)
        st = st_ref[...]
        o = o + lax.dot_general(q_in[rows], st.astype(BF16), NT_DIMS, preferred_element_type=F32)
        upd = lax.dot_general(vc, k_out[rows], TN_DIMS, preferred_element_type=F32)
        st_ref[...] = st * jnp.exp(b[(c + 1) * CHUNK - 1:(c + 1) * CHUNK, :]) + jnp.where(same_head, upd, 0.0)
        outs.append(o)
    return jnp.concatenate(outs, axis=0)


def _head_rms_gate(o, norm, gate):
    parts = []
    for hd in range(2):
        oh = o[:, hd * GLA_DV:(hd + 1) * GLA_DV]
        parts.append(oh * lax.rsqrt(jnp.mean(oh * oh, axis=-1, keepdims=True) + HEAD_NORM_EPS))
    return jnp.concatenate(parts, axis=1) * norm * _silu(gate)


def _log_sigmoid(x):
    return jnp.minimum(x, 0.0) - jnp.log1p(jnp.exp(-jnp.abs(x)))


def _gla_kernel(q_ref, k_ref, v_ref, ga_ref, gr_ref, wg_ref, bg_ref, norm_ref, o_ref, st_ref):
    @pl.when(pl.program_id(2) == 0)
    def _():
        st_ref[...] = jnp.zeros_like(st_ref)

    gate_in = jnp.dot(ga_ref[...].astype(BF16), wg_ref[...], preferred_element_type=F32) + bg_ref[...]
    la = _log_sigmoid(gate_in) / GLA_TAU
    q = q_ref[...].astype(F32) * (GLA_DK ** -0.5)
    o = _gated_linear_tile(q, k_ref[...].astype(F32), v_ref[...], la, st_ref)
    o_ref[...] = _head_rms_gate(o, norm_ref[...], gr_ref[...]).astype(o_ref.dtype)


def _gla(hb, hf, w_gate, b_gate, norm, batch, seq):
    m = hb.shape[0]
    tt = min(SEQ_TILE, seq)
    nt = seq // tt
    hp_n = GLA_HEADS // 2
    wg = jnp.zeros((LANES, GLA_HEADS * GLA_DK), BF16).at[:GLA_RANK].set(w_gate.astype(BF16))
    row = lambda b, h, i: b * nt + i
    return pl.pallas_call(
        _gla_kernel,
        name="gla",
        grid=(batch, hp_n, nt),
        in_specs=[pl.BlockSpec((tt, LANES), lambda b, h, i: (row(b, h, i), 8 + h)),
                  pl.BlockSpec((tt, LANES), lambda b, h, i: (row(b, h, i), 10 + h)),
                  pl.BlockSpec((tt, 2 * GLA_DV), lambda b, h, i: (row(b, h, i), 6 + h)),
                  pl.BlockSpec((tt, LANES), lambda b, h, i: (row(b, h, i), 4)),
                  pl.BlockSpec((tt, 2 * GLA_DV), lambda b, h, i: (row(b, h, i), h)),
                  pl.BlockSpec((LANES, LANES), lambda b, h, i: (0, h)),
                  pl.BlockSpec((1, LANES), lambda b, h, i: (0, h)),
                  pl.BlockSpec((1, 2 * GLA_DV), lambda b, h, i: (0, h))],
        out_specs=pl.BlockSpec((tt, 2 * GLA_DV), lambda b, h, i: (row(b, h, i), h)),
        out_shape=jax.ShapeDtypeStruct((m, GLA_HEADS * GLA_DV), BF16),
        scratch_shapes=[pltpu.VMEM((2 * GLA_DV, LANES), F32)],
        compiler_params=_cparams(("parallel", "parallel", "arbitrary")),
    )(hb, hb, hb, hf, hf, wg, b_gate.reshape(1, -1), norm.reshape(1, -1))


def _hgrn_kernel(q_ref, f_ref, v_ref, g_ref, lb_ref, norm_ref, o_ref, st_ref):
    @pl.when(pl.program_id(2) == 0)
    def _():
        st_ref[...] = jnp.zeros_like(st_ref)

    lb = lb_ref[...]
    f = lb + (1.0 - lb) * _sigmoid(f_ref[...])
    o = _gated_linear_tile(q_ref[...], 1.0 - f, v_ref[...].astype(BF16), jnp.log(f), st_ref)
    o_ref[...] = _head_rms_gate(o, norm_ref[...], g_ref[...]).astype(o_ref.dtype)


def _hgrn(h, lb, norm, batch, seq):
    m = h.shape[0]
    tt = min(SEQ_TILE, seq)
    nt = seq // tt
    row = lambda b, hh, i: b * nt + i
    return pl.pallas_call(
        _hgrn_kernel,
        name="hgrn",
        grid=(batch, HGRN_HEADS // 2, nt),
        in_specs=[pl.BlockSpec((tt, LANES), lambda b, hh, i: (row(b, hh, i), hh)),
                  pl.BlockSpec((tt, LANES), lambda b, hh, i: (row(b, hh, i), 2 + hh)),
                  pl.BlockSpec((tt, 2 * GLA_DV), lambda b, hh, i: (row(b, hh, i), 2 + hh)),
                  pl.BlockSpec((tt, 2 * GLA_DV), lambda b, hh, i: (row(b, hh, i), 4 + hh)),
                  pl.BlockSpec((1, LANES), lambda b, hh, i: (0, hh)),
                  pl.BlockSpec((1, 2 * GLA_DV), lambda b, hh, i: (0, hh))],
        out_specs=pl.BlockSpec((tt, 2 * GLA_DV), lambda b, hh, i: (row(b, hh, i), hh)),
        out_shape=jax.ShapeDtypeStruct((m, HGRN_HEADS * GLA_DV), BF16),
        scratch_shapes=[pltpu.VMEM((2 * GLA_DV, LANES), F32)],
        compiler_params=_cparams(("parallel", "parallel", "arbitrary")),
    )(h, h, h, h, lb.reshape(1, -1), norm.reshape(1, -1))


def _rotate(x, cos, sin_signed):
    lane = lax.broadcasted_iota(jnp.int32, x.shape, 1)
    half = GLA_DK // 2
    swapped = jnp.where(lane % GLA_DK < half, pltpu.roll(x, LANES - half, 1), pltpu.roll(x, half, 1))
    return x * cos + swapped * sin_signed


def _ret_kernel(q_ref, k_ref, v_ref, g_ref, cos_ref, sin_ref, dmask_ref, zeta_ref, xi_ref, gch_ref,
                norm_ref, o_ref, st_ref):
    @pl.when(pl.program_id(2) == 0)
    def _():
        st_ref[...] = jnp.zeros_like(st_ref)

    t = q_ref.shape[0]
    cos, sin = cos_ref[...], sin_ref[...]
    q = _rotate(q_ref[...], cos, sin)
    k = _rotate(k_ref[...], cos, sin) * (GLA_DK ** -0.5)
    v = v_ref[...].astype(BF16)
    nc = t // CHUNK
    zeta = jnp.broadcast_to(zeta_ref[0][None], (nc, CHUNK, LANES)).reshape(t, LANES)
    xi = jnp.broadcast_to(xi_ref[0][None], (nc, CHUNK, LANES)).reshape(t, LANES)
    q_b, k_b = q.astype(BF16), k.astype(BF16)
    q_in = (q * xi).astype(BF16)
    k_out = (k * zeta).astype(BF16)
    dmask = dmask_ref[0]
    g_chunk = gch_ref[0]
    e = lax.broadcasted_iota(jnp.int32, st_ref.shape, 0)
    d = lax.broadcasted_iota(jnp.int32, st_ref.shape, 1)
    same_head = (e < GLA_DV) == (d < GLA_DK)
    outs = []
    for c in range(nc):
        rows = slice(c * CHUNK, (c + 1) * CHUNK)
        a = lax.dot_general(q_b[rows], _stack_heads(k_b[rows], GLA_DK), NT_DIMS,
                            preferred_element_type=F32) * dmask
        vc = v[rows]
        o = jnp.dot(a.astype(BF16), _stack_heads(vc, GLA_DV), preferred_element_type=F32)
        st = st_ref[...]
        o = o + lax.dot_general(q_in[rows], st.astype(BF16), NT_DIMS, preferred_element_type=F32)
        upd = lax.dot_general(vc, k_out[rows], TN_DIMS, preferred_element_type=F32)
        st_ref[...] = st * g_chunk + jnp.where(same_head, upd, 0.0)
        outs.append(o)
    o = jnp.concatenate(outs, axis=0)
    parts = []
    for hd in range(2):
        oh = o[:, hd * GLA_DV:(hd + 1) * GLA_DV]
        oh = oh - jnp.mean(oh, axis=-1, keepdims=True)
        parts.append(oh * lax.rsqrt(jnp.mean(oh * oh, axis=-1, keepdims=True) + HEAD_NORM_EPS))
    o_ref[...] = (jnp.concatenate(parts, axis=1) * norm_ref[...] * _silu(g_ref[...])).astype(o_ref.dtype)


def _retention_tables(seq):
    half = GLA_DK // 2
    inv = ROPE_BASE ** (-jnp.arange(half, dtype=F32) / half)
    ang = jnp.arange(seq, dtype=F32)[:, None] * inv[None, :]
    cos, sin = jnp.cos(ang), jnp.sin(ang)
    cos_t = jnp.tile(cos, (1, 4))
    sin_t = jnp.tile(jnp.concatenate([-sin, sin], axis=1), (1, 2))
    log_g = jnp.log1p(-jnp.exp2(-5.0 - jnp.arange(RET_HEADS, dtype=F32)))
    pos = jnp.arange(CHUNK, dtype=F32)
    rel = pos[:, None] - pos[None, :]
    dmask = jnp.where(rel >= 0, jnp.exp(jnp.maximum(rel, 0.0)[None] * log_g[:, None, None]), 0.0)
    zeta = jnp.exp((CHUNK - 1 - pos)[None, :] * log_g[:, None])
    xi = jnp.exp((pos + 1.0)[None, :] * log_g[:, None])
    g_chunk = jnp.exp(CHUNK * log_g)
    hp_n = RET_HEADS // 2
    pair = lambda x: x.reshape(hp_n, 2, *x.shape[1:])
    dmask_p = jnp.concatenate([pair(dmask)[:, 0], pair(dmask)[:, 1]], axis=-1)
    lanes = lambda x: jnp.repeat(pair(x), GLA_DK, axis=1)
    zeta_p = jnp.swapaxes(lanes(zeta), 1, 2)
    xi_p = jnp.swapaxes(lanes(xi), 1, 2)
    gch_p = lanes(g_chunk)[:, None, :]
    return cos_t, sin_t, dmask_p, zeta_p, xi_p, gch_p


def _retention(h, norm, batch, seq):
    m = h.shape[0]
    tt = min(SEQ_TILE, seq)
    nt = seq // tt
    cos_t, sin_t, dmask, zeta, xi, gch = _retention_tables(seq)
    row = lambda b, hh, i: b * nt + i
    const3 = lambda shape: pl.BlockSpec((1,) + shape, lambda b, hh, i: (hh, 0, 0))
    return pl.pallas_call(
        _ret_kernel,
        name="retention",
        grid=(batch, RET_HEADS // 2, nt),
        in_specs=[pl.BlockSpec((tt, LANES), lambda b, hh, i: (row(b, hh, i), 12 + hh)),
                  pl.BlockSpec((tt, LANES), lambda b, hh, i: (row(b, hh, i), 14 + hh)),
                  pl.BlockSpec((tt, 2 * GLA_DV), lambda b, hh, i: (row(b, hh, i), 8 + hh)),
                  pl.BlockSpec((tt, 2 * GLA_DV), lambda b, hh, i: (row(b, hh, i), 10 + hh)),
                  pl.BlockSpec((tt, LANES), lambda b, hh, i: (i, 0)),
                  pl.BlockSpec((tt, LANES), lambda b, hh, i: (i, 0)),
                  const3((CHUNK, 2 * CHUNK)), const3((CHUNK, LANES)), const3((CHUNK, LANES)),
                  const3((1, LANES)),
                  pl.BlockSpec((1, 2 * GLA_DV), lambda b, hh, i: (0, hh))],
        out_specs=pl.BlockSpec((tt, 2 * GLA_DV), lambda b, hh, i: (row(b, hh, i), hh)),
        out_shape=jax.ShapeDtypeStruct((m, RET_HEADS * GLA_DV), BF16),
        scratch_shapes=[pltpu.VMEM((2 * GLA_DV, LANES), F32)],
        compiler_params=_cparams(("parallel", "parallel", "arbitrary")),
    )(h, h, h, h, cos_t, sin_t, dmask, zeta, xi, gch, norm.reshape(1, -1))


def _route(xn, w, b):
    logits = jnp.dot(xn.astype(BF16), w, preferred_element_type=F32) + b
    lane = lax.broadcasted_iota(jnp.int32, logits.shape, 1)
    lane_f = lane.astype(F32)
    logits = jnp.where(lane < N_EXPERTS, logits, -jnp.inf)
    m1 = jnp.max(logits, axis=1, keepdims=True)
    i1 = jnp.min(jnp.where(logits == m1, lane_f, float(LANES)), axis=1, keepdims=True)
    rest = jnp.where(lane_f == i1, -jnp.inf, logits)
    m2 = jnp.max(rest, axis=1, keepdims=True)
    i2 = jnp.min(jnp.where(rest == m2, lane_f, float(LANES)), axis=1, keepdims=True)
    z = jnp.exp(m2 - m1)
    g1 = 1.0 / (1.0 + z)
    experts = jnp.where(lane == 0, i1, jnp.where(lane == 1, i2, 0.0)).astype(jnp.int32)
    gates = jnp.where(lane == 0, g1, jnp.where(lane == 1, z * g1, 0.0))
    return experts, gates


def _outproj_kernel(oa_ref, ob_ref, x_ref, wa_ref, wb_ref, g_ref, b_ref, *rest, alpha):
    y = jnp.dot(oa_ref[...], wa_ref[...], preferred_element_type=F32)
    y = y + jnp.dot(ob_ref[...], wb_ref[...], preferred_element_type=F32)
    xn = _layer_norm(alpha * x_ref[...] + y, g_ref[...], b_ref[...])
    if len(rest) == 1:
        rest[0][...] = xn
    else:
        wr_ref, br_ref, o_ref, e_ref, gt_ref = rest
        o_ref[...] = xn
        e_ref[...], gt_ref[...] = _route(xn, wr_ref[...], br_ref[...])


def _outproj_ln(oa, ob, x, w_out, g, b, alpha, router=None):
    m, d = x.shape
    tm = min(ROW_TILE, m)
    wa = w_out[:oa.shape[1]].astype(BF16)
    wb = w_out[oa.shape[1]:].astype(BF16)
    full = lambda a: pl.BlockSpec(a.shape, lambda i: (0, 0))
    rows = lambda a: pl.BlockSpec((tm, a.shape[1]), lambda i: (i, 0))
    args = [oa, ob, x, wa, wb, g.reshape(1, -1), b.reshape(1, -1)]
    in_specs = [rows(oa), rows(ob), rows(x)] + [full(a) for a in args[3:]]
    out_specs = [rows(x)]
    out_shape = [jax.ShapeDtypeStruct((m, d), F32)]
    if router is not None:
        wr = jnp.zeros((d, LANES), F32).at[:, :N_EXPERTS].set(router[0]).astype(BF16)
        br = jnp.zeros((1, LANES), F32).at[0, :N_EXPERTS].set(router[1])
        args += [wr, br]
        in_specs += [full(wr), full(br)]
        lanes = pl.BlockSpec((tm, LANES), lambda i: (i, 0))
        out_specs += [lanes, lanes]
        out_shape += [jax.ShapeDtypeStruct((m, LANES), jnp.int32), jax.ShapeDtypeStruct((m, LANES), F32)]
    out = pl.pallas_call(
        functools.partial(_outproj_kernel, alpha=alpha),
        name="out_proj_ln",
        grid=(m // tm,),
        in_specs=in_specs,
        out_specs=out_specs,
        out_shape=out_shape,
        compiler_params=_cparams(("parallel",)),
    )(*args)
    if router is None:
        return out[0]
    return out[0], out[1][:, :TOP_K], out[2][:, :TOP_K]


def _ln_ple(x, ffn, p, g, b, wg, wp, alpha):
    y = _layer_norm(alpha * x + ffn, g, b)
    gate = _sigmoid(jnp.dot(y.astype(BF16), wg, preferred_element_type=F32))
    return y + gate * jnp.dot(p.astype(BF16), wp, preferred_element_type=F32)


def _swiglu_chunks(xb_ref, w1, w3, w2):
    n = xb_ref.shape[0] // SWIGLU_ROWS
    h = []
    for c in range(n):
        xc = xb_ref[c * SWIGLU_ROWS:(c + 1) * SWIGLU_ROWS, :]
        h.append((jnp.dot(xc, w1, preferred_element_type=F32), jnp.dot(xc, w3, preferred_element_type=F32)))
    return [jnp.dot((_silu(h1) * h3).astype(BF16), w2, preferred_element_type=F32) for h1, h3 in h]


def _accumulate_steps(parts, acc_ref, finish, n_steps):
    assert n_steps > 1
    f = pl.program_id(1)
    last = n_steps - 1
    chunk = lambda c: slice(c * SWIGLU_ROWS, (c + 1) * SWIGLU_ROWS)

    @pl.when(f == 0)
    def _():
        for c, part in enumerate(parts):
            acc_ref[chunk(c), :] = part

    @pl.when((f > 0) & (f < last))
    def _():
        for c, part in enumerate(parts):
            acc_ref[chunk(c), :] += part

    @pl.when(f == last)
    def _():
        finish(jnp.concatenate([acc_ref[chunk(c), :] + part for c, part in enumerate(parts)], axis=0))


def _swiglu_kernel(x_ref, p_ref, w1_ref, w3_ref, w2_ref, g_ref, b_ref, wg_ref, wp_ref, o_ref,
                   xb_ref, acc_ref, *, alpha, n_steps):
    @pl.when(pl.program_id(1) == 0)
    def _():
        xb_ref[...] = x_ref[...].astype(BF16)

    def finish(ffn):
        o_ref[...] = _ln_ple(x_ref[...], ffn, p_ref[...], g_ref[...], b_ref[...], wg_ref[...], wp_ref[...],
                             alpha)

    _accumulate_steps(_swiglu_chunks(xb_ref, w1_ref[...], w3_ref[...], w2_ref[...]), acc_ref, finish,
                      n_steps)


def _swiglu_ln_ple(x, p, w1, w3, w2, g, b, wg, wp, alpha):
    m, d = x.shape
    tm = min(ROW_TILE, m)
    pad = FFN_F_PAD - w1.shape[1]
    w1p = jnp.pad(w1.astype(BF16), ((0, 0), (0, pad)))
    w3p = jnp.pad(w3.astype(BF16), ((0, 0), (0, pad)))
    w2p = jnp.pad(w2.astype(BF16), ((0, pad), (0, 0)))
    nf = FFN_F_PAD // FFN_F_TILE
    g2, b2 = g.reshape(1, -1), b.reshape(1, -1)
    wgb, wpb = wg.astype(BF16), wp.astype(BF16)
    full = lambda a: pl.BlockSpec(a.shape, lambda i, f: (0, 0))
    rows = lambda a: pl.BlockSpec((tm, a.shape[1]), lambda i, f: (i, 0))
    return pl.pallas_call(
        functools.partial(_swiglu_kernel, alpha=alpha, n_steps=nf),
        name="swiglu_ln_ple",
        grid=(m // tm, nf),
        in_specs=[rows(x), rows(p),
                  pl.BlockSpec((d, FFN_F_TILE), lambda i, f: (0, f)),
                  pl.BlockSpec((d, FFN_F_TILE), lambda i, f: (0, f)),
                  pl.BlockSpec((FFN_F_TILE, d), lambda i, f: (f, 0)),
                  full(g2), full(b2), full(wgb), full(wpb)],
        out_specs=rows(x),
        out_shape=jax.ShapeDtypeStruct((m, d), F32),
        scratch_shapes=[pltpu.VMEM((tm, d), BF16), pltpu.VMEM((tm, d), F32)],
        compiler_params=_cparams(("parallel", "arbitrary")),
    )(x, p, w1p, w3p, w2p, g2, b2, wgb, wpb)


def _experts_kernel(be_ref, x_ref, w1_ref, w3_ref, w2_ref, o_ref, acc_ref, *, n_steps):
    def finish(y):
        o_ref[...] = y.astype(o_ref.dtype)

    _accumulate_steps(_swiglu_chunks(x_ref, w1_ref[0], w3_ref[0], w2_ref[0]), acc_ref, finish, n_steps)


def _experts(xg, blk_e, w1, w3, w2):
    r, d = xg.shape
    fe = w1.shape[2]
    nf = fe // MOE_F_TILE
    grid_spec = pltpu.PrefetchScalarGridSpec(
        num_scalar_prefetch=1,
        grid=(r // MOE_ROW_TILE, nf),
        in_specs=[pl.BlockSpec((MOE_ROW_TILE, d), lambda i, f, be: (i, 0)),
                  pl.BlockSpec((1, d, MOE_F_TILE), lambda i, f, be: (be[i], 0, f)),
                  pl.BlockSpec((1, d, MOE_F_TILE), lambda i, f, be: (be[i], 0, f)),
                  pl.BlockSpec((1, MOE_F_TILE, d), lambda i, f, be: (be[i], f, 0))],
        out_specs=pl.BlockSpec((MOE_ROW_TILE, d), lambda i, f, be: (i, 0)),
        scratch_shapes=[pltpu.VMEM((MOE_ROW_TILE, d), F32)],
    )
    return pl.pallas_call(
        functools.partial(_experts_kernel, n_steps=nf),
        name="experts",
        grid_spec=grid_spec,
        out_shape=jax.ShapeDtypeStruct((r, d), BF16),
        compiler_params=_cparams(("parallel", "arbitrary")),
    )(blk_e, xg, w1, w3, w2)


def _combine_kernel(x_ref, y_ref, gt_ref, p_ref, g_ref, b_ref, wg_ref, wp_ref, o_ref, *, alpha):
    gt = gt_ref[...]
    d = x_ref.shape[1]
    ffn = y_ref[:, :d].astype(F32) * gt[:, 0:1] + y_ref[:, d:].astype(F32) * gt[:, 1:2]
    o_ref[...] = _ln_ple(x_ref[...], ffn, p_ref[...], g_ref[...], b_ref[...], wg_ref[...], wp_ref[...],
                         alpha)


def _combine_ln_ple(x, y, gates, p, g, b, wg, wp, alpha):
    m, d = x.shape
    tm = min(ROW_TILE, m)
    g2, b2 = g.reshape(1, -1), b.reshape(1, -1)
    wgb, wpb = wg.astype(BF16), wp.astype(BF16)
    gt = jnp.zeros((m, LANES), F32).at[:, :TOP_K].set(gates)
    full = lambda a: pl.BlockSpec(a.shape, lambda i: (0, 0))
    rows = lambda a: pl.BlockSpec((tm, a.shape[1]), lambda i: (i, 0))
    return pl.pallas_call(
        functools.partial(_combine_kernel, alpha=alpha),
        name="combine_ln_ple",
        grid=(m // tm,),
        in_specs=[rows(x), rows(y), rows(gt), rows(p), full(g2), full(b2), full(wgb), full(wpb)],
        out_specs=rows(x),
        out_shape=jax.ShapeDtypeStruct((m, d), F32),
        compiler_params=_cparams(("parallel",)),
    )(x, y, gt, p, g2, b2, wgb, wpb)


def _moe_ln_ple(x, top_e, gates, p, w1, w3, w2, g, b, wg, wp, alpha):
    m, d = x.shape
    n_asg = m * TOP_K
    e_flat = top_e.reshape(-1)
    onehot = (jnp.arange(N_EXPERTS, dtype=jnp.int32)[:, None] == e_flat[None, :]).astype(jnp.int32)
    csum = jnp.cumsum(onehot, axis=1)
    rank = jnp.sum(csum * onehot, axis=0) - 1
    counts = csum[:, -1]
    starts = jnp.cumsum(counts) - counts
    padded = (counts + MOE_ROW_TILE - 1) // MOE_ROW_TILE * MOE_ROW_TILE
    p_ends = jnp.cumsum(padded)
    p_starts = p_ends - padded
    pos = (p_starts[e_flat] + rank).reshape(m, TOP_K)
    n_rows = (n_asg // MOE_ROW_TILE + N_EXPERTS) * MOE_ROW_TILE
    blk_start = jnp.arange(n_rows // MOE_ROW_TILE, dtype=jnp.int32) * MOE_ROW_TILE
    blk_e = jnp.minimum(jnp.searchsorted(p_ends, blk_start, side='right'), N_EXPERTS - 1).astype(jnp.int32)
    tok_s = (jnp.argsort(e_flat) // TOP_K).astype(jnp.int32)
    row = jnp.arange(n_rows, dtype=jnp.int32)
    row_e = jnp.repeat(blk_e, MOE_ROW_TILE)
    local = row - p_starts[row_e]
    tok_buf = jnp.where(local < counts[row_e], tok_s[jnp.minimum(starts[row_e] + local, n_asg - 1)], 0)
    xg = x.astype(BF16)[tok_buf]
    y = _experts(xg, blk_e, w1.astype(BF16), w3.astype(BF16), w2.astype(BF16))
    y_tok = y[pos.reshape(-1)].reshape(m, TOP_K * d)
    return _combine_ln_ple(x, y_tok, gates, p, g, b, wg, wp, alpha)


def kernel(x, p, rel_bias, even_w_in, gla_w_gate, gla_b_gate, gla_norm, even_w_out, odd_w_in, hgrn_gamma,
           hgrn_norm, ret_norm, odd_w_out, ln_mix_g, ln_mix_b, ln_ffn_g, ln_ffn_b, ffn_w1, ffn_w3, ffn_w2,
           router_w, router_b, expert_w1, expert_w3, expert_w2, ple_w_gate, ple_w_proj):
    batch, seq, d = x.shape
    depth = p.shape[0]
    m = batch * seq
    alpha = (2.0 * depth) ** 0.25
    lb_p = jax.nn.softmax(hgrn_gamma.astype(F32), axis=0)
    lb_all = jnp.cumsum(lb_p, axis=0) - lb_p[0]
    xf = x.reshape(m, d)
    pf = p.reshape(depth, m, PLE_DIM)
    n_mq = 3 * MOBA_HEADS * MOBA_DH + 2 * GLA_HEADS * GLA_DK + GLA_HEADS * GLA_DV
    for i in range(depth):
        j = i // 2
        if i % 2 == 0:
            w_in = even_w_in[j]
            n_qk = 2 * MOBA_HEADS * MOBA_DH
            n_v = MOBA_HEADS * MOBA_DH
            w_q = w_in[:, :n_v] * (MOBA_DH ** -0.5 * LOG2E)
            w_b = jnp.concatenate([w_q, w_in[:, n_v:n_qk], w_in[:, n_qk + n_v:n_mq]], axis=1).astype(BF16)
            w_vt = w_in[:, n_qk:n_qk + n_v].T.astype(BF16)
            w_f = jnp.concatenate([w_in[:, n_mq + GLA_RANK:], w_in[:, n_mq:n_mq + GLA_RANK],
                                   jnp.zeros((d, LANES - GLA_RANK), F32)], axis=1).astype(BF16)
            hb, hf, vt = _proj_even(xf, w_b, w_f, w_vt)
            o_a = _moba(hb, vt, rel_bias, batch, seq)
            o_b = _gla(hb, hf, gla_w_gate[j], gla_b_gate[j], gla_norm[j], batch, seq)
            xf = _outproj_ln(o_a, o_b, xf, even_w_out[j], ln_mix_g[i], ln_mix_b[i], alpha)
            xf = _swiglu_ln_ple(xf, pf[i], ffn_w1[j], ffn_w3[j], ffn_w2[j], ln_ffn_g[i], ln_ffn_b[i],
                                ple_w_gate[i], ple_w_proj[i], alpha)
        else:
            h = _proj(xf, odd_w_in[j].astype(BF16), F32)
            o_c = _hgrn(h, lb_all[i], hgrn_norm[j], batch, seq)
            o_d = _retention(h, ret_norm[j], batch, seq)
            xf, top_e, gates = _outproj_ln(o_c, o_d, xf, odd_w_out[j], ln_mix_g[i], ln_mix_b[i], alpha,
                                           router=(router_w[j], router_b[j]))
            xf = _moe_ln_ple(xf, top_e, gates, pf[i], expert_w1[j], expert_w3[j], expert_w2[j],
                             ln_ffn_g[i], ln_ffn_b[i], ple_w_gate[i], ple_w_proj[i], alpha)
    return xf.reshape(batch, seq, d)
```

```python
import functools
import math

import numpy as np
import jax
import jax.numpy as jnp
from jax import lax
from jax.experimental import pallas as pl
from jax.experimental.pallas import tpu as pltpu

D_MODEL = 1024
PLE_DIM = 256
LN_EPS = 1e-5
HEAD_NORM_EPS = 1e-6
MOBA_HEADS, MOBA_DH, MOBA_BLOCK, MOBA_TOPK = 8, 64, 256, 3
REL_BUCKETS, REL_MAX_DIST = 32, 128
GLA_HEADS, GLA_DK, GLA_DV, GLA_RANK, GLA_TAU = 4, 64, 128, 16, 16.0
HGRN_HEADS = 4
RET_HEADS = 4
ROPE_BASE = 10000.0
CHUNK = 64
FFN_DENSE = 2752
N_EXPERTS, TOP_K, FFN_EXPERT = 8, 2, 3584

LANES = 128
VMEM_LIMIT = 56 * 1024 * 1024
ROW_TILE = 512
SEQ_TILE = 512
MOE_ROW_TILE = 512
MOE_F_TILE = 1792
FFN_F_PAD = 2816
FFN_F_TILE = 1408
SWIGLU_ROWS = 256
NEG = -1e30
LOG2E = math.log2(math.e)
FAR_UNROLL = 4
DENOM_ROWS = 16

BF16 = jnp.bfloat16
F32 = jnp.float32
NT_DIMS = (((1,), (1,)), ((), ()))
TN_DIMS = (((0,), (0,)), ((), ()))


def _cparams(sem):
    return pltpu.CompilerParams(dimension_semantics=sem, vmem_limit_bytes=VMEM_LIMIT)


def _sigmoid(x):
    return 1.0 / (1.0 + jnp.exp(-x))


def _silu(x):
    return x * _sigmoid(x)


def _layer_norm(y, g, b):
    yc = y - jnp.mean(y, axis=-1, keepdims=True)
    return yc * lax.rsqrt(jnp.mean(yc * yc, axis=-1, keepdims=True) + LN_EPS) * g + b


def _proj_kernel(x_ref, w_ref, o_ref):
    o_ref[...] = jnp.dot(x_ref[...].astype(BF16), w_ref[...],
                         preferred_element_type=F32).astype(o_ref.dtype)


def _proj(x, w, out_dtype):
    m, d = x.shape
    n = w.shape[1]
    tm = min(ROW_TILE, m)
    return pl.pallas_call(
        _proj_kernel,
        name="in_proj",
        grid=(m // tm,),
        in_specs=[pl.BlockSpec((tm, d), lambda i: (i, 0)),
                  pl.BlockSpec((d, n), lambda i: (0, 0))],
        out_specs=pl.BlockSpec((tm, n), lambda i: (i, 0)),
        out_shape=jax.ShapeDtypeStruct((m, n), out_dtype),
        compiler_params=_cparams(("parallel",)),
    )(x, w)


def _moba_kernel(c_far_ref, q_ref, k_ref, vt_ref, bias_ref, o_ref, kmean_ref, s_ref, smax_ref, *,
                 nb_pad, nb_gate):
    hp = pl.program_id(1)
    qi = pl.program_id(2)
    tq = MOBA_BLOCK

    @pl.when(qi == 0)
    def _():
        kmean_ref[...] = jnp.zeros_like(kmean_ref)

    k_own = k_ref[pl.ds(pl.multiple_of(qi * tq, tq), tq), :]
    q_tb = q_ref[...].astype(F32).T.astype(BF16)
    kmean = kmean_ref[0:nb_gate, :]
    kmean_ref[pl.ds(qi, 1), :] = jnp.mean(k_own.astype(F32), axis=0, keepdims=True)

    km_hi = kmean.astype(BF16)
    km_lo = (kmean - km_hi.astype(F32)).astype(BF16)
    kdim = lax.broadcasted_iota(jnp.int32, (nb_gate, LANES), 1)
    zero_k = jnp.zeros_like(km_hi)
    gate_lhs = jnp.concatenate(
        [jnp.concatenate([jnp.where((kdim < MOBA_DH) == (hd == 0), part, zero_k) for part in (km_hi, km_lo)],
                         axis=1) for hd in range(2)], axis=0)
    gate = jnp.dot(gate_lhs, jnp.concatenate([q_tb, q_tb], axis=0), preferred_element_type=F32)

    dim = lax.broadcasted_iota(jnp.int32, (LANES, tq), 0)
    blk_t = lax.broadcasted_iota(jnp.int32, (nb_gate, tq), 0)
    blk_tf = blk_t.astype(F32)
    blk = lax.broadcasted_iota(jnp.int32, (tq, nb_pad), 1)
    qs, qa = [], []
    for hd in range(2):
        g = jnp.where(blk_t < qi, gate[hd * nb_gate:(hd + 1) * nb_gate], -jnp.inf)
        sel = jnp.zeros((nb_gate, tq), jnp.bool_)
        for _ in range(MOBA_TOPK):
            gmax = jnp.max(g, axis=0, keepdims=True)
            idx = jnp.min(jnp.where(g == gmax, blk_tf, float(nb_pad)), axis=0, keepdims=True)
            pick = (blk_tf == idx) & (gmax > -jnp.inf)
            sel = sel | pick
            g = jnp.where(pick, -jnp.inf, g)
        qh_t = jnp.where((dim < MOBA_DH) == (hd == 0), q_tb, jnp.zeros_like(q_tb))
        rows = [qh_t, jnp.where(sel, 0.0, NEG).astype(BF16)]
        if nb_pad > nb_gate:
            rows.append(jnp.zeros((nb_pad - nb_gate, tq), BF16))
        qs.append(qh_t)
        qa.append(jnp.concatenate(rows, axis=0))

    ones_rows = jnp.ones((DENOM_ROWS, tq), BF16)

    def v_t(j, hd):
        return jnp.concatenate([vt_ref[j, hd * MOBA_DH:(hd + 1) * MOBA_DH, :], ones_rows], axis=0)

    def scores_into(slot, j):
        onehot = jnp.where(blk == j, 1.0, 0.0).astype(BF16)
        ka = jnp.concatenate([k_ref[pl.ds(pl.multiple_of(j * tq, tq), tq), :], onehot], axis=1)
        for hd in range(2):
            s = jnp.dot(ka, qa[hd], preferred_element_type=F32)
            s_ref[slot, hd] = s
            smax_ref[slot, hd] = jnp.max(s, axis=0, keepdims=True)

    def update(state, s, smax, shift, vt):
        m, acc = state
        m_new = jnp.maximum(m, smax + shift)
        p = jnp.exp2(s - (m_new - shift))
        acc = jnp.exp2(m - m_new) * acc + jnp.dot(vt, p.astype(BF16), preferred_element_type=F32)
        return m_new, acc

    s_own = [jnp.dot(k_own, qs[hd], preferred_element_type=F32) + bias_ref[hd, 0] for hd in range(2)]
    scores_into(0, 0)
    state = []
    for hd in range(2):
        m0 = jnp.max(s_own[hd], axis=0, keepdims=True)
        p = jnp.exp2(s_own[hd] - m0)
        state.append((m0, jnp.dot(v_t(qi, hd), p.astype(BF16), preferred_element_type=F32)))

    n_far = qi - 1

    last = pl.num_programs(2) - 1

    def far_group(jj, st):
        a = FAR_UNROLL * jj
        for i in range(FAR_UNROLL):
            scores_into((i + 1) % 2, jnp.minimum(a + i + 1, last))
            shift = [jnp.where(a + i < n_far, c_far_ref[2 * hp + hd], NEG) for hd in range(2)]
            st = tuple(update(st[hd], s_ref[i % 2, hd], smax_ref[i % 2, hd], shift[hd],
                              v_t(jnp.minimum(a + i, last), hd)) for hd in range(2))
        return st

    state = lax.fori_loop(0, (n_far + FAR_UNROLL - 1) // FAR_UNROLL, far_group, tuple(state))
    has_adj = qi >= 1
    j_adj = jnp.maximum(qi - 1, 0)
    scores_into(0, j_adj)
    out = []
    for hd in range(2):
        s = s_ref[0, hd] + bias_ref[hd, 1]
        _, acc = update(state[hd], s, jnp.max(s, axis=0, keepdims=True), jnp.where(has_adj, 0.0, NEG),
                        v_t(j_adj, hd))
        out.append(acc[:MOBA_DH] / acc[MOBA_DH:MOBA_DH + 1])
    o_ref[...] = jnp.concatenate(out, axis=0).T.astype(o_ref.dtype)


def _t5_bucket(dist):
    n = jnp.maximum(dist, 0)
    max_exact = REL_BUCKETS // 2
    nf = jnp.maximum(n, 1).astype(F32)
    large = max_exact + (jnp.log(nf / max_exact) / math.log(REL_MAX_DIST / max_exact)
                         * (REL_BUCKETS - max_exact)).astype(jnp.int32)
    large = jnp.minimum(large, REL_BUCKETS - 1)
    return jnp.where(n < max_exact, n, large)


def _moba_bias_tables(rel_bias):
    tbl = rel_bias.astype(F32).T
    kpos = jnp.arange(MOBA_BLOCK)[:, None]
    qpos = jnp.arange(MOBA_BLOCK)[None, :]
    d_own = qpos - kpos
    own = jnp.where(d_own >= 0, tbl[:, _t5_bucket(d_own)], NEG)
    adj = tbl[:, _t5_bucket(d_own + MOBA_BLOCK)]
    return jnp.stack([own, adj], axis=1) * LOG2E, tbl[:, REL_BUCKETS - 1] * LOG2E


def _proj_even_kernel(x_ref, wb_ref, wf_ref, wvt_ref, hb_ref, hf_ref, vt_ref):
    xb = x_ref[...].astype(BF16)
    hb_ref[...] = jnp.dot(xb, wb_ref[...], preferred_element_type=F32).astype(hb_ref.dtype)
    hf_ref[...] = jnp.dot(xb, wf_ref[...], preferred_element_type=F32)
    for c in range(vt_ref.shape[0]):
        xc = xb[c * MOBA_BLOCK:(c + 1) * MOBA_BLOCK]
        vt_ref[c] = lax.dot_general(wvt_ref[...], xc, NT_DIMS, preferred_element_type=F32).astype(vt_ref.dtype)


def _proj_even(x, wb, wf, wvt):
    m, d = x.shape
    tm = min(ROW_TILE, m)
    nv = tm // MOBA_BLOCK
    full = lambda a: pl.BlockSpec(a.shape, lambda i: (0, 0))
    return pl.pallas_call(
        _proj_even_kernel,
        name="in_proj_even",
        grid=(m // tm,),
        in_specs=[pl.BlockSpec((tm, d), lambda i: (i, 0)), full(wb), full(wf), full(wvt)],
        out_specs=[pl.BlockSpec((tm, wb.shape[1]), lambda i: (i, 0)),
                   pl.BlockSpec((tm, wf.shape[1]), lambda i: (i, 0)),
                   pl.BlockSpec((nv, wvt.shape[0], MOBA_BLOCK), lambda i: (i, 0, 0))],
        out_shape=[jax.ShapeDtypeStruct((m, wb.shape[1]), BF16),
                   jax.ShapeDtypeStruct((m, wf.shape[1]), F32),
                   jax.ShapeDtypeStruct((m // MOBA_BLOCK, wvt.shape[0], MOBA_BLOCK), BF16)],
        compiler_params=_cparams(("parallel",)),
    )(x, wb, wf, wvt)


def _moba(hb, vt, rel_bias, batch, seq):
    m = hb.shape[0]
    nq = seq // MOBA_BLOCK
    nb_pad = LANES * (-(-nq // LANES))
    bias, c_far = _moba_bias_tables(rel_bias)
    hp_n = MOBA_HEADS // 2
    nb_gate = min(nb_pad, 16 * (-(-nq // 16)))
    kern = functools.partial(_moba_kernel, nb_pad=nb_pad, nb_gate=nb_gate)
    return pl.pallas_call(
        kern,
        name="moba",
        grid=(batch, hp_n, nq),
        in_specs=[pl.BlockSpec(memory_space=pltpu.SMEM),
                  pl.BlockSpec((MOBA_BLOCK, LANES), lambda b, h, i: (b * nq + i, h)),
                  pl.BlockSpec((seq, LANES), lambda b, h, i: (b, hp_n + h)),
                  pl.BlockSpec((nq, LANES, MOBA_BLOCK), lambda b, h, i: (b, h, 0)),
                  pl.BlockSpec((2, 2, MOBA_BLOCK, MOBA_BLOCK), lambda b, h, i: (h, 0, 0, 0))],
        out_specs=pl.BlockSpec((MOBA_BLOCK, LANES), lambda b, h, i: (b * nq + i, h)),
        out_shape=jax.ShapeDtypeStruct((m, MOBA_HEADS * MOBA_DH), BF16),
        scratch_shapes=[pltpu.VMEM((nb_pad, LANES), F32),
                        pltpu.VMEM((2, 2, MOBA_BLOCK, MOBA_BLOCK), F32),
                        pltpu.VMEM((2, 2, 1, MOBA_BLOCK), F32)],
        compiler_params=_cparams(("parallel", "parallel", "arbitrary")),
    )(c_far, hb, hb, vt, bias)


LEVELS = (32, 16, 8, 4, 2, 1)


def _chunk_cumsum(la):
    t = la.shape[0]
    row = lax.broadcasted_iota(jnp.int32, la.shape, 0) % CHUNK
    b = la
    sh = 1
    while sh < CHUNK:
        b = b + jnp.where(row >= sh, pltpu.roll(b, sh, 0), 0.0)
        sh *= 2
    return b


def _level_reference(b, h):
    t, w = b.shape
    if 2 * h >= 8:
        b3 = b.reshape(t // (2 * h), 2 * h, w)
        return jnp.broadcast_to(b3[:, h - 1:h, :], b3.shape).reshape(t, w)
    row = lax.broadcasted_iota(jnp.int32, b.shape, 0)
    if h == 2:
        r4 = row % 4
        return jnp.where(r4 == 0, pltpu.roll(b, t - 1, 0),
                         jnp.where(r4 == 1, b,
                                   jnp.where(r4 == 2, pltpu.roll(b, 1, 0), pltpu.roll(b, 2, 0))))
    return jnp.where(row % 2 == 1, pltpu.roll(b, 1, 0), b)


def _pair_masks():
    t = lax.broadcasted_iota(jnp.int32, (CHUNK, 2 * CHUNK), 0)
    s = lax.broadcasted_iota(jnp.int32, (CHUNK, 2 * CHUNK), 1) % CHUNK
    masks = []
    for h in LEVELS:
        masks.append((t // (2 * h) == s // (2 * h)) & ((t // h) % 2 == 1) & ((s // h) % 2 == 0))
    masks.append(t == s)
    return masks


def _stack_heads(x, split):
    lane = lax.broadcasted_iota(jnp.int32, x.shape, 1)
    z = jnp.zeros_like(x)
    return jnp.concatenate([jnp.where(lane < split, x, z), jnp.where(lane >= split, x, z)], axis=0)


def _gated_linear_tile(q, k, v, la, st_ref):
    t = q.shape[0]
    b = _chunk_cumsum(la)
    q_lv, k_lv = [], []
    for h in LEVELS:
        w = jnp.exp(-jnp.abs(b - _level_reference(b, h)))
        q_lv.append((q * w).astype(BF16))
        k_lv.append((k * w).astype(BF16))
    q_lv.append(q.astype(BF16))
    k_lv.append(k.astype(BF16))
    b3 = b.reshape(t // CHUNK, CHUNK, LANES)
    b_last = jnp.broadcast_to(b3[:, CHUNK - 1:CHUNK, :], b3.shape).reshape(t, LANES)
    q_in = (q * jnp.exp(b)).astype(BF16)
    k_out = (k * jnp.exp(b_last - b)).astype(BF16)
    masks = _pair_masks()
    e = lax.broadcasted_iota(jnp.int32, st_ref.shape, 0)
    d = lax.broadcasted_iota(jnp.int32, st_ref.shape, 1)
    same_head = (e < GLA_DV) == (d < GLA_DK)
    outs = []
    for c in range(t // CHUNK):
        rows = slice(c * CHUNK, (c + 1) * CHUNK)
        a = jnp.zeros((CHUNK, 2 * CHUNK), F32)
        for lv in range(len(masks)):
            p = lax.dot_general(q_lv[lv][rows], _stack_heads(k_lv[lv][rows], GLA_DK), NT_DIMS,
                                preferred_element_type=F32)
            a = a + jnp.where(masks[lv], p, 0.0)
        vc = v[rows]
        o = jnp.dot(a.astype(BF16), _stack_heads(vc, GLA_DV), preferred_element_type=F32)
        st = st_ref[...]
        o = o + lax.dot_general(q_in[rows], st.astype(BF16), NT_DIMS, preferred_element_type=F32)
        upd = lax.dot_general(vc, k_out[rows], TN_DIMS, preferred_element_type=F32)
        st_ref[...] = st * jnp.exp(b[(c + 1) * CHUNK - 1:(c + 1) * CHUNK, :]) + jnp.where(same_head, upd, 0.0)
        outs.append(o)
    return jnp.concatenate(outs, axis=0)


def _head_rms_gate(o, norm, gate):
    parts = []
    for hd in range(2):
        oh = o[:, hd * GLA_DV:(hd + 1) * GLA_DV]
        parts.append(oh * lax.rsqrt(jnp.mean(oh * oh, axis=-1, keepdims=True) + HEAD_NORM_EPS))
    return jnp.concatenate(parts, axis=1) * norm * _silu(gate)


def _log_sigmoid(x):
    return jnp.minimum(x, 0.0) - jnp.log1p(jnp.exp(-jnp.abs(x)))


def _gla_kernel(q_ref, k_ref, v_ref, ga_ref, gr_ref, wg_ref, bg_ref, norm_ref, o_ref, st_ref):
    @pl.when(pl.program_id(2) == 0)
    def _():
        st_ref[...] = jnp.zeros_like(st_ref)

    gate_in = jnp.dot(ga_ref[...].astype(BF16), wg_ref[...], preferred_element_type=F32) + bg_ref[...]
    la = _log_sigmoid(gate_in) / GLA_TAU
    q = q_ref[...].astype(F32) * (GLA_DK ** -0.5)
    o = _gated_linear_tile(q, k_ref[...].astype(F32), v_ref[...], la, st_ref)
    o_ref[...] = _head_rms_gate(o, norm_ref[...], gr_ref[...]).astype(o_ref.dtype)


def _gla(hb, hf, w_gate, b_gate, norm, batch, seq):
    m = hb.shape[0]
    tt = min(SEQ_TILE, seq)
    nt = seq // tt
    hp_n = GLA_HEADS // 2
    wg = jnp.zeros((LANES, GLA_HEADS * GLA_DK), BF16).at[:GLA_RANK].set(w_gate.astype(BF16))
    row = lambda b, h, i: b * nt + i
    return pl.pallas_call(
        _gla_kernel,
        name="gla",
        grid=(batch, hp_n, nt),
        in_specs=[pl.BlockSpec((tt, LANES), lambda b, h, i: (row(b, h, i), 8 + h)),
                  pl.BlockSpec((tt, LANES), lambda b, h, i: (row(b, h, i), 10 + h)),
                  pl.BlockSpec((tt, 2 * GLA_DV), lambda b, h, i: (row(b, h, i), 6 + h)),
                  pl.BlockSpec((tt, LANES), lambda b, h, i: (row(b, h, i), 4)),
                  pl.BlockSpec((tt, 2 * GLA_DV), lambda b, h, i: (row(b, h, i), h)),
                  pl.BlockSpec((LANES, LANES), lambda b, h, i: (0, h)),
                  pl.BlockSpec((1, LANES), lambda b, h, i: (0, h)),
                  pl.BlockSpec((1, 2 * GLA_DV), lambda b, h, i: (0, h))],
        out_specs=pl.BlockSpec((tt, 2 * GLA_DV), lambda b, h, i: (row(b, h, i), h)),
        out_shape=jax.ShapeDtypeStruct((m, GLA_HEADS * GLA_DV), BF16),
        scratch_shapes=[pltpu.VMEM((2 * GLA_DV, LANES), F32)],
        compiler_params=_cparams(("parallel", "parallel", "arbitrary")),
    )(hb, hb, hb, hf, hf, wg, b_gate.reshape(1, -1), norm.reshape(1, -1))


def _hgrn_kernel(q_ref, f_ref, v_ref, g_ref, lb_ref, norm_ref, o_ref, st_ref):
    @pl.when(pl.program_id(2) == 0)
    def _():
        st_ref[...] = jnp.zeros_like(st_ref)

    lb = lb_ref[...]
    f = lb + (1.0 - lb) * _sigmoid(f_ref[...])
    o = _gated_linear_tile(q_ref[...], 1.0 - f, v_ref[...].astype(BF16), jnp.log(f), st_ref)
    o_ref[...] = _head_rms_gate(o, norm_ref[...], g_ref[...]).astype(o_ref.dtype)


def _hgrn(h, lb, norm, batch, seq):
    m = h.shape[0]
    tt = min(SEQ_TILE, seq)
    nt = seq // tt
    row = lambda b, hh, i: b * nt + i
    return pl.pallas_call(
        _hgrn_kernel,
        name="hgrn",
        grid=(batch, HGRN_HEADS // 2, nt),
        in_specs=[pl.BlockSpec((tt, LANES), lambda b, hh, i: (row(b, hh, i), hh)),
                  pl.BlockSpec((tt, LANES), lambda b, hh, i: (row(b, hh, i), 2 + hh)),
                  pl.BlockSpec((tt, 2 * GLA_DV), lambda b, hh, i: (row(b, hh, i), 2 + hh)),
                  pl.BlockSpec((tt, 2 * GLA_DV), lambda b, hh, i: (row(b, hh, i), 4 + hh)),
                  pl.BlockSpec((1, LANES), lambda b, hh, i: (0, hh)),
                  pl.BlockSpec((1, 2 * GLA_DV), lambda b, hh, i: (0, hh))],
        out_specs=pl.BlockSpec((tt, 2 * GLA_DV), lambda b, hh, i: (row(b, hh, i), hh)),
        out_shape=jax.ShapeDtypeStruct((m, HGRN_HEADS * GLA_DV), BF16),
        scratch_shapes=[pltpu.VMEM((2 * GLA_DV, LANES), F32)],
        compiler_params=_cparams(("parallel", "parallel", "arbitrary")),
    )(h, h, h, h, lb.reshape(1, -1), norm.reshape(1, -1))


def _rotate(x, cos, sin_signed):
    lane = lax.broadcasted_iota(jnp.int32, x.shape, 1)
    half = GLA_DK // 2
    swapped = jnp.where(lane % GLA_DK < half, pltpu.roll(x, LANES - half, 1), pltpu.roll(x, half, 1))
    return x * cos + swapped * sin_signed


def _ret_kernel(q_ref, k_ref, v_ref, g_ref, cos_ref, sin_ref, dmask_ref, zeta_ref, xi_ref, gch_ref,
                norm_ref, o_ref, st_ref):
    @pl.when(pl.program_id(2) == 0)
    def _():
        st_ref[...] = jnp.zeros_like(st_ref)

    t = q_ref.shape[0]
    cos, sin = cos_ref[...], sin_ref[...]
    q = _rotate(q_ref[...], cos, sin)
    k = _rotate(k_ref[...], cos, sin) * (GLA_DK ** -0.5)
    v = v_ref[...].astype(BF16)
    nc = t // CHUNK
    zeta = jnp.broadcast_to(zeta_ref[0][None], (nc, CHUNK, LANES)).reshape(t, LANES)
    xi = jnp.broadcast_to(xi_ref[0][None], (nc, CHUNK, LANES)).reshape(t, LANES)
    q_b, k_b = q.astype(BF16), k.astype(BF16)
    q_in = (q * xi).astype(BF16)
    k_out = (k * zeta).astype(BF16)
    dmask = dmask_ref[0]
    g_chunk = gch_ref[0]
    e = lax.broadcasted_iota(jnp.int32, st_ref.shape, 0)
    d = lax.broadcasted_iota(jnp.int32, st_ref.shape, 1)
    same_head = (e < GLA_DV) == (d < GLA_DK)
    outs = []
    for c in range(nc):
        rows = slice(c * CHUNK, (c + 1) * CHUNK)
        a = lax.dot_general(q_b[rows], _stack_heads(k_b[rows], GLA_DK), NT_DIMS,
                            preferred_element_type=F32) * dmask
        vc = v[rows]
        o = jnp.dot(a.astype(BF16), _stack_heads(vc, GLA_DV), preferred_element_type=F32)
        st = st_ref[...]
        o = o + lax.dot_general(q_in[rows], st.astype(BF16), NT_DIMS, preferred_element_type=F32)
        upd = lax.dot_general(vc, k_out[rows], TN_DIMS, preferred_element_type=F32)
        st_ref[...] = st * g_chunk + jnp.where(same_head, upd, 0.0)
        outs.append(o)
    o = jnp.concatenate(outs, axis=0)
    parts = []
    for hd in range(2):
        oh = o[:, hd * GLA_DV:(hd + 1) * GLA_DV]
        oh = oh - jnp.mean(oh, axis=-1, keepdims=True)
        parts.append(oh * lax.rsqrt(jnp.mean(oh * oh, axis=-1, keepdims=True) + HEAD_NORM_EPS))
    o_ref[...] = (jnp.concatenate(parts, axis=1) * norm_ref[...] * _silu(g_ref[...])).astype(o_ref.dtype)


def _retention_tables(seq):
    half = GLA_DK // 2
    inv = ROPE_BASE ** (-jnp.arange(half, dtype=F32) / half)
    ang = jnp.arange(seq, dtype=F32)[:, None] * inv[None, :]
    cos, sin = jnp.cos(ang), jnp.sin(ang)
    cos_t = jnp.tile(cos, (1, 4))
    sin_t = jnp.tile(jnp.concatenate([-sin, sin], axis=1), (1, 2))
    log_g = jnp.log1p(-jnp.exp2(-5.0 - jnp.arange(RET_HEADS, dtype=F32)))
    pos = jnp.arange(CHUNK, dtype=F32)
    rel = pos[:, None] - pos[None, :]
    dmask = jnp.where(rel >= 0, jnp.exp(jnp.maximum(rel, 0.0)[None] * log_g[:, None, None]), 0.0)
    zeta = jnp.exp((CHUNK - 1 - pos)[None, :] * log_g[:, None])
    xi = jnp.exp((pos + 1.0)[None, :] * log_g[:, None])
    g_chunk = jnp.exp(CHUNK * log_g)
    hp_n = RET_HEADS // 2
    pair = lambda x: x.reshape(hp_n, 2, *x.shape[1:])
    dmask_p = jnp.concatenate([pair(dmask)[:, 0], pair(dmask)[:, 1]], axis=-1)
    lanes = lambda x: jnp.repeat(pair(x), GLA_DK, axis=1)
    zeta_p = jnp.swapaxes(lanes(zeta), 1, 2)
    xi_p = jnp.swapaxes(lanes(xi), 1, 2)
    gch_p = lanes(g_chunk)[:, None, :]
    return cos_t, sin_t, dmask_p, zeta_p, xi_p, gch_p


def _retention(h, norm, batch, seq):
    m = h.shape[0]
    tt = min(SEQ_TILE, seq)
    nt = seq // tt
    cos_t, sin_t, dmask, zeta, xi, gch = _retention_tables(seq)
    row = lambda b, hh, i: b * nt + i
    const3 = lambda shape: pl.BlockSpec((1,) + shape, lambda b, hh, i: (hh, 0, 0))
    return pl.pallas_call(
        _ret_kernel,
        name="retention",
        grid=(batch, RET_HEADS // 2, nt),
        in_specs=[pl.BlockSpec((tt, LANES), lambda b, hh, i: (row(b, hh, i), 12 + hh)),
                  pl.BlockSpec((tt, LANES), lambda b, hh, i: (row(b, hh, i), 14 + hh)),
                  pl.BlockSpec((tt, 2 * GLA_DV), lambda b, hh, i: (row(b, hh, i), 8 + hh)),
                  pl.BlockSpec((tt, 2 * GLA_DV), lambda b, hh, i: (row(b, hh, i), 10 + hh)),
                  pl.BlockSpec((tt, LANES), lambda b, hh, i: (i, 0)),
                  pl.BlockSpec((tt, LANES), lambda b, hh, i: (i, 0)),
                  const3((CHUNK, 2 * CHUNK)), const3((CHUNK, LANES)), const3((CHUNK, LANES)),
                  const3((1, LANES)),
                  pl.BlockSpec((1, 2 * GLA_DV), lambda b, hh, i: (0, hh))],
        out_specs=pl.BlockSpec((tt, 2 * GLA_DV), lambda b, hh, i: (row(b, hh, i), hh)),
        out_shape=jax.ShapeDtypeStruct((m, RET_HEADS * GLA_DV), BF16),
        scratch_shapes=[pltpu.VMEM((2 * GLA_DV, LANES), F32)],
        compiler_params=_cparams(("parallel", "parallel", "arbitrary")),
    )(h, h, h, h, cos_t, sin_t, dmask, zeta, xi, gch, norm.reshape(1, -1))


def _route(xn, w, b):
    logits = jnp.dot(xn.astype(BF16), w, preferred_element_type=F32) + b
    lane = lax.broadcasted_iota(jnp.int32, logits.shape, 1)
    lane_f = lane.astype(F32)
    logits = jnp.where(lane < N_EXPERTS, logits, -jnp.inf)
    m1 = jnp.max(logits, axis=1, keepdims=True)
    i1 = jnp.min(jnp.where(logits == m1, lane_f, float(LANES)), axis=1, keepdims=True)
    rest = jnp.where(lane_f == i1, -jnp.inf, logits)
    m2 = jnp.max(rest, axis=1, keepdims=True)
    i2 = jnp.min(jnp.where(rest == m2, lane_f, float(LANES)), axis=1, keepdims=True)
    z = jnp.exp(m2 - m1)
    g1 = 1.0 / (1.0 + z)
    experts = jnp.where(lane == 0, i1, jnp.where(lane == 1, i2, 0.0)).astype(jnp.int32)
    gates = jnp.where(lane == 0, g1, jnp.where(lane == 1, z * g1, 0.0))
    return experts, gates


def _outproj_kernel(oa_ref, ob_ref, x_ref, wa_ref, wb_ref, g_ref, b_ref, *rest, alpha):
    y = jnp.dot(oa_ref[...], wa_ref[...], preferred_element_type=F32)
    y = y + jnp.dot(ob_ref[...], wb_ref[...], preferred_element_type=F32)
    xn = _layer_norm(alpha * x_ref[...] + y, g_ref[...], b_ref[...])
    if len(rest) == 1:
        rest[0][...] = xn
    else:
        wr_ref, br_ref, o_ref, e_ref, gt_ref = rest
        o_ref[...] = xn
        e_ref[...], gt_ref[...] = _route(xn, wr_ref[...], br_ref[...])


def _outproj_ln(oa, ob, x, w_out, g, b, alpha, router=None):
    m, d = x.shape
    tm = min(ROW_TILE, m)
    wa = w_out[:oa.shape[1]].astype(BF16)
    wb = w_out[oa.shape[1]:].astype(BF16)
    full = lambda a: pl.BlockSpec(a.shape, lambda i: (0, 0))
    rows = lambda a: pl.BlockSpec((tm, a.shape[1]), lambda i: (i, 0))
    args = [oa, ob, x, wa, wb, g.reshape(1, -1), b.reshape(1, -1)]
    in_specs = [rows(oa), rows(ob), rows(x)] + [full(a) for a in args[3:]]
    out_specs = [rows(x)]
    out_shape = [jax.ShapeDtypeStruct((m, d), F32)]
    if router is not None:
        wr = jnp.zeros((d, LANES), F32).at[:, :N_EXPERTS].set(router[0]).astype(BF16)
        br = jnp.zeros((1, LANES), F32).at[0, :N_EXPERTS].set(router[1])
        args += [wr, br]
        in_specs += [full(wr), full(br)]
        lanes = pl.BlockSpec((tm, LANES), lambda i: (i, 0))
        out_specs += [lanes, lanes]
        out_shape += [jax.ShapeDtypeStruct((m, LANES), jnp.int32), jax.ShapeDtypeStruct((m, LANES), F32)]
    out = pl.pallas_call(
        functools.partial(_outproj_kernel, alpha=alpha),
        name="out_proj_ln",
        grid=(m // tm,),
        in_specs=in_specs,
        out_specs=out_specs,
        out_shape=out_shape,
        compiler_params=_cparams(("parallel",)),
    )(*args)
    if router is None:
        return out[0]
    return out[0], out[1][:, :TOP_K], out[2][:, :TOP_K]


def _ln_ple(x, ffn, p, g, b, wg, wp, alpha):
    y = _layer_norm(alpha * x + ffn, g, b)
    gate = _sigmoid(jnp.dot(y.astype(BF16), wg, preferred_element_type=F32))
    return y + gate * jnp.dot(p.astype(BF16), wp, preferred_element_type=F32)


def _swiglu_chunks(xb_ref, w1, w3, w2):
    n = xb_ref.shape[0] // SWIGLU_ROWS
    h = []
    for c in range(n):
        xc = xb_ref[c * SWIGLU_ROWS:(c + 1) * SWIGLU_ROWS, :]
        h.append((jnp.dot(xc, w1, preferred_element_type=F32), jnp.dot(xc, w3, preferred_element_type=F32)))
    return [jnp.dot((_silu(h1) * h3).astype(BF16), w2, preferred_element_type=F32) for h1, h3 in h]


def _accumulate_steps(parts, acc_ref, finish, n_steps):
    assert n_steps > 1
    f = pl.program_id(1)
    last = n_steps - 1
    chunk = lambda c: slice(c * SWIGLU_ROWS, (c + 1) * SWIGLU_ROWS)

    @pl.when(f == 0)
    def _():
        for c, part in enumerate(parts):
            acc_ref[chunk(c), :] = part

    @pl.when((f > 0) & (f < last))
    def _():
        for c, part in enumerate(parts):
            acc_ref[chunk(c), :] += part

    @pl.when(f == last)
    def _():
        finish(jnp.concatenate([acc_ref[chunk(c), :] + part for c, part in enumerate(parts)], axis=0))


def _swiglu_kernel(x_ref, p_ref, w1_ref, w3_ref, w2_ref, g_ref, b_ref, wg_ref, wp_ref, o_ref,
                   xb_ref, acc_ref, *, alpha, n_steps):
    @pl.when(pl.program_id(1) == 0)
    def _():
        xb_ref[...] = x_ref[...].astype(BF16)

    def finish(ffn):
        o_ref[...] = _ln_ple(x_ref[...], ffn, p_ref[...], g_ref[...], b_ref[...], wg_ref[...], wp_ref[...],
                             alpha)

    _accumulate_steps(_swiglu_chunks(xb_ref, w1_ref[...], w3_ref[...], w2_ref[...]), acc_ref, finish,
                      n_steps)


def _swiglu_ln_ple(x, p, w1, w3, w2, g, b, wg, wp, alpha):
    m, d = x.shape
    tm = min(ROW_TILE, m)
    pad = FFN_F_PAD - w1.shape[1]
    w1p = jnp.pad(w1.astype(BF16), ((0, 0), (0, pad)))
    w3p = jnp.pad(w3.astype(BF16), ((0, 0), (0, pad)))
    w2p = jnp.pad(w2.astype(BF16), ((0, pad), (0, 0)))
    nf = FFN_F_PAD // FFN_F_TILE
    g2, b2 = g.reshape(1, -1), b.reshape(1, -1)
    wgb, wpb = wg.astype(BF16), wp.astype(BF16)
    full = lambda a: pl.BlockSpec(a.shape, lambda i, f: (0, 0))
    rows = lambda a: pl.BlockSpec((tm, a.shape[1]), lambda i, f: (i, 0))
    return pl.pallas_call(
        functools.partial(_swiglu_kernel, alpha=alpha, n_steps=nf),
        name="swiglu_ln_ple",
        grid=(m // tm, nf),
        in_specs=[rows(x), rows(p),
                  pl.BlockSpec((d, FFN_F_TILE), lambda i, f: (0, f)),
                  pl.BlockSpec((d, FFN_F_TILE), lambda i, f: (0, f)),
                  pl.BlockSpec((FFN_F_TILE, d), lambda i, f: (f, 0)),
                  full(g2), full(b2), full(wgb), full(wpb)],
        out_specs=rows(x),
        out_shape=jax.ShapeDtypeStruct((m, d), F32),
        scratch_shapes=[pltpu.VMEM((tm, d), BF16), pltpu.VMEM((tm, d), F32)],
        compiler_params=_cparams(("parallel", "arbitrary")),
    )(x, p, w1p, w3p, w2p, g2, b2, wgb, wpb)


def _cast_kernel(w_ref, o_ref):
    o_ref[...] = w_ref[...].astype(o_ref.dtype)


def _cast_bf16(w):
    e, r, c = w.shape
    spec = pl.BlockSpec((1, r // 2, c), lambda i, j: (i, j, 0))
    return pl.pallas_call(
        _cast_kernel,
        name="cast_bf16",
        grid=(e, 2),
        in_specs=[spec],
        out_specs=spec,
        out_shape=jax.ShapeDtypeStruct(w.shape, BF16),
        compiler_params=_cparams(("parallel", "parallel")),
    )(w)


def _experts_kernel(be_ref, x_ref, w1_ref, w3_ref, w2_ref, o_ref, acc_ref, *, n_steps):
    def finish(y):
        o_ref[...] = y.astype(o_ref.dtype)

    _accumulate_steps(_swiglu_chunks(x_ref, w1_ref[0], w3_ref[0], w2_ref[0]), acc_ref, finish, n_steps)


def _experts(xg, blk_e, w1, w3, w2):
    r, d = xg.shape
    fe = w1.shape[2]
    nf = fe // MOE_F_TILE
    grid_spec = pltpu.PrefetchScalarGridSpec(
        num_scalar_prefetch=1,
        grid=(r // MOE_ROW_TILE, nf),
        in_specs=[pl.BlockSpec((MOE_ROW_TILE, d), lambda i, f, be: (i, 0)),
                  pl.BlockSpec((1, d, MOE_F_TILE), lambda i, f, be: (be[i], 0, f)),
                  pl.BlockSpec((1, d, MOE_F_TILE), lambda i, f, be: (be[i], 0, f)),
                  pl.BlockSpec((1, MOE_F_TILE, d), lambda i, f, be: (be[i], f, 0))],
        out_specs=pl.BlockSpec((MOE_ROW_TILE, d), lambda i, f, be: (i, 0)),
        scratch_shapes=[pltpu.VMEM((MOE_ROW_TILE, d), F32)],
    )
    return pl.pallas_call(
        functools.partial(_experts_kernel, n_steps=nf),
        name="experts",
        grid_spec=grid_spec,
        out_shape=jax.ShapeDtypeStruct((r, d), BF16),
        compiler_params=_cparams(("parallel", "arbitrary")),
    )(blk_e, xg, w1, w3, w2)


def _combine_kernel(x_ref, y0_ref, y1_ref, gt_ref, p_ref, g_ref, b_ref, wg_ref, wp_ref, o_ref, *, alpha):
    gt = gt_ref[...]
    ffn = y0_ref[...].astype(F32) * gt[:, 0:1] + y1_ref[...].astype(F32) * gt[:, 1:2]
    o_ref[...] = _ln_ple(x_ref[...], ffn, p_ref[...], g_ref[...], b_ref[...], wg_ref[...], wp_ref[...],
                         alpha)


def _combine_ln_ple(x, y, gates, p, g, b, wg, wp, alpha):
    m, d = x.shape
    tm = min(ROW_TILE, m)
    g2, b2 = g.reshape(1, -1), b.reshape(1, -1)
    wgb, wpb = wg.astype(BF16), wp.astype(BF16)
    gt = jnp.zeros((m, LANES), F32).at[:, :TOP_K].set(gates)
    full = lambda a: pl.BlockSpec(a.shape, lambda i: (0, 0))
    rows = lambda a: pl.BlockSpec((tm, a.shape[1]), lambda i: (i, 0))
    return pl.pallas_call(
        functools.partial(_combine_kernel, alpha=alpha),
        name="combine_ln_ple",
        grid=(m // tm,),
        in_specs=[rows(x), rows(y), pl.BlockSpec((tm, d), lambda i: (i + m // tm, 0)), rows(gt), rows(p),
                  full(g2), full(b2), full(wgb), full(wpb)],
        out_specs=rows(x),
        out_shape=jax.ShapeDtypeStruct((m, d), F32),
        compiler_params=_cparams(("parallel",)),
    )(x, y, y, gt, p, g2, b2, wgb, wpb)


def _moe_ln_ple(x, top_e, gates, p, w1, w3, w2, g, b, wg, wp, alpha):
    m, d = x.shape
    n_asg = m * TOP_K
    e_flat = top_e.reshape(-1)
    onehot = (jnp.arange(N_EXPERTS, dtype=jnp.int32)[:, None] == e_flat[None, :]).astype(jnp.int32)
    csum = jnp.cumsum(onehot, axis=1)
    rank = jnp.sum(csum * onehot, axis=0) - 1
    counts = csum[:, -1]
    padded = (counts + MOE_ROW_TILE - 1) // MOE_ROW_TILE * MOE_ROW_TILE
    p_ends = jnp.cumsum(padded)
    p_starts = p_ends - padded
    pos = (jnp.sum(onehot * p_starts[:, None], axis=0) + rank).reshape(m, TOP_K)
    n_rows = (n_asg // MOE_ROW_TILE + N_EXPERTS) * MOE_ROW_TILE
    blk_start = jnp.arange(n_rows // MOE_ROW_TILE, dtype=jnp.int32) * MOE_ROW_TILE
    blk_e = jnp.minimum(jnp.sum((blk_start[None, :] >= p_ends[:, None]).astype(jnp.int32), axis=0),
                        N_EXPERTS - 1)
    n_pad = n_rows - n_asg
    pad_i = jnp.arange(n_pad, dtype=jnp.int32)
    pad_e = jnp.sum((pad_i[None, :] >= jnp.cumsum(padded - counts)[:, None]).astype(jnp.int32), axis=0)
    asg_i = jnp.arange(n_asg, dtype=jnp.int32)
    keys = jnp.concatenate([e_flat * (2 * n_rows) + asg_i, pad_e * (2 * n_rows) + n_rows + pad_i])
    toks = jnp.concatenate([asg_i // TOP_K, jnp.zeros((n_pad,), jnp.int32)])
    _, tok_buf = lax.sort((keys, toks), num_keys=1)
    xg = x.astype(BF16)[tok_buf]
    y = _experts(xg, blk_e, _cast_bf16(w1), _cast_bf16(w3), _cast_bf16(w2))
    y_tok = y[pos.T.reshape(-1)]
    return _combine_ln_ple(x, y_tok, gates, p, g, b, wg, wp, alpha)


def kernel(x, p, rel_bias, even_w_in, gla_w_gate, gla_b_gate, gla_norm, even_w_out, odd_w_in, hgrn_gamma,
           hgrn_norm, ret_norm, odd_w_out, ln_mix_g, ln_mix_b, ln_ffn_g, ln_ffn_b, ffn_w1, ffn_w3, ffn_w2,
           router_w, router_b, expert_w1, expert_w3, expert_w2, ple_w_gate, ple_w_proj):
    batch, seq, d = x.shape
    depth = p.shape[0]
    m = batch * seq
    alpha = (2.0 * depth) ** 0.25
    lb_p = jax.nn.softmax(hgrn_gamma.astype(F32), axis=0)
    lb_all = jnp.cumsum(lb_p, axis=0) - lb_p[0]
    xf = x.reshape(m, d)
    pf = p.reshape(depth, m, PLE_DIM)
    n_mq = 3 * MOBA_HEADS * MOBA_DH + 2 * GLA_HEADS * GLA_DK + GLA_HEADS * GLA_DV
    for i in range(depth):
        j = i // 2
        if i % 2 == 0:
            w_in = even_w_in[j]
            n_qk = 2 * MOBA_HEADS * MOBA_DH
            n_v = MOBA_HEADS * MOBA_DH
            w_q = w_in[:, :n_v] * (MOBA_DH ** -0.5 * LOG2E)
            w_b = jnp.concatenate([w_q, w_in[:, n_v:n_qk], w_in[:, n_qk + n_v:n_mq]], axis=1).astype(BF16)
            w_vt = w_in[:, n_qk:n_qk + n_v].T.astype(BF16)
            w_f = jnp.concatenate([w_in[:, n_mq + GLA_RANK:], w_in[:, n_mq:n_mq + GLA_RANK],
                                   jnp.zeros((d, LANES - GLA_RANK), F32)], axis=1).astype(BF16)
            hb, hf, vt = _proj_even(xf, w_b, w_f, w_vt)
            o_a = _moba(hb, vt, rel_bias, batch, seq)
            o_b = _gla(hb, hf, gla_w_gate[j], gla_b_gate[j], gla_norm[j], batch, seq)
            xf = _outproj_ln(o_a, o_b, xf, even_w_out[j], ln_mix_g[i], ln_mix_b[i], alpha)
            xf = _swiglu_ln_ple(xf, pf[i], ffn_w1[j], ffn_w3[j], ffn_w2[j], ln_ffn_g[i], ln_ffn_b[i],
                                ple_w_gate[i], ple_w_proj[i], alpha)
        else:
            h = _proj(xf, odd_w_in[j].astype(BF16), F32)
            o_c = _hgrn(h, lb_all[i], hgrn_norm[j], batch, seq)
            o_d = _retention(h, ret_norm[j], batch, seq)
            xf, top_e, gates = _outproj_ln(o_c, o_d, xf, odd_w_out[j], ln_mix_g[i], ln_mix_b[i], alpha,
                                           router=(router_w[j], router_b[j]))
            xf = _moe_ln_ple(xf, top_e, gates, pf[i], expert_w1[j], expert_w3[j], expert_w2[j],
                             ln_ffn_g[i], ln_ffn_b[i], ple_w_gate[i], ple_w_proj[i], alpha)
    return xf.reshape(batch, seq, d)
```

```python
import functools
import math

import numpy as np
import jax
import jax.numpy as jnp
from jax import lax
from jax.experimental import pallas as pl
from jax.experimental.pallas import tpu as pltpu

D_MODEL = 1024
PLE_DIM = 256
LN_EPS = 1e-5
HEAD_NORM_EPS = 1e-6
MOBA_HEADS, MOBA_DH, MOBA_BLOCK, MOBA_TOPK = 8, 64, 256, 3
REL_BUCKETS, REL_MAX_DIST = 32, 128
GLA_HEADS, GLA_DK, GLA_DV, GLA_RANK, GLA_TAU = 4, 64, 128, 16, 16.0
HGRN_HEADS = 4
RET_HEADS = 4
ROPE_BASE = 10000.0
CHUNK = 64
FFN_DENSE = 2752
N_EXPERTS, TOP_K, FFN_EXPERT = 8, 2, 3584

LANES = 128
VMEM_LIMIT = 56 * 1024 * 1024
ROW_TILE = 512
SEQ_TILE = 512
MOE_ROW_TILE = 512
MOE_F_TILE = 1792
FFN_F_PAD = 2816
FFN_F_TILE = 1408
SWIGLU_ROWS = 256
NEG = -1e30
LOG2E = math.log2(math.e)
SCORE_SLOTS = 3
FAR_UNROLL = 2 * SCORE_SLOTS
DENOM_ROWS = 16

BF16 = jnp.bfloat16
F32 = jnp.float32
NT_DIMS = (((1,), (1,)), ((), ()))
TN_DIMS = (((0,), (0,)), ((), ()))


def _cparams(sem):
    return pltpu.CompilerParams(dimension_semantics=sem, vmem_limit_bytes=VMEM_LIMIT)


def _sigmoid(x):
    return 1.0 / (1.0 + jnp.exp(-x))


def _silu(x):
    return x * _sigmoid(x)


def _layer_norm(y, g, b):
    yc = y - jnp.mean(y, axis=-1, keepdims=True)
    return yc * lax.rsqrt(jnp.mean(yc * yc, axis=-1, keepdims=True) + LN_EPS) * g + b


def _proj_kernel(x_ref, w_ref, o_ref):
    o_ref[...] = jnp.dot(x_ref[...].astype(BF16), w_ref[...],
                         preferred_element_type=F32).astype(o_ref.dtype)


def _proj(x, w, out_dtype):
    m, d = x.shape
    n = w.shape[1]
    tm = min(ROW_TILE, m)
    return pl.pallas_call(
        _proj_kernel,
        name="in_proj",
        grid=(m // tm,),
        in_specs=[pl.BlockSpec((tm, d), lambda i: (i, 0)),
                  pl.BlockSpec((d, n), lambda i: (0, 0))],
        out_specs=pl.BlockSpec((tm, n), lambda i: (i, 0)),
        out_shape=jax.ShapeDtypeStruct((m, n), out_dtype),
        compiler_params=_cparams(("parallel",)),
    )(x, w)


def _moba_kernel(c_far_ref, q_ref, k_ref, vt_ref, bias_ref, o_ref, kmean_ref, s_ref, smax_ref, *,
                 nb_pad, nb_gate):
    hp = pl.program_id(1)
    qi = pl.program_id(2)
    tq = MOBA_BLOCK

    @pl.when(qi == 0)
    def _():
        kmean_ref[...] = jnp.zeros_like(kmean_ref)

    k_own = k_ref[pl.ds(pl.multiple_of(qi * tq, tq), tq), :]
    q_tb = q_ref[...].astype(F32).T.astype(BF16)
    kmean = kmean_ref[0:nb_gate, :]
    kmean_ref[pl.ds(qi, 1), :] = jnp.mean(k_own.astype(F32), axis=0, keepdims=True)

    km_hi = kmean.astype(BF16)
    km_lo = (kmean - km_hi.astype(F32)).astype(BF16)
    kdim = lax.broadcasted_iota(jnp.int32, (nb_gate, LANES), 1)
    zero_k = jnp.zeros_like(km_hi)
    gate_lhs = jnp.concatenate(
        [jnp.concatenate([jnp.where((kdim < MOBA_DH) == (hd == 0), part, zero_k) for part in (km_hi, km_lo)],
                         axis=1) for hd in range(2)], axis=0)
    gate = jnp.dot(gate_lhs, jnp.concatenate([q_tb, q_tb], axis=0), preferred_element_type=F32)

    dim = lax.broadcasted_iota(jnp.int32, (LANES, tq), 0)
    blk_t = lax.broadcasted_iota(jnp.int32, (nb_gate, tq), 0)
    blk_tf = blk_t.astype(F32)
    blk = lax.broadcasted_iota(jnp.int32, (tq, nb_pad), 1)
    qs, qa = [], []
    for hd in range(2):
        g = jnp.where(blk_t < qi, gate[hd * nb_gate:(hd + 1) * nb_gate], -jnp.inf)
        sel = jnp.zeros((nb_gate, tq), jnp.bool_)
        for _ in range(MOBA_TOPK):
            gmax = jnp.max(g, axis=0, keepdims=True)
            idx = jnp.min(jnp.where(g == gmax, blk_tf, float(nb_pad)), axis=0, keepdims=True)
            pick = (blk_tf == idx) & (gmax > -jnp.inf)
            sel = sel | pick
            g = jnp.where(pick, -jnp.inf, g)
        qh_t = jnp.where((dim < MOBA_DH) == (hd == 0), q_tb, jnp.zeros_like(q_tb))
        rows = [qh_t, jnp.where(sel, 0.0, NEG).astype(BF16)]
        if nb_pad > nb_gate:
            rows.append(jnp.zeros((nb_pad - nb_gate, tq), BF16))
        qs.append(qh_t)
        qa.append(jnp.concatenate(rows, axis=0))

    ones_rows = jnp.ones((DENOM_ROWS, tq), BF16)

    def v_t(j, hd):
        return jnp.concatenate([vt_ref[j, hd * MOBA_DH:(hd + 1) * MOBA_DH, :], ones_rows], axis=0)

    def scores_into(slot, j):
        onehot = jnp.where(blk == j, 1.0, 0.0).astype(BF16)
        ka = jnp.concatenate([k_ref[pl.ds(pl.multiple_of(j * tq, tq), tq), :], onehot], axis=1)
        for hd in range(2):
            s = jnp.dot(ka, qa[hd], preferred_element_type=F32)
            s_ref[slot, hd] = s
            smax_ref[slot, hd] = jnp.max(s, axis=0, keepdims=True)

    def update(state, s, smax, shift, vt):
        m, acc = state
        m_new = jnp.maximum(m, smax + shift)
        p = jnp.exp2(s - (m_new - shift))
        acc = jnp.exp2(m - m_new) * acc + jnp.dot(vt, p.astype(BF16), preferred_element_type=F32)
        return m_new, acc

    s_own = [jnp.dot(k_own, qs[hd], preferred_element_type=F32) + bias_ref[hd, 0] for hd in range(2)]
    last = pl.num_programs(2) - 1
    for ahead in range(SCORE_SLOTS - 1):
        scores_into(ahead, jnp.minimum(ahead, last))
    state = []
    for hd in range(2):
        m0 = jnp.max(s_own[hd], axis=0, keepdims=True)
        p = jnp.exp2(s_own[hd] - m0)
        state.append((m0, jnp.dot(v_t(qi, hd), p.astype(BF16), preferred_element_type=F32)))

    n_far = qi - 1

    def far_group(jj, st):
        a = FAR_UNROLL * jj
        for i in range(FAR_UNROLL):
            ahead = i + SCORE_SLOTS - 1
            scores_into(ahead % SCORE_SLOTS, jnp.minimum(a + ahead, last))
            shift = [jnp.where(a + i < n_far, c_far_ref[2 * hp + hd], NEG) for hd in range(2)]
            st = tuple(update(st[hd], s_ref[i % SCORE_SLOTS, hd], smax_ref[i % SCORE_SLOTS, hd], shift[hd],
                              v_t(jnp.minimum(a + i, last), hd)) for hd in range(2))
        return st

    state = lax.fori_loop(0, (n_far + FAR_UNROLL - 1) // FAR_UNROLL, far_group, tuple(state))
    has_adj = qi >= 1
    j_adj = jnp.maximum(qi - 1, 0)
    scores_into(0, j_adj)
    out = []
    for hd in range(2):
        s = s_ref[0, hd] + bias_ref[hd, 1]
        _, acc = update(state[hd], s, jnp.max(s, axis=0, keepdims=True), jnp.where(has_adj, 0.0, NEG),
                        v_t(j_adj, hd))
        out.append(acc[:MOBA_DH] / acc[MOBA_DH:MOBA_DH + 1])
    o_ref[...] = jnp.concatenate(out, axis=0).T.astype(o_ref.dtype)


def _t5_bucket(dist):
    n = jnp.maximum(dist, 0)
    max_exact = REL_BUCKETS // 2
    nf = jnp.maximum(n, 1).astype(F32)
    large = max_exact + (jnp.log(nf / max_exact) / math.log(REL_MAX_DIST / max_exact)
                         * (REL_BUCKETS - max_exact)).astype(jnp.int32)
    large = jnp.minimum(large, REL_BUCKETS - 1)
    return jnp.where(n < max_exact, n, large)


def _moba_bias_tables(rel_bias):
    tbl = rel_bias.astype(F32).T
    kpos = jnp.arange(MOBA_BLOCK)[:, None]
    qpos = jnp.arange(MOBA_BLOCK)[None, :]
    d_own = qpos - kpos
    own = jnp.where(d_own >= 0, tbl[:, _t5_bucket(d_own)], NEG)
    adj = tbl[:, _t5_bucket(d_own + MOBA_BLOCK)]
    return jnp.stack([own, adj], axis=1) * LOG2E, tbl[:, REL_BUCKETS - 1] * LOG2E


def _proj_even_kernel(x_ref, wb_ref, wf_ref, wvt_ref, hb_ref, hf_ref, vt_ref):
    xb = x_ref[...].astype(BF16)
    hb_ref[...] = jnp.dot(xb, wb_ref[...], preferred_element_type=F32).astype(hb_ref.dtype)
    hf_ref[...] = jnp.dot(xb, wf_ref[...], preferred_element_type=F32)
    for c in range(vt_ref.shape[0]):
        xc = xb[c * MOBA_BLOCK:(c + 1) * MOBA_BLOCK]
        vt_ref[c] = lax.dot_general(wvt_ref[...], xc, NT_DIMS, preferred_element_type=F32).astype(vt_ref.dtype)


def _proj_even(x, wb, wf, wvt):
    m, d = x.shape
    tm = min(ROW_TILE, m)
    nv = tm // MOBA_BLOCK
    full = lambda a: pl.BlockSpec(a.shape, lambda i: (0, 0))
    return pl.pallas_call(
        _proj_even_kernel,
        name="in_proj_even",
        grid=(m // tm,),
        in_specs=[pl.BlockSpec((tm, d), lambda i: (i, 0)), full(wb), full(wf), full(wvt)],
        out_specs=[pl.BlockSpec((tm, wb.shape[1]), lambda i: (i, 0)),
                   pl.BlockSpec((tm, wf.shape[1]), lambda i: (i, 0)),
                   pl.BlockSpec((nv, wvt.shape[0], MOBA_BLOCK), lambda i: (i, 0, 0))],
        out_shape=[jax.ShapeDtypeStruct((m, wb.shape[1]), BF16),
                   jax.ShapeDtypeStruct((m, wf.shape[1]), F32),
                   jax.ShapeDtypeStruct((m // MOBA_BLOCK, wvt.shape[0], MOBA_BLOCK), BF16)],
        compiler_params=_cparams(("parallel",)),
    )(x, wb, wf, wvt)


def _moba(hb, vt, rel_bias, batch, seq):
    m = hb.shape[0]
    nq = seq // MOBA_BLOCK
    nb_pad = LANES * (-(-nq // LANES))
    bias, c_far = _moba_bias_tables(rel_bias)
    hp_n = MOBA_HEADS // 2
    nb_gate = min(nb_pad, 16 * (-(-nq // 16)))
    kern = functools.partial(_moba_kernel, nb_pad=nb_pad, nb_gate=nb_gate)
    return pl.pallas_call(
        kern,
        name="moba",
        grid=(batch, hp_n, nq),
        in_specs=[pl.BlockSpec(memory_space=pltpu.SMEM),
                  pl.BlockSpec((MOBA_BLOCK, LANES), lambda b, h, i: (b * nq + i, h)),
                  pl.BlockSpec((seq, LANES), lambda b, h, i: (b, hp_n + h)),
                  pl.BlockSpec((nq, LANES, MOBA_BLOCK), lambda b, h, i: (b, h, 0)),
                  pl.BlockSpec((2, 2, MOBA_BLOCK, MOBA_BLOCK), lambda b, h, i: (h, 0, 0, 0))],
        out_specs=pl.BlockSpec((MOBA_BLOCK, LANES), lambda b, h, i: (b * nq + i, h)),
        out_shape=jax.ShapeDtypeStruct((m, MOBA_HEADS * MOBA_DH), BF16),
        scratch_shapes=[pltpu.VMEM((nb_pad, LANES), F32),
                        pltpu.VMEM((SCORE_SLOTS, 2, MOBA_BLOCK, MOBA_BLOCK), F32),
                        pltpu.VMEM((SCORE_SLOTS, 2, 1, MOBA_BLOCK), F32)],
        compiler_params=_cparams(("parallel", "parallel", "arbitrary")),
    )(c_far, hb, hb, vt, bias)


LEVELS = (32, 16, 8, 4, 2, 1)


def _chunk_cumsum(la):
    t = la.shape[0]
    row = lax.broadcasted_iota(jnp.int32, la.shape, 0) % CHUNK
    b = la
    sh = 1
    while sh < CHUNK:
        b = b + jnp.where(row >= sh, pltpu.roll(b, sh, 0), 0.0)
        sh *= 2
    return b


def _level_reference(b, h):
    t, w = b.shape
    if 2 * h >= 8:
        b3 = b.reshape(t // (2 * h), 2 * h, w)
        return jnp.broadcast_to(b3[:, h - 1:h, :], b3.shape).reshape(t, w)
    row = lax.broadcasted_iota(jnp.int32, b.shape, 0)
    if h == 2:
        r4 = row % 4
        return jnp.where(r4 == 0, pltpu.roll(b, t - 1, 0),
                         jnp.where(r4 == 1, b,
                                   jnp.where(r4 == 2, pltpu.roll(b, 1, 0), pltpu.roll(b, 2, 0))))
    return jnp.where(row % 2 == 1, pltpu.roll(b, 1, 0), b)


def _pair_masks():
    t = lax.broadcasted_iota(jnp.int32, (CHUNK, 2 * CHUNK), 0)
    s = lax.broadcasted_iota(jnp.int32, (CHUNK, 2 * CHUNK), 1) % CHUNK
    masks = []
    for h in LEVELS:
        masks.append((t // (2 * h) == s // (2 * h)) & ((t // h) % 2 == 1) & ((s // h) % 2 == 0))
    masks.append(t == s)
    return masks


def _stack_heads(x, split):
    lane = lax.broadcasted_iota(jnp.int32, x.shape, 1)
    z = jnp.zeros_like(x)
    return jnp.concatenate([jnp.where(lane < split, x, z), jnp.where(lane >= split, x, z)], axis=0)


def _gated_linear_tile(q, k, v, la, st_ref):
    t = q.shape[0]
    b = _chunk_cumsum(la)
    q_lv, k_lv = [], []
    for h in LEVELS:
        w = jnp.exp(-jnp.abs(b - _level_reference(b, h)))
        q_lv.append((q * w).astype(BF16))
        k_lv.append((k * w).astype(BF16))
    q_lv.append(q.astype(BF16))
    k_lv.append(k.astype(BF16))
    b3 = b.reshape(t // CHUNK, CHUNK, LANES)
    b_last = jnp.broadcast_to(b3[:, CHUNK - 1:CHUNK, :], b3.shape).reshape(t, LANES)
    q_in = (q * jnp.exp(b)).astype(BF16)
    k_out = (k * jnp.exp(b_last - b)).astype(BF16)
    masks = _pair_masks()
    e = lax.broadcasted_iota(jnp.int32, st_ref.shape, 0)
    d = lax.broadcasted_iota(jnp.int32, st_ref.shape, 1)
    same_head = (e < GLA_DV) == (d < GLA_DK)
    outs = []
    for c in range(t // CHUNK):
        rows = slice(c * CHUNK, (c + 1) * CHUNK)
        a = jnp.zeros((CHUNK, 2 * CHUNK), F32)
        for lv in range(len(masks)):
            p = lax.dot_general(q_lv[lv][rows], _stack_heads(k_lv[lv][rows], GLA_DK), NT_DIMS,
                                preferred_element_type=F32)
            a = a + jnp.where(masks[lv], p, 0.0)
        vc = v[rows]
        o = jnp.dot(a.astype(BF16), _stack_heads(vc, GLA_DV), preferred_element_type=F32)
        st = st_ref[...]
        o = o + lax.dot_general(q_in[rows], st.astype(BF16), NT_DIMS, preferred_element_type=F32)
        upd = lax.dot_general(vc, k_out[rows], TN_DIMS, preferred_element_type=F32)
        st_ref[...] = st * jnp.exp(b[(c + 1) * CHUNK - 1:(c + 1) * CHUNK, :]) + jnp.where(same_head, upd, 0.0)
        outs.append(o)
    return jnp.concatenate(outs, axis=0)


def _head_rms_gate(o, norm, gate):
    parts = []
    for hd in range(2):
        oh = o[:, hd * GLA_DV:(hd + 1) * GLA_DV]
        parts.append(oh * lax.rsqrt(jnp.mean(oh * oh, axis=-1, keepdims=True) + HEAD_NORM_EPS))
    return jnp.concatenate(parts, axis=1) * norm * _silu(gate)


def _log_sigmoid(x):
    return jnp.minimum(x, 0.0) - jnp.log1p(jnp.exp(-jnp.abs(x)))


def _gla_kernel(q_ref, k_ref, v_ref, ga_ref, gr_ref, wg_ref, bg_ref, norm_ref, o_ref, st_ref):
    @pl.when(pl.program_id(2) == 0)
    def _():
        st_ref[...] = jnp.zeros_like(st_ref)

    gate_in = jnp.dot(ga_ref[...].astype(BF16), wg_ref[...], preferred_element_type=F32) + bg_ref[...]
    la = _log_sigmoid(gate_in) / GLA_TAU
    q = q_ref[...].astype(F32) * (GLA_DK ** -0.5)
    o = _gated_linear_tile(q, k_ref[...].astype(F32), v_ref[...], la, st_ref)
    o_ref[...] = _head_rms_gate(o, norm_ref[...], gr_ref[...]).astype(o_ref.dtype)


def _gla(hb, hf, w_gate, b_gate, norm, batch, seq):
    m = hb.shape[0]
    tt = min(SEQ_TILE, seq)
    nt = seq // tt
    hp_n = GLA_HEADS // 2
    wg = jnp.zeros((LANES, GLA_HEADS * GLA_DK), BF16).at[:GLA_RANK].set(w_gate.astype(BF16))
    row = lambda b, h, i: b * nt + i
    return pl.pallas_call(
        _gla_kernel,
        name="gla",
        grid=(batch, hp_n, nt),
        in_specs=[pl.BlockSpec((tt, LANES), lambda b, h, i: (row(b, h, i), 8 + h)),
                  pl.BlockSpec((tt, LANES), lambda b, h, i: (row(b, h, i), 10 + h)),
                  pl.BlockSpec((tt, 2 * GLA_DV), lambda b, h, i: (row(b, h, i), 6 + h)),
                  pl.BlockSpec((tt, LANES), lambda b, h, i: (row(b, h, i), 4)),
                  pl.BlockSpec((tt, 2 * GLA_DV), lambda b, h, i: (row(b, h, i), h)),
                  pl.BlockSpec((LANES, LANES), lambda b, h, i: (0, h)),
                  pl.BlockSpec((1, LANES), lambda b, h, i: (0, h)),
                  pl.BlockSpec((1, 2 * GLA_DV), lambda b, h, i: (0, h))],
        out_specs=pl.BlockSpec((tt, 2 * GLA_DV), lambda b, h, i: (row(b, h, i), h)),
        out_shape=jax.ShapeDtypeStruct((m, GLA_HEADS * GLA_DV), BF16),
        scratch_shapes=[pltpu.VMEM((2 * GLA_DV, LANES), F32)],
        compiler_params=_cparams(("parallel", "parallel", "arbitrary")),
    )(hb, hb, hb, hf, hf, wg, b_gate.reshape(1, -1), norm.reshape(1, -1))


def _hgrn_kernel(q_ref, f_ref, v_ref, g_ref, lb_ref, norm_ref, o_ref, st_ref):
    @pl.when(pl.program_id(2) == 0)
    def _():
        st_ref[...] = jnp.zeros_like(st_ref)

    lb = lb_ref[...]
    f = lb + (1.0 - lb) * _sigmoid(f_ref[...])
    o = _gated_linear_tile(q_ref[...], 1.0 - f, v_ref[...].astype(BF16), jnp.log(f), st_ref)
    o_ref[...] = _head_rms_gate(o, norm_ref[...], g_ref[...]).astype(o_ref.dtype)


def _hgrn(h, lb, norm, batch, seq):
    m = h.shape[0]
    tt = min(SEQ_TILE, seq)
    nt = seq // tt
    row = lambda b, hh, i: b * nt + i
    return pl.pallas_call(
        _hgrn_kernel,
        name="hgrn",
        grid=(batch, HGRN_HEADS // 2, nt),
        in_specs=[pl.BlockSpec((tt, LANES), lambda b, hh, i: (row(b, hh, i), hh)),
                  pl.BlockSpec((tt, LANES), lambda b, hh, i: (row(b, hh, i), 2 + hh)),
                  pl.BlockSpec((tt, 2 * GLA_DV), lambda b, hh, i: (row(b, hh, i), 2 + hh)),
                  pl.BlockSpec((tt, 2 * GLA_DV), lambda b, hh, i: (row(b, hh, i), 4 + hh)),
                  pl.BlockSpec((1, LANES), lambda b, hh, i: (0, hh)),
                  pl.BlockSpec((1, 2 * GLA_DV), lambda b, hh, i: (0, hh))],
        out_specs=pl.BlockSpec((tt, 2 * GLA_DV), lambda b, hh, i: (row(b, hh, i), hh)),
        out_shape=jax.ShapeDtypeStruct((m, HGRN_HEADS * GLA_DV), BF16),
        scratch_shapes=[pltpu.VMEM((2 * GLA_DV, LANES), F32)],
        compiler_params=_cparams(("parallel", "parallel", "arbitrary")),
    )(h, h, h, h, lb.reshape(1, -1), norm.reshape(1, -1))


def _rotate(x, cos, sin_signed):
    lane = lax.broadcasted_iota(jnp.int32, x.shape, 1)
    half = GLA_DK // 2
    swapped = jnp.where(lane % GLA_DK < half, pltpu.roll(x, LANES - half, 1), pltpu.roll(x, half, 1))
    return x * cos + swapped * sin_signed


def _ret_kernel(q_ref, k_ref, v_ref, g_ref, cos_ref, sin_ref, dmask_ref, zeta_ref, xi_ref, gch_ref,
                norm_ref, o_ref, st_ref):
    @pl.when(pl.program_id(2) == 0)
    def _():
        st_ref[...] = jnp.zeros_like(st_ref)

    t = q_ref.shape[0]
    cos, sin = cos_ref[...], sin_ref[...]
    q = _rotate(q_ref[...], cos, sin)
    k = _rotate(k_ref[...], cos, sin) * (GLA_DK ** -0.5)
    v = v_ref[...].astype(BF16)
    nc = t // CHUNK
    zeta = jnp.broadcast_to(zeta_ref[0][None], (nc, CHUNK, LANES)).reshape(t, LANES)
    xi = jnp.broadcast_to(xi_ref[0][None], (nc, CHUNK, LANES)).reshape(t, LANES)
    q_b, k_b = q.astype(BF16), k.astype(BF16)
    q_in = (q * xi).astype(BF16)
    k_out = (k * zeta).astype(BF16)
    dmask = dmask_ref[0]
    g_chunk = gch_ref[0]
    e = lax.broadcasted_iota(jnp.int32, st_ref.shape, 0)
    d = lax.broadcasted_iota(jnp.int32, st_ref.shape, 1)
    same_head = (e < GLA_DV) == (d < GLA_DK)
    outs = []
    for c in range(nc):
        rows = slice(c * CHUNK, (c + 1) * CHUNK)
        a = lax.dot_general(q_b[rows], _stack_heads(k_b[rows], GLA_DK), NT_DIMS,
                            preferred_element_type=F32) * dmask
        vc = v[rows]
        o = jnp.dot(a.astype(BF16), _stack_heads(vc, GLA_DV), preferred_element_type=F32)
        st = st_ref[...]
        o = o + lax.dot_general(q_in[rows], st.astype(BF16), NT_DIMS, preferred_element_type=F32)
        upd = lax.dot_general(vc, k_out[rows], TN_DIMS, preferred_element_type=F32)
        st_ref[...] = st * g_chunk + jnp.where(same_head, upd, 0.0)
        outs.append(o)
    o = jnp.concatenate(outs, axis=0)
    parts = []
    for hd in range(2):
        oh = o[:, hd * GLA_DV:(hd + 1) * GLA_DV]
        oh = oh - jnp.mean(oh, axis=-1, keepdims=True)
        parts.append(oh * lax.rsqrt(jnp.mean(oh * oh, axis=-1, keepdims=True) + HEAD_NORM_EPS))
    o_ref[...] = (jnp.concatenate(parts, axis=1) * norm_ref[...] * _silu(g_ref[...])).astype(o_ref.dtype)


def _retention_tables(seq):
    half = GLA_DK // 2
    inv = np.float32(ROPE_BASE) ** (-np.arange(half, dtype=np.float32) / np.float32(half))
    ang = (np.arange(seq, dtype=np.float32)[:, None] * inv[None, :]).astype(np.float64)
    cos, sin = np.cos(ang).astype(np.float32), np.sin(ang).astype(np.float32)
    cos_t = jnp.tile(jnp.asarray(cos), (1, 4))
    sin_t = jnp.tile(jnp.asarray(np.concatenate([-sin, sin], axis=1)), (1, 2))
    log_g = jnp.log1p(-jnp.exp2(-5.0 - jnp.arange(RET_HEADS, dtype=F32)))
    pos = jnp.arange(CHUNK, dtype=F32)
    rel = pos[:, None] - pos[None, :]
    dmask = jnp.where(rel >= 0, jnp.exp(jnp.maximum(rel, 0.0)[None] * log_g[:, None, None]), 0.0)
    zeta = jnp.exp((CHUNK - 1 - pos)[None, :] * log_g[:, None])
    xi = jnp.exp((pos + 1.0)[None, :] * log_g[:, None])
    g_chunk = jnp.exp(CHUNK * log_g)
    hp_n = RET_HEADS // 2
    pair = lambda x: x.reshape(hp_n, 2, *x.shape[1:])
    dmask_p = jnp.concatenate([pair(dmask)[:, 0], pair(dmask)[:, 1]], axis=-1)
    lanes = lambda x: jnp.repeat(pair(x), GLA_DK, axis=1)
    zeta_p = jnp.swapaxes(lanes(zeta), 1, 2)
    xi_p = jnp.swapaxes(lanes(xi), 1, 2)
    gch_p = lanes(g_chunk)[:, None, :]
    return cos_t, sin_t, dmask_p, zeta_p, xi_p, gch_p


def _retention(h, norm, batch, seq):
    m = h.shape[0]
    tt = min(SEQ_TILE, seq)
    nt = seq // tt
    cos_t, sin_t, dmask, zeta, xi, gch = _retention_tables(seq)
    row = lambda b, hh, i: b * nt + i
    const3 = lambda shape: pl.BlockSpec((1,) + shape, lambda b, hh, i: (hh, 0, 0))
    return pl.pallas_call(
        _ret_kernel,
        name="retention",
        grid=(batch, RET_HEADS // 2, nt),
        in_specs=[pl.BlockSpec((tt, LANES), lambda b, hh, i: (row(b, hh, i), 12 + hh)),
                  pl.BlockSpec((tt, LANES), lambda b, hh, i: (row(b, hh, i), 14 + hh)),
                  pl.BlockSpec((tt, 2 * GLA_DV), lambda b, hh, i: (row(b, hh, i), 8 + hh)),
                  pl.BlockSpec((tt, 2 * GLA_DV), lambda b, hh, i: (row(b, hh, i), 10 + hh)),
                  pl.BlockSpec((tt, LANES), lambda b, hh, i: (i, 0)),
                  pl.BlockSpec((tt, LANES), lambda b, hh, i: (i, 0)),
                  const3((CHUNK, 2 * CHUNK)), const3((CHUNK, LANES)), const3((CHUNK, LANES)),
                  const3((1, LANES)),
                  pl.BlockSpec((1, 2 * GLA_DV), lambda b, hh, i: (0, hh))],
        out_specs=pl.BlockSpec((tt, 2 * GLA_DV), lambda b, hh, i: (row(b, hh, i), hh)),
        out_shape=jax.ShapeDtypeStruct((m, RET_HEADS * GLA_DV), BF16),
        scratch_shapes=[pltpu.VMEM((2 * GLA_DV, LANES), F32)],
        compiler_params=_cparams(("parallel", "parallel", "arbitrary")),
    )(h, h, h, h, cos_t, sin_t, dmask, zeta, xi, gch, norm.reshape(1, -1))


def _route(xn, w, b):
    logits = jnp.dot(xn.astype(BF16), w, preferred_element_type=F32) + b
    lane = lax.broadcasted_iota(jnp.int32, logits.shape, 1)
    lane_f = lane.astype(F32)
    logits = jnp.where(lane < N_EXPERTS, logits, -jnp.inf)
    m1 = jnp.max(logits, axis=1, keepdims=True)
    i1 = jnp.min(jnp.where(logits == m1, lane_f, float(LANES)), axis=1, keepdims=True)
    rest = jnp.where(lane_f == i1, -jnp.inf, logits)
    m2 = jnp.max(rest, axis=1, keepdims=True)
    i2 = jnp.min(jnp.where(rest == m2, lane_f, float(LANES)), axis=1, keepdims=True)
    z = jnp.exp(m2 - m1)
    g1 = 1.0 / (1.0 + z)
    experts = jnp.where(lane == 0, i1, jnp.where(lane == 1, i2, 0.0)).astype(jnp.int32)
    gates = jnp.where(lane == 0, g1, jnp.where(lane == 1, z * g1, 0.0))
    return experts, gates


def _outproj_kernel(oa_ref, ob_ref, x_ref, wa_ref, wb_ref, g_ref, b_ref, *rest, alpha):
    y = jnp.dot(oa_ref[...], wa_ref[...], preferred_element_type=F32)
    y = y + jnp.dot(ob_ref[...], wb_ref[...], preferred_element_type=F32)
    xn = _layer_norm(alpha * x_ref[...] + y, g_ref[...], b_ref[...])
    if len(rest) == 1:
        rest[0][...] = xn
    else:
        wr_ref, br_ref, o_ref, e_ref, gt_ref = rest
        o_ref[...] = xn
        e_ref[...], gt_ref[...] = _route(xn, wr_ref[...], br_ref[...])


def _outproj_ln(oa, ob, x, w_out, g, b, alpha, router=None):
    m, d = x.shape
    tm = min(ROW_TILE, m)
    wa = w_out[:oa.shape[1]].astype(BF16)
    wb = w_out[oa.shape[1]:].astype(BF16)
    full = lambda a: pl.BlockSpec(a.shape, lambda i: (0, 0))
    rows = lambda a: pl.BlockSpec((tm, a.shape[1]), lambda i: (i, 0))
    args = [oa, ob, x, wa, wb, g.reshape(1, -1), b.reshape(1, -1)]
    in_specs = [rows(oa), rows(ob), rows(x)] + [full(a) for a in args[3:]]
    out_specs = [rows(x)]
    out_shape = [jax.ShapeDtypeStruct((m, d), F32)]
    if router is not None:
        wr = jnp.zeros((d, LANES), F32).at[:, :N_EXPERTS].set(router[0]).astype(BF16)
        br = jnp.zeros((1, LANES), F32).at[0, :N_EXPERTS].set(router[1])
        args += [wr, br]
        in_specs += [full(wr), full(br)]
        lanes = pl.BlockSpec((tm, LANES), lambda i: (i, 0))
        out_specs += [lanes, lanes]
        out_shape += [jax.ShapeDtypeStruct((m, LANES), jnp.int32), jax.ShapeDtypeStruct((m, LANES), F32)]
    out = pl.pallas_call(
        functools.partial(_outproj_kernel, alpha=alpha),
        name="out_proj_ln",
        grid=(m // tm,),
        in_specs=in_specs,
        out_specs=out_specs,
        out_shape=out_shape,
        compiler_params=_cparams(("parallel",)),
    )(*args)
    if router is None:
        return out[0]
    return out[0], out[1][:, :TOP_K], out[2][:, :TOP_K]


def _ln_ple(x, ffn, p, g, b, wg, wp, alpha):
    y = _layer_norm(alpha * x + ffn, g, b)
    gate = _sigmoid(jnp.dot(y.astype(BF16), wg, preferred_element_type=F32))
    return y + gate * jnp.dot(p.astype(BF16), wp, preferred_element_type=F32)


def _swiglu_chunks(xb_ref, w1, w3, w2):
    n = xb_ref.shape[0] // SWIGLU_ROWS
    h = []
    for c in range(n):
        xc = xb_ref[c * SWIGLU_ROWS:(c + 1) * SWIGLU_ROWS, :]
        h.append((jnp.dot(xc, w1, preferred_element_type=F32), jnp.dot(xc, w3, preferred_element_type=F32)))
    return [jnp.dot((_silu(h1) * h3).astype(BF16), w2, preferred_element_type=F32) for h1, h3 in h]


def _accumulate_steps(parts, acc_ref, finish, n_steps):
    assert n_steps > 1
    f = pl.program_id(1)
    last = n_steps - 1
    chunk = lambda c: slice(c * SWIGLU_ROWS, (c + 1) * SWIGLU_ROWS)

    @pl.when(f == 0)
    def _():
        for c, part in enumerate(parts):
            acc_ref[chunk(c), :] = part

    @pl.when((f > 0) & (f < last))
    def _():
        for c, part in enumerate(parts):
            acc_ref[chunk(c), :] += part

    @pl.when(f == last)
    def _():
        finish(jnp.concatenate([acc_ref[chunk(c), :] + part for c, part in enumerate(parts)], axis=0))


def _swiglu_kernel(x_ref, p_ref, w1_ref, w3_ref, w2_ref, g_ref, b_ref, wg_ref, wp_ref, o_ref,
                   xb_ref, acc_ref, *, alpha, n_steps):
    @pl.when(pl.program_id(1) == 0)
    def _():
        xb_ref[...] = x_ref[...].astype(BF16)

    def finish(ffn):
        o_ref[...] = _ln_ple(x_ref[...], ffn, p_ref[...], g_ref[...], b_ref[...], wg_ref[...], wp_ref[...],
                             alpha)

    _accumulate_steps(_swiglu_chunks(xb_ref, w1_ref[...], w3_ref[...], w2_ref[...]), acc_ref, finish,
                      n_steps)


def _swiglu_ln_ple(x, p, w1, w3, w2, g, b, wg, wp, alpha):
    m, d = x.shape
    tm = min(ROW_TILE, m)
    pad = FFN_F_PAD - w1.shape[1]
    w1p = jnp.pad(w1.astype(BF16), ((0, 0), (0, pad)))
    w3p = jnp.pad(w3.astype(BF16), ((0, 0), (0, pad)))
    w2p = jnp.pad(w2.astype(BF16), ((0, pad), (0, 0)))
    nf = FFN_F_PAD // FFN_F_TILE
    g2, b2 = g.reshape(1, -1), b.reshape(1, -1)
    wgb, wpb = wg.astype(BF16), wp.astype(BF16)
    full = lambda a: pl.BlockSpec(a.shape, lambda i, f: (0, 0))
    rows = lambda a: pl.BlockSpec((tm, a.shape[1]), lambda i, f: (i, 0))
    return pl.pallas_call(
        functools.partial(_swiglu_kernel, alpha=alpha, n_steps=nf),
        name="swiglu_ln_ple",
        grid=(m // tm, nf),
        in_specs=[rows(x), rows(p),
                  pl.BlockSpec((d, FFN_F_TILE), lambda i, f: (0, f)),
                  pl.BlockSpec((d, FFN_F_TILE), lambda i, f: (0, f)),
                  pl.BlockSpec((FFN_F_TILE, d), lambda i, f: (f, 0)),
                  full(g2), full(b2), full(wgb), full(wpb)],
        out_specs=rows(x),
        out_shape=jax.ShapeDtypeStruct((m, d), F32),
        scratch_shapes=[pltpu.VMEM((tm, d), BF16), pltpu.VMEM((tm, d), F32)],
        compiler_params=_cparams(("parallel", "arbitrary")),
    )(x, p, w1p, w3p, w2p, g2, b2, wgb, wpb)


def _experts_kernel(be_ref, x_ref, w1_ref, w3_ref, w2_ref, o_ref, acc_ref, *, n_steps):
    def finish(y):
        o_ref[...] = y.astype(o_ref.dtype)

    _accumulate_steps(_swiglu_chunks(x_ref, w1_ref[0], w3_ref[0], w2_ref[0]), acc_ref, finish, n_steps)


def _experts(xg, blk_e, w1, w3, w2):
    r, d = xg.shape
    fe = w1.shape[2]
    nf = fe // MOE_F_TILE
    grid_spec = pltpu.PrefetchScalarGridSpec(
        num_scalar_prefetch=1,
        grid=(r // MOE_ROW_TILE, nf),
        in_specs=[pl.BlockSpec((MOE_ROW_TILE, d), lambda i, f, be: (i, 0)),
                  pl.BlockSpec((1, d, MOE_F_TILE), lambda i, f, be: (be[i], 0, f)),
                  pl.BlockSpec((1, d, MOE_F_TILE), lambda i, f, be: (be[i], 0, f)),
                  pl.BlockSpec((1, MOE_F_TILE, d), lambda i, f, be: (be[i], f, 0))],
        out_specs=pl.BlockSpec((MOE_ROW_TILE, d), lambda i, f, be: (i, 0)),
        scratch_shapes=[pltpu.VMEM((MOE_ROW_TILE, d), F32)],
    )
    return pl.pallas_call(
        functools.partial(_experts_kernel, n_steps=nf),
        name="experts",
        grid_spec=grid_spec,
        out_shape=jax.ShapeDtypeStruct((r, d), BF16),
        compiler_params=_cparams(("parallel", "arbitrary")),
    )(blk_e, xg, w1, w3, w2)


def _combine_kernel(x_ref, y0_ref, y1_ref, gt_ref, p_ref, g_ref, b_ref, wg_ref, wp_ref, o_ref, *, alpha):
    gt = gt_ref[...]
    ffn = y0_ref[...].astype(F32) * gt[:, 0:1] + y1_ref[...].astype(F32) * gt[:, 1:2]
    o_ref[...] = _ln_ple(x_ref[...], ffn, p_ref[...], g_ref[...], b_ref[...], wg_ref[...], wp_ref[...],
                         alpha)


def _combine_ln_ple(x, y, gates, p, g, b, wg, wp, alpha):
    m, d = x.shape
    tm = min(ROW_TILE, m)
    g2, b2 = g.reshape(1, -1), b.reshape(1, -1)
    wgb, wpb = wg.astype(BF16), wp.astype(BF16)
    gt = jnp.zeros((m, LANES), F32).at[:, :TOP_K].set(gates)
    full = lambda a: pl.BlockSpec(a.shape, lambda i: (0, 0))
    rows = lambda a: pl.BlockSpec((tm, a.shape[1]), lambda i: (i, 0))
    return pl.pallas_call(
        functools.partial(_combine_kernel, alpha=alpha),
        name="combine_ln_ple",
        grid=(m // tm,),
        in_specs=[rows(x), rows(y), pl.BlockSpec((tm, d), lambda i: (i + m // tm, 0)), rows(gt), rows(p),
                  full(g2), full(b2), full(wgb), full(wpb)],
        out_specs=rows(x),
        out_shape=jax.ShapeDtypeStruct((m, d), F32),
        compiler_params=_cparams(("parallel",)),
    )(x, y, y, gt, p, g2, b2, wgb, wpb)


def _moe_ln_ple(x, top_e, gates, p, w1, w3, w2, g, b, wg, wp, alpha):
    m, d = x.shape
    n_asg = m * TOP_K
    e_flat = top_e.reshape(-1)
    onehot = (jnp.arange(N_EXPERTS, dtype=jnp.int32)[:, None] == e_flat[None, :]).astype(jnp.int32)
    csum = jnp.cumsum(onehot, axis=1)
    rank = jnp.sum(csum * onehot, axis=0) - 1
    counts = csum[:, -1]
    padded = (counts + MOE_ROW_TILE - 1) // MOE_ROW_TILE * MOE_ROW_TILE
    p_ends = jnp.cumsum(padded)
    p_starts = p_ends - padded
    pos = (jnp.sum(onehot * p_starts[:, None], axis=0) + rank).reshape(m, TOP_K)
    n_rows = (n_asg // MOE_ROW_TILE + N_EXPERTS) * MOE_ROW_TILE
    blk_start = jnp.arange(n_rows // MOE_ROW_TILE, dtype=jnp.int32) * MOE_ROW_TILE
    blk_e = jnp.minimum(jnp.sum((blk_start[None, :] >= p_ends[:, None]).astype(jnp.int32), axis=0),
                        N_EXPERTS - 1)
    n_pad = n_rows - n_asg
    pad_i = jnp.arange(n_pad, dtype=jnp.int32)
    pad_e = jnp.sum((pad_i[None, :] >= jnp.cumsum(padded - counts)[:, None]).astype(jnp.int32), axis=0)
    asg_i = jnp.arange(n_asg, dtype=jnp.int32)
    keys = jnp.concatenate([e_flat * (2 * n_rows) + asg_i, pad_e * (2 * n_rows) + n_rows + pad_i])
    toks = jnp.concatenate([asg_i // TOP_K, jnp.zeros((n_pad,), jnp.int32)])
    _, tok_buf = lax.sort((keys, toks), num_keys=1)
    xg = x.astype(BF16)[tok_buf]
    y = _experts(xg, blk_e, w1.astype(BF16), w3.astype(BF16), w2.astype(BF16))
    y_tok = y[pos.T.reshape(-1)]
    return _combine_ln_ple(x, y_tok, gates, p, g, b, wg, wp, alpha)


def kernel(x, p, rel_bias, even_w_in, gla_w_gate, gla_b_gate, gla_norm, even_w_out, odd_w_in, hgrn_gamma,
           hgrn_norm, ret_norm, odd_w_out, ln_mix_g, ln_mix_b, ln_ffn_g, ln_ffn_b, ffn_w1, ffn_w3, ffn_w2,
           router_w, router_b, expert_w1, expert_w3, expert_w2, ple_w_gate, ple_w_proj):
    batch, seq, d = x.shape
    depth = p.shape[0]
    m = batch * seq
    alpha = (2.0 * depth) ** 0.25
    lb_p = jax.nn.softmax(hgrn_gamma.astype(F32), axis=0)
    lb_all = jnp.cumsum(lb_p, axis=0) - lb_p[0]
    xf = x.reshape(m, d)
    pf = p.reshape(depth, m, PLE_DIM)
    n_mq = 3 * MOBA_HEADS * MOBA_DH + 2 * GLA_HEADS * GLA_DK + GLA_HEADS * GLA_DV
    for i in range(depth):
        j = i // 2
        if i % 2 == 0:
            w_in = even_w_in[j]
            n_qk = 2 * MOBA_HEADS * MOBA_DH
            n_v = MOBA_HEADS * MOBA_DH
            w_q = w_in[:, :n_v] * (MOBA_DH ** -0.5 * LOG2E)
            w_b = jnp.concatenate([w_q, w_in[:, n_v:n_qk], w_in[:, n_qk + n_v:n_mq]], axis=1).astype(BF16)
            w_vt = w_in[:, n_qk:n_qk + n_v].T.astype(BF16)
            w_f = jnp.concatenate([w_in[:, n_mq + GLA_RANK:], w_in[:, n_mq:n_mq + GLA_RANK],
                                   jnp.zeros((d, LANES - GLA_RANK), F32)], axis=1).astype(BF16)
            hb, hf, vt = _proj_even(xf, w_b, w_f, w_vt)
            o_a = _moba(hb, vt, rel_bias, batch, seq)
            o_b = _gla(hb, hf, gla_w_gate[j], gla_b_gate[j], gla_norm[j], batch, seq)
            xf = _outproj_ln(o_a, o_b, xf, even_w_out[j], ln_mix_g[i], ln_mix_b[i], alpha)
            xf = _swiglu_ln_ple(xf, pf[i], ffn_w1[j], ffn_w3[j], ffn_w2[j], ln_ffn_g[i], ln_ffn_b[i],
                                ple_w_gate[i], ple_w_proj[i], alpha)
        else:
            h = _proj(xf, odd_w_in[j].astype(BF16), F32)
            o_c = _hgrn(h, lb_all[i], hgrn_norm[j], batch, seq)
            o_d = _retention(h, ret_norm[j], batch, seq)
            xf, top_e, gates = _outproj_ln(o_c, o_d, xf, odd_w_out[j], ln_mix_g[i], ln_mix_b[i], alpha,
                                           router=(router_w[j], router_b[j]))
            xf = _moe_ln_ple(xf, top_e, gates, pf[i], expert_w1[j], expert_w3[j], expert_w2[j],
                             ln_ffn_g[i], ln_ffn_b[i], ple_w_gate[i], ple_w_proj[i], alpha)
    return xf.reshape(batch, seq, d)
```

```python
import functools
import math

import numpy as np
import jax
import jax.numpy as jnp
from jax import lax
from jax.experimental import pallas as pl
from jax.experimental.pallas import tpu as pltpu

D_MODEL = 1024
PLE_DIM = 256
LN_EPS = 1e-5
HEAD_NORM_EPS = 1e-6
MOBA_HEADS, MOBA_DH, MOBA_BLOCK, MOBA_TOPK = 8, 64, 256, 3
REL_BUCKETS, REL_MAX_DIST = 32, 128
GLA_HEADS, GLA_DK, GLA_DV, GLA_RANK, GLA_TAU = 4, 64, 128, 16, 16.0
HGRN_HEADS = 4
RET_HEADS = 4
ROPE_BASE = 10000.0
CHUNK = 64
FFN_DENSE = 2752
N_EXPERTS, TOP_K, FFN_EXPERT = 8, 2, 3584

LANES = 128
VMEM_LIMIT = 56 * 1024 * 1024
ROW_TILE = 512
SEQ_TILE = 512
MOE_ROW_TILE = 512
MOE_F_TILE = 1792
FFN_F_PAD = 2816
FFN_F_TILE = 1408
SWIGLU_ROWS = 256
NEG = -1e30
LOG2E = math.log2(math.e)
SCORE_SLOTS = 3
FAR_UNROLL = 2 * SCORE_SLOTS
DENOM_ROWS = 16

BF16 = jnp.bfloat16
F32 = jnp.float32
NT_DIMS = (((1,), (1,)), ((), ()))
TN_DIMS = (((0,), (0,)), ((), ()))


def _cparams(sem):
    return pltpu.CompilerParams(dimension_semantics=sem, vmem_limit_bytes=VMEM_LIMIT)


def _sigmoid(x):
    return 1.0 / (1.0 + jnp.exp(-x))


def _silu(x):
    return x * _sigmoid(x)


def _layer_norm(y, g, b):
    yc = y - jnp.mean(y, axis=-1, keepdims=True)
    return yc * lax.rsqrt(jnp.mean(yc * yc, axis=-1, keepdims=True) + LN_EPS) * g + b


def _proj_kernel(x_ref, w_ref, o_ref):
    o_ref[...] = jnp.dot(x_ref[...].astype(BF16), w_ref[...],
                         preferred_element_type=F32).astype(o_ref.dtype)


def _proj(x, w, out_dtype):
    m, d = x.shape
    n = w.shape[1]
    tm = min(ROW_TILE, m)
    return pl.pallas_call(
        _proj_kernel,
        name="in_proj",
        grid=(m // tm,),
        in_specs=[pl.BlockSpec((tm, d), lambda i: (i, 0)),
                  pl.BlockSpec((d, n), lambda i: (0, 0))],
        out_specs=pl.BlockSpec((tm, n), lambda i: (i, 0)),
        out_shape=jax.ShapeDtypeStruct((m, n), out_dtype),
        compiler_params=_cparams(("parallel",)),
    )(x, w)


def _moba_kernel(c_far_ref, q_ref, k_ref, vt_ref, bias_ref, o_ref, kmean_ref, s_ref, smax_ref, *,
                 nb_pad, nb_gate):
    hp = pl.program_id(1)
    qi = pl.program_id(2)
    tq = MOBA_BLOCK

    @pl.when(qi == 0)
    def _():
        kmean_ref[...] = jnp.zeros_like(kmean_ref)

    k_own = k_ref[pl.ds(pl.multiple_of(qi * tq, tq), tq), :]
    q_tb = q_ref[...].astype(F32).T.astype(BF16)
    kmean = kmean_ref[0:nb_gate, :]
    kmean_ref[pl.ds(qi, 1), :] = jnp.mean(k_own.astype(F32), axis=0, keepdims=True)

    km_hi = kmean.astype(BF16)
    km_lo = (kmean - km_hi.astype(F32)).astype(BF16)
    kdim = lax.broadcasted_iota(jnp.int32, (nb_gate, LANES), 1)
    zero_k = jnp.zeros_like(km_hi)
    gate_lhs = jnp.concatenate(
        [jnp.concatenate([jnp.where((kdim < MOBA_DH) == (hd == 0), part, zero_k) for part in (km_hi, km_lo)],
                         axis=1) for hd in range(2)], axis=0)
    gate = jnp.dot(gate_lhs, jnp.concatenate([q_tb, q_tb], axis=0), preferred_element_type=F32)

    dim = lax.broadcasted_iota(jnp.int32, (LANES, tq), 0)
    blk_t = lax.broadcasted_iota(jnp.int32, (nb_gate, tq), 0)
    blk_tf = blk_t.astype(F32)
    blk = lax.broadcasted_iota(jnp.int32, (tq, nb_pad), 1)
    qs, qa = [], []
    for hd in range(2):
        g = jnp.where(blk_t < qi, gate[hd * nb_gate:(hd + 1) * nb_gate], -jnp.inf)
        sel = jnp.zeros((nb_gate, tq), jnp.bool_)
        for _ in range(MOBA_TOPK):
            gmax = jnp.max(g, axis=0, keepdims=True)
            idx = jnp.min(jnp.where(g == gmax, blk_tf, float(nb_pad)), axis=0, keepdims=True)
            pick = (blk_tf == idx) & (gmax > -jnp.inf)
            sel = sel | pick
            g = jnp.where(pick, -jnp.inf, g)
        qh_t = jnp.where((dim < MOBA_DH) == (hd == 0), q_tb, jnp.zeros_like(q_tb))
        rows = [qh_t, jnp.where(sel, 0.0, NEG).astype(BF16)]
        if nb_pad > nb_gate:
            rows.append(jnp.zeros((nb_pad - nb_gate, tq), BF16))
        qs.append(qh_t)
        qa.append(jnp.concatenate(rows, axis=0))

    ones_rows = jnp.ones((DENOM_ROWS, tq), BF16)

    def v_t(j, hd):
        return jnp.concatenate([vt_ref[j, hd * MOBA_DH:(hd + 1) * MOBA_DH, :], ones_rows], axis=0)

    def scores_into(slot, j):
        onehot = jnp.where(blk == j, 1.0, 0.0).astype(BF16)
        ka = jnp.concatenate([k_ref[pl.ds(pl.multiple_of(j * tq, tq), tq), :], onehot], axis=1)
        for hd in range(2):
            s = jnp.dot(ka, qa[hd], preferred_element_type=F32)
            s_ref[slot, hd] = s
            smax_ref[slot, hd] = jnp.max(s, axis=0, keepdims=True)

    def update(state, s, smax, shift, vt):
        m, acc = state
        m_new = jnp.maximum(m, smax + shift)
        p = jnp.exp2(s - (m_new - shift))
        acc = jnp.exp2(m - m_new) * acc + jnp.dot(vt, p.astype(BF16), preferred_element_type=F32)
        return m_new, acc

    s_own = [jnp.dot(k_own, qs[hd], preferred_element_type=F32) + bias_ref[hd, 0] for hd in range(2)]
    last = pl.num_programs(2) - 1
    for ahead in range(SCORE_SLOTS - 1):
        scores_into(ahead, jnp.minimum(ahead, last))
    state = []
    for hd in range(2):
        m0 = jnp.max(s_own[hd], axis=0, keepdims=True)
        p = jnp.exp2(s_own[hd] - m0)
        state.append((m0, jnp.dot(v_t(qi, hd), p.astype(BF16), preferred_element_type=F32)))

    n_far = qi - 1

    def far_group(jj, st):
        a = FAR_UNROLL * jj
        for i in range(FAR_UNROLL):
            ahead = i + SCORE_SLOTS - 1
            scores_into(ahead % SCORE_SLOTS, jnp.minimum(a + ahead, last))
            shift = [jnp.where(a + i < n_far, c_far_ref[2 * hp + hd], NEG) for hd in range(2)]
            st = tuple(update(st[hd], s_ref[i % SCORE_SLOTS, hd], smax_ref[i % SCORE_SLOTS, hd], shift[hd],
                              v_t(jnp.minimum(a + i, last), hd)) for hd in range(2))
        return st

    state = lax.fori_loop(0, (n_far + FAR_UNROLL - 1) // FAR_UNROLL, far_group, tuple(state))
    has_adj = qi >= 1
    j_adj = jnp.maximum(qi - 1, 0)
    scores_into(0, j_adj)
    out = []
    for hd in range(2):
        s = s_ref[0, hd] + bias_ref[hd, 1]
        _, acc = update(state[hd], s, jnp.max(s, axis=0, keepdims=True), jnp.where(has_adj, 0.0, NEG),
                        v_t(j_adj, hd))
        out.append(acc[:MOBA_DH] / acc[MOBA_DH:MOBA_DH + 1])
    o_ref[...] = jnp.concatenate(out, axis=0).T.astype(o_ref.dtype)


def _t5_bucket(dist):
    n = jnp.maximum(dist, 0)
    max_exact = REL_BUCKETS // 2
    nf = jnp.maximum(n, 1).astype(F32)
    large = max_exact + (jnp.log(nf / max_exact) / math.log(REL_MAX_DIST / max_exact)
                         * (REL_BUCKETS - max_exact)).astype(jnp.int32)
    large = jnp.minimum(large, REL_BUCKETS - 1)
    return jnp.where(n < max_exact, n, large)


def _moba_bias_tables(rel_bias):
    tbl = rel_bias.astype(F32).T
    blk, period = MOBA_BLOCK, 4 * MOBA_BLOCK
    j = jnp.arange(period)
    u = jnp.where(j[None, :] < 2 * blk, tbl[:, _t5_bucket(jnp.minimum(j, 2 * blk - 1))], NEG)
    toeplitz = jnp.tile(u, (1, blk))[:, :blk * (period - 1)].reshape(-1, blk, period - 1)
    own = toeplitz[:, :, :blk]
    adj = toeplitz[:, :, blk:2 * blk]
    return jnp.stack([own, adj], axis=1) * LOG2E, tbl[:, REL_BUCKETS - 1] * LOG2E


def _proj_even_kernel(x_ref, wb_ref, wf_ref, wvt_ref, hb_ref, hf_ref, vt_ref):
    xb = x_ref[...].astype(BF16)
    hb_ref[...] = jnp.dot(xb, wb_ref[...], preferred_element_type=F32).astype(hb_ref.dtype)
    hf_ref[...] = jnp.dot(xb, wf_ref[...], preferred_element_type=F32)
    for c in range(vt_ref.shape[0]):
        xc = xb[c * MOBA_BLOCK:(c + 1) * MOBA_BLOCK]
        vt_ref[c] = lax.dot_general(wvt_ref[...], xc, NT_DIMS, preferred_element_type=F32).astype(vt_ref.dtype)


def _proj_even(x, wb, wf, wvt):
    m, d = x.shape
    tm = min(ROW_TILE, m)
    nv = tm // MOBA_BLOCK
    full = lambda a: pl.BlockSpec(a.shape, lambda i: (0, 0))
    return pl.pallas_call(
        _proj_even_kernel,
        name="in_proj_even",
        grid=(m // tm,),
        in_specs=[pl.BlockSpec((tm, d), lambda i: (i, 0)), full(wb), full(wf), full(wvt)],
        out_specs=[pl.BlockSpec((tm, wb.shape[1]), lambda i: (i, 0)),
                   pl.BlockSpec((tm, wf.shape[1]), lambda i: (i, 0)),
                   pl.BlockSpec((nv, wvt.shape[0], MOBA_BLOCK), lambda i: (i, 0, 0))],
        out_shape=[jax.ShapeDtypeStruct((m, wb.shape[1]), BF16),
                   jax.ShapeDtypeStruct((m, wf.shape[1]), F32),
                   jax.ShapeDtypeStruct((m // MOBA_BLOCK, wvt.shape[0], MOBA_BLOCK), BF16)],
        compiler_params=_cparams(("parallel",)),
    )(x, wb, wf, wvt)


def _moba(hb, vt, rel_bias, batch, seq):
    m = hb.shape[0]
    nq = seq // MOBA_BLOCK
    nb_pad = LANES * (-(-nq // LANES))
    bias, c_far = _moba_bias_tables(rel_bias)
    hp_n = MOBA_HEADS // 2
    nb_gate = min(nb_pad, 16 * (-(-nq // 16)))
    kern = functools.partial(_moba_kernel, nb_pad=nb_pad, nb_gate=nb_gate)
    return pl.pallas_call(
        kern,
        name="moba",
        grid=(batch, hp_n, nq),
        in_specs=[pl.BlockSpec(memory_space=pltpu.SMEM),
                  pl.BlockSpec((MOBA_BLOCK, LANES), lambda b, h, i: (b * nq + i, h)),
                  pl.BlockSpec((seq, LANES), lambda b, h, i: (b, hp_n + h)),
                  pl.BlockSpec((nq, LANES, MOBA_BLOCK), lambda b, h, i: (b, h, 0)),
                  pl.BlockSpec((2, 2, MOBA_BLOCK, MOBA_BLOCK), lambda b, h, i: (h, 0, 0, 0))],
        out_specs=pl.BlockSpec((MOBA_BLOCK, LANES), lambda b, h, i: (b * nq + i, h)),
        out_shape=jax.ShapeDtypeStruct((m, MOBA_HEADS * MOBA_DH), BF16),
        scratch_shapes=[pltpu.VMEM((nb_pad, LANES), F32),
                        pltpu.VMEM((SCORE_SLOTS, 2, MOBA_BLOCK, MOBA_BLOCK), F32),
                        pltpu.VMEM((SCORE_SLOTS, 2, 1, MOBA_BLOCK), F32)],
        compiler_params=_cparams(("parallel", "parallel", "arbitrary")),
    )(c_far, hb, hb, vt, bias)


LEVELS = (32, 16, 8, 4, 2, 1)


def _chunk_cumsum(la):
    t = la.shape[0]
    row = lax.broadcasted_iota(jnp.int32, la.shape, 0) % CHUNK
    b = la
    sh = 1
    while sh < CHUNK:
        b = b + jnp.where(row >= sh, pltpu.roll(b, sh, 0), 0.0)
        sh *= 2
    return b


def _level_reference(b, h):
    t, w = b.shape
    if 2 * h >= 8:
        b3 = b.reshape(t // (2 * h), 2 * h, w)
        return jnp.broadcast_to(b3[:, h - 1:h, :], b3.shape).reshape(t, w)
    row = lax.broadcasted_iota(jnp.int32, b.shape, 0)
    if h == 2:
        r4 = row % 4
        return jnp.where(r4 == 0, pltpu.roll(b, t - 1, 0),
                         jnp.where(r4 == 1, b,
                                   jnp.where(r4 == 2, pltpu.roll(b, 1, 0), pltpu.roll(b, 2, 0))))
    return jnp.where(row % 2 == 1, pltpu.roll(b, 1, 0), b)


def _pair_masks():
    t = lax.broadcasted_iota(jnp.int32, (CHUNK, 2 * CHUNK), 0)
    s = lax.broadcasted_iota(jnp.int32, (CHUNK, 2 * CHUNK), 1) % CHUNK
    masks = []
    for h in LEVELS:
        masks.append((t // (2 * h) == s // (2 * h)) & ((t // h) % 2 == 1) & ((s // h) % 2 == 0))
    masks.append(t == s)
    return masks


def _stack_heads(x, split):
    lane = lax.broadcasted_iota(jnp.int32, x.shape, 1)
    z = jnp.zeros_like(x)
    return jnp.concatenate([jnp.where(lane < split, x, z), jnp.where(lane >= split, x, z)], axis=0)


def _gated_linear_tile(q, k, v, la, st_ref):
    t = q.shape[0]
    b = _chunk_cumsum(la)
    q_lv, k_lv = [], []
    for h in LEVELS:
        w = jnp.exp(-jnp.abs(b - _level_reference(b, h)))
        q_lv.append((q * w).astype(BF16))
        k_lv.append((k * w).astype(BF16))
    q_lv.append(q.astype(BF16))
    k_lv.append(k.astype(BF16))
    b3 = b.reshape(t // CHUNK, CHUNK, LANES)
    b_last = jnp.broadcast_to(b3[:, CHUNK - 1:CHUNK, :], b3.shape).reshape(t, LANES)
    q_in = (q * jnp.exp(b)).astype(BF16)
    k_out = (k * jnp.exp(b_last - b)).astype(BF16)
    masks = _pair_masks()
    e = lax.broadcasted_iota(jnp.int32, st_ref.shape, 0)
    d = lax.broadcasted_iota(jnp.int32, st_ref.shape, 1)
    same_head = (e < GLA_DV) == (d < GLA_DK)
    outs = []
    for c in range(t // CHUNK):
        rows = slice(c * CHUNK, (c + 1) * CHUNK)
        a = jnp.zeros((CHUNK, 2 * CHUNK), F32)
        for lv in range(len(masks)):
            p = lax.dot_general(q_lv[lv][rows], _stack_heads(k_lv[lv][rows], GLA_DK), NT_DIMS,
                                preferred_element_type=F32)
            a = a + jnp.where(masks[lv], p, 0.0)
        vc = v[rows]
        o = jnp.dot(a.astype(BF16), _stack_heads(vc, GLA_DV), preferred_element_type=F32)
        st = st_ref[...]
        o = o + lax.dot_general(q_in[rows], st.astype(BF16), NT_DIMS, preferred_element_type=F32)
        upd = lax.dot_general(vc, k_out[rows], TN_DIMS, preferred_element_type=F32)
        st_ref[...] = st * jnp.exp(b[(c + 1) * CHUNK - 1:(c + 1) * CHUNK, :]) + jnp.where(same_head, upd, 0.0)
        outs.append(o)
    return jnp.concatenate(outs, axis=0)


def _head_rms_gate(o, norm, gate):
    parts = []
    for hd in range(2):
        oh = o[:, hd * GLA_DV:(hd + 1) * GLA_DV]
        parts.append(oh * lax.rsqrt(jnp.mean(oh * oh, axis=-1, keepdims=True) + HEAD_NORM_EPS))
    return jnp.concatenate(parts, axis=1) * norm * _silu(gate)


def _log_sigmoid(x):
    return jnp.minimum(x, 0.0) - jnp.log1p(jnp.exp(-jnp.abs(x)))


def _gla_kernel(q_ref, k_ref, v_ref, ga_ref, gr_ref, wg_ref, bg_ref, norm_ref, o_ref, st_ref):
    @pl.when(pl.program_id(2) == 0)
    def _():
        st_ref[...] = jnp.zeros_like(st_ref)

    gate_in = jnp.dot(ga_ref[...].astype(BF16), wg_ref[...], preferred_element_type=F32) + bg_ref[...]
    la = _log_sigmoid(gate_in) / GLA_TAU
    q = q_ref[...].astype(F32) * (GLA_DK ** -0.5)
    o = _gated_linear_tile(q, k_ref[...].astype(F32), v_ref[...], la, st_ref)
    o_ref[...] = _head_rms_gate(o, norm_ref[...], gr_ref[...]).astype(o_ref.dtype)


def _gla(hb, hf, w_gate, b_gate, norm, batch, seq):
    m = hb.shape[0]
    tt = min(SEQ_TILE, seq)
    nt = seq // tt
    hp_n = GLA_HEADS // 2
    wg = jnp.zeros((LANES, GLA_HEADS * GLA_DK), BF16).at[:GLA_RANK].set(w_gate.astype(BF16))
    row = lambda b, h, i: b * nt + i
    return pl.pallas_call(
        _gla_kernel,
        name="gla",
        grid=(batch, hp_n, nt),
        in_specs=[pl.BlockSpec((tt, LANES), lambda b, h, i: (row(b, h, i), 8 + h)),
                  pl.BlockSpec((tt, LANES), lambda b, h, i: (row(b, h, i), 10 + h)),
                  pl.BlockSpec((tt, 2 * GLA_DV), lambda b, h, i: (row(b, h, i), 6 + h)),
                  pl.BlockSpec((tt, LANES), lambda b, h, i: (row(b, h, i), 4)),
                  pl.BlockSpec((tt, 2 * GLA_DV), lambda b, h, i: (row(b, h, i), h)),
                  pl.BlockSpec((LANES, LANES), lambda b, h, i: (0, h)),
                  pl.BlockSpec((1, LANES), lambda b, h, i: (0, h)),
                  pl.BlockSpec((1, 2 * GLA_DV), lambda b, h, i: (0, h))],
        out_specs=pl.BlockSpec((tt, 2 * GLA_DV), lambda b, h, i: (row(b, h, i), h)),
        out_shape=jax.ShapeDtypeStruct((m, GLA_HEADS * GLA_DV), BF16),
        scratch_shapes=[pltpu.VMEM((2 * GLA_DV, LANES), F32)],
        compiler_params=_cparams(("parallel", "parallel", "arbitrary")),
    )(hb, hb, hb, hf, hf, wg, b_gate.reshape(1, -1), norm.reshape(1, -1))


def _hgrn_kernel(q_ref, f_ref, v_ref, g_ref, lb_ref, norm_ref, o_ref, st_ref):
    @pl.when(pl.program_id(2) == 0)
    def _():
        st_ref[...] = jnp.zeros_like(st_ref)

    lb = lb_ref[...]
    f = lb + (1.0 - lb) * _sigmoid(f_ref[...])
    o = _gated_linear_tile(q_ref[...], 1.0 - f, v_ref[...].astype(BF16), jnp.log(f), st_ref)
    o_ref[...] = _head_rms_gate(o, norm_ref[...], g_ref[...]).astype(o_ref.dtype)


def _hgrn(h, lb, norm, batch, seq):
    m = h.shape[0]
    tt = min(SEQ_TILE, seq)
    nt = seq // tt
    row = lambda b, hh, i: b * nt + i
    return pl.pallas_call(
        _hgrn_kernel,
        name="hgrn",
        grid=(batch, HGRN_HEADS // 2, nt),
        in_specs=[pl.BlockSpec((tt, LANES), lambda b, hh, i: (row(b, hh, i), hh)),
                  pl.BlockSpec((tt, LANES), lambda b, hh, i: (row(b, hh, i), 2 + hh)),
                  pl.BlockSpec((tt, 2 * GLA_DV), lambda b, hh, i: (row(b, hh, i), 2 + hh)),
                  pl.BlockSpec((tt, 2 * GLA_DV), lambda b, hh, i: (row(b, hh, i), 4 + hh)),
                  pl.BlockSpec((1, LANES), lambda b, hh, i: (0, hh)),
                  pl.BlockSpec((1, 2 * GLA_DV), lambda b, hh, i: (0, hh))],
        out_specs=pl.BlockSpec((tt, 2 * GLA_DV), lambda b, hh, i: (row(b, hh, i), hh)),
        out_shape=jax.ShapeDtypeStruct((m, HGRN_HEADS * GLA_DV), BF16),
        scratch_shapes=[pltpu.VMEM((2 * GLA_DV, LANES), F32)],
        compiler_params=_cparams(("parallel", "parallel", "arbitrary")),
    )(h, h, h, h, lb.reshape(1, -1), norm.reshape(1, -1))


def _rotate(x, cos, sin_signed):
    lane = lax.broadcasted_iota(jnp.int32, x.shape, 1)
    half = GLA_DK // 2
    swapped = jnp.where(lane % GLA_DK < half, pltpu.roll(x, LANES - half, 1), pltpu.roll(x, half, 1))
    return x * cos + swapped * sin_signed


def _ret_kernel(q_ref, k_ref, v_ref, g_ref, cos_ref, sin_ref, dmask_ref, zeta_ref, xi_ref, gch_ref,
                norm_ref, o_ref, st_ref):
    @pl.when(pl.program_id(2) == 0)
    def _():
        st_ref[...] = jnp.zeros_like(st_ref)

    t = q_ref.shape[0]
    cos, sin = cos_ref[...], sin_ref[...]
    q = _rotate(q_ref[...], cos, sin)
    k = _rotate(k_ref[...], cos, sin) * (GLA_DK ** -0.5)
    v = v_ref[...].astype(BF16)
    nc = t // CHUNK
    zeta = jnp.broadcast_to(zeta_ref[0][None], (nc, CHUNK, LANES)).reshape(t, LANES)
    xi = jnp.broadcast_to(xi_ref[0][None], (nc, CHUNK, LANES)).reshape(t, LANES)
    q_b, k_b = q.astype(BF16), k.astype(BF16)
    q_in = (q * xi).astype(BF16)
    k_out = (k * zeta).astype(BF16)
    dmask = dmask_ref[0]
    g_chunk = gch_ref[0]
    e = lax.broadcasted_iota(jnp.int32, st_ref.shape, 0)
    d = lax.broadcasted_iota(jnp.int32, st_ref.shape, 1)
    same_head = (e < GLA_DV) == (d < GLA_DK)
    outs = []
    for c in range(nc):
        rows = slice(c * CHUNK, (c + 1) * CHUNK)
        a = lax.dot_general(q_b[rows], _stack_heads(k_b[rows], GLA_DK), NT_DIMS,
                            preferred_element_type=F32) * dmask
        vc = v[rows]
        o = jnp.dot(a.astype(BF16), _stack_heads(vc, GLA_DV), preferred_element_type=F32)
        st = st_ref[...]
        o = o + lax.dot_general(q_in[rows], st.astype(BF16), NT_DIMS, preferred_element_type=F32)
        upd = lax.dot_general(vc, k_out[rows], TN_DIMS, preferred_element_type=F32)
        st_ref[...] = st * g_chunk + jnp.where(same_head, upd, 0.0)
        outs.append(o)
    o = jnp.concatenate(outs, axis=0)
    parts = []
    for hd in range(2):
        oh = o[:, hd * GLA_DV:(hd + 1) * GLA_DV]
        oh = oh - jnp.mean(oh, axis=-1, keepdims=True)
        parts.append(oh * lax.rsqrt(jnp.mean(oh * oh, axis=-1, keepdims=True) + HEAD_NORM_EPS))
    o_ref[...] = (jnp.concatenate(parts, axis=1) * norm_ref[...] * _silu(g_ref[...])).astype(o_ref.dtype)


def _retention_tables(seq):
    half = GLA_DK // 2
    inv = np.float32(ROPE_BASE) ** (-np.arange(half, dtype=np.float32) / np.float32(half))
    ang = (np.arange(seq, dtype=np.float32)[:, None] * inv[None, :]).astype(np.float64)
    cos, sin = np.cos(ang).astype(np.float32), np.sin(ang).astype(np.float32)
    cos_t = jnp.tile(jnp.asarray(cos), (1, 4))
    sin_t = jnp.tile(jnp.asarray(np.concatenate([-sin, sin], axis=1)), (1, 2))
    log_g = jnp.log1p(-jnp.exp2(-5.0 - jnp.arange(RET_HEADS, dtype=F32)))
    pos = jnp.arange(CHUNK, dtype=F32)
    rel = pos[:, None] - pos[None, :]
    dmask = jnp.where(rel >= 0, jnp.exp(jnp.maximum(rel, 0.0)[None] * log_g[:, None, None]), 0.0)
    zeta = jnp.exp((CHUNK - 1 - pos)[None, :] * log_g[:, None])
    xi = jnp.exp((pos + 1.0)[None, :] * log_g[:, None])
    g_chunk = jnp.exp(CHUNK * log_g)
    hp_n = RET_HEADS // 2
    pair = lambda x: x.reshape(hp_n, 2, *x.shape[1:])
    dmask_p = jnp.concatenate([pair(dmask)[:, 0], pair(dmask)[:, 1]], axis=-1)
    lanes = lambda x: jnp.repeat(pair(x), GLA_DK, axis=1)
    zeta_p = jnp.swapaxes(lanes(zeta), 1, 2)
    xi_p = jnp.swapaxes(lanes(xi), 1, 2)
    gch_p = lanes(g_chunk)[:, None, :]
    return cos_t, sin_t, dmask_p, zeta_p, xi_p, gch_p


def _retention(h, norm, batch, seq):
    m = h.shape[0]
    tt = min(SEQ_TILE, seq)
    nt = seq // tt
    cos_t, sin_t, dmask, zeta, xi, gch = _retention_tables(seq)
    row = lambda b, hh, i: b * nt + i
    const3 = lambda shape: pl.BlockSpec((1,) + shape, lambda b, hh, i: (hh, 0, 0))
    return pl.pallas_call(
        _ret_kernel,
        name="retention",
        grid=(batch, RET_HEADS // 2, nt),
        in_specs=[pl.BlockSpec((tt, LANES), lambda b, hh, i: (row(b, hh, i), 12 + hh)),
                  pl.BlockSpec((tt, LANES), lambda b, hh, i: (row(b, hh, i), 14 + hh)),
                  pl.BlockSpec((tt, 2 * GLA_DV), lambda b, hh, i: (row(b, hh, i), 8 + hh)),
                  pl.BlockSpec((tt, 2 * GLA_DV), lambda b, hh, i: (row(b, hh, i), 10 + hh)),
                  pl.BlockSpec((tt, LANES), lambda b, hh, i: (i, 0)),
                  pl.BlockSpec((tt, LANES), lambda b, hh, i: (i, 0)),
                  const3((CHUNK, 2 * CHUNK)), const3((CHUNK, LANES)), const3((CHUNK, LANES)),
                  const3((1, LANES)),
                  pl.BlockSpec((1, 2 * GLA_DV), lambda b, hh, i: (0, hh))],
        out_specs=pl.BlockSpec((tt, 2 * GLA_DV), lambda b, hh, i: (row(b, hh, i), hh)),
        out_shape=jax.ShapeDtypeStruct((m, RET_HEADS * GLA_DV), BF16),
        scratch_shapes=[pltpu.VMEM((2 * GLA_DV, LANES), F32)],
        compiler_params=_cparams(("parallel", "parallel", "arbitrary")),
    )(h, h, h, h, cos_t, sin_t, dmask, zeta, xi, gch, norm.reshape(1, -1))


def _route(xn, w, b):
    logits = jnp.dot(xn.astype(BF16), w, preferred_element_type=F32) + b
    lane = lax.broadcasted_iota(jnp.int32, logits.shape, 1)
    lane_f = lane.astype(F32)
    logits = jnp.where(lane < N_EXPERTS, logits, -jnp.inf)
    m1 = jnp.max(logits, axis=1, keepdims=True)
    i1 = jnp.min(jnp.where(logits == m1, lane_f, float(LANES)), axis=1, keepdims=True)
    rest = jnp.where(lane_f == i1, -jnp.inf, logits)
    m2 = jnp.max(rest, axis=1, keepdims=True)
    i2 = jnp.min(jnp.where(rest == m2, lane_f, float(LANES)), axis=1, keepdims=True)
    z = jnp.exp(m2 - m1)
    g1 = 1.0 / (1.0 + z)
    experts = jnp.where(lane == 0, i1, jnp.where(lane == 1, i2, 0.0)).astype(jnp.int32)
    gates = jnp.where(lane == 0, g1, jnp.where(lane == 1, z * g1, 0.0))
    return experts, gates


def _outproj_kernel(oa_ref, ob_ref, x_ref, wa_ref, wb_ref, g_ref, b_ref, *rest, alpha):
    y = jnp.dot(oa_ref[...], wa_ref[...], preferred_element_type=F32)
    y = y + jnp.dot(ob_ref[...], wb_ref[...], preferred_element_type=F32)
    xn = _layer_norm(alpha * x_ref[...] + y, g_ref[...], b_ref[...])
    if len(rest) == 1:
        rest[0][...] = xn
    else:
        wr_ref, br_ref, o_ref, e_ref, gt_ref = rest
        o_ref[...] = xn
        e_ref[...], gt_ref[...] = _route(xn, wr_ref[...], br_ref[...])


def _outproj_ln(oa, ob, x, w_out, g, b, alpha, router=None):
    m, d = x.shape
    tm = min(ROW_TILE, m)
    wa = w_out[:oa.shape[1]].astype(BF16)
    wb = w_out[oa.shape[1]:].astype(BF16)
    full = lambda a: pl.BlockSpec(a.shape, lambda i: (0, 0))
    rows = lambda a: pl.BlockSpec((tm, a.shape[1]), lambda i: (i, 0))
    args = [oa, ob, x, wa, wb, g.reshape(1, -1), b.reshape(1, -1)]
    in_specs = [rows(oa), rows(ob), rows(x)] + [full(a) for a in args[3:]]
    out_specs = [rows(x)]
    out_shape = [jax.ShapeDtypeStruct((m, d), F32)]
    if router is not None:
        wr = jnp.zeros((d, LANES), F32).at[:, :N_EXPERTS].set(router[0]).astype(BF16)
        br = jnp.zeros((1, LANES), F32).at[0, :N_EXPERTS].set(router[1])
        args += [wr, br]
        in_specs += [full(wr), full(br)]
        lanes = pl.BlockSpec((tm, LANES), lambda i: (i, 0))
        out_specs += [lanes, lanes]
        out_shape += [jax.ShapeDtypeStruct((m, LANES), jnp.int32), jax.ShapeDtypeStruct((m, LANES), F32)]
    out = pl.pallas_call(
        functools.partial(_outproj_kernel, alpha=alpha),
        name="out_proj_ln",
        grid=(m // tm,),
        in_specs=in_specs,
        out_specs=out_specs,
        out_shape=out_shape,
        compiler_params=_cparams(("parallel",)),
    )(*args)
    if router is None:
        return out[0]
    return out[0], out[1][:, :TOP_K], out[2][:, :TOP_K]


def _ln_ple(x, ffn, p, g, b, wg, wp, alpha):
    y = _layer_norm(alpha * x + ffn, g, b)
    gate = _sigmoid(jnp.dot(y.astype(BF16), wg, preferred_element_type=F32))
    return y + gate * jnp.dot(p.astype(BF16), wp, preferred_element_type=F32)


def _swiglu_chunks(xb_ref, w1, w3, w2):
    n = xb_ref.shape[0] // SWIGLU_ROWS
    h = []
    for c in range(n):
        xc = xb_ref[c * SWIGLU_ROWS:(c + 1) * SWIGLU_ROWS, :]
        h.append((jnp.dot(xc, w1, preferred_element_type=F32), jnp.dot(xc, w3, preferred_element_type=F32)))
    return [jnp.dot((_silu(h1) * h3).astype(BF16), w2, preferred_element_type=F32) for h1, h3 in h]


def _accumulate_steps(parts, acc_ref, finish, n_steps):
    assert n_steps > 1
    f = pl.program_id(1)
    last = n_steps - 1
    chunk = lambda c: slice(c * SWIGLU_ROWS, (c + 1) * SWIGLU_ROWS)

    @pl.when(f == 0)
    def _():
        for c, part in enumerate(parts):
            acc_ref[chunk(c), :] = part

    @pl.when((f > 0) & (f < last))
    def _():
        for c, part in enumerate(parts):
            acc_ref[chunk(c), :] += part

    @pl.when(f == last)
    def _():
        finish(jnp.concatenate([acc_ref[chunk(c), :] + part for c, part in enumerate(parts)], axis=0))


def _swiglu_kernel(x_ref, p_ref, w1_ref, w3_ref, w2_ref, g_ref, b_ref, wg_ref, wp_ref, o_ref,
                   xb_ref, acc_ref, *, alpha, n_steps):
    @pl.when(pl.program_id(1) == 0)
    def _():
        xb_ref[...] = x_ref[...].astype(BF16)

    def finish(ffn):
        o_ref[...] = _ln_ple(x_ref[...], ffn, p_ref[...], g_ref[...], b_ref[...], wg_ref[...], wp_ref[...],
                             alpha)

    _accumulate_steps(_swiglu_chunks(xb_ref, w1_ref[...], w3_ref[...], w2_ref[...]), acc_ref, finish,
                      n_steps)


def _swiglu_ln_ple(x, p, w1, w3, w2, g, b, wg, wp, alpha):
    m, d = x.shape
    tm = min(ROW_TILE, m)
    pad = FFN_F_PAD - w1.shape[1]
    w1p = jnp.pad(w1.astype(BF16), ((0, 0), (0, pad)))
    w3p = jnp.pad(w3.astype(BF16), ((0, 0), (0, pad)))
    w2p = jnp.pad(w2.astype(BF16), ((0, pad), (0, 0)))
    nf = FFN_F_PAD // FFN_F_TILE
    g2, b2 = g.reshape(1, -1), b.reshape(1, -1)
    wgb, wpb = wg.astype(BF16), wp.astype(BF16)
    full = lambda a: pl.BlockSpec(a.shape, lambda i, f: (0, 0))
    rows = lambda a: pl.BlockSpec((tm, a.shape[1]), lambda i, f: (i, 0))
    return pl.pallas_call(
        functools.partial(_swiglu_kernel, alpha=alpha, n_steps=nf),
        name="swiglu_ln_ple",
        grid=(m // tm, nf),
        in_specs=[rows(x), rows(p),
                  pl.BlockSpec((d, FFN_F_TILE), lambda i, f: (0, f)),
                  pl.BlockSpec((d, FFN_F_TILE), lambda i, f: (0, f)),
                  pl.BlockSpec((FFN_F_TILE, d), lambda i, f: (f, 0)),
                  full(g2), full(b2), full(wgb), full(wpb)],
        out_specs=rows(x),
        out_shape=jax.ShapeDtypeStruct((m, d), F32),
        scratch_shapes=[pltpu.VMEM((tm, d), BF16), pltpu.VMEM((tm, d), F32)],
        compiler_params=_cparams(("parallel", "arbitrary")),
    )(x, p, w1p, w3p, w2p, g2, b2, wgb, wpb)


def _experts_kernel(be_ref, x_ref, w1_ref, w3_ref, w2_ref, o_ref, acc_ref, *, n_steps):
    def finish(y):
        o_ref[...] = y.astype(o_ref.dtype)

    _accumulate_steps(_swiglu_chunks(x_ref, w1_ref[0], w3_ref[0], w2_ref[0]), acc_ref, finish, n_steps)


def _experts(xg, blk_e, w1, w3, w2):
    r, d = xg.shape
    fe = w1.shape[2]
    nf = fe // MOE_F_TILE
    grid_spec = pltpu.PrefetchScalarGridSpec(
        num_scalar_prefetch=1,
        grid=(r // MOE_ROW_TILE, nf),
        in_specs=[pl.BlockSpec((MOE_ROW_TILE, d), lambda i, f, be: (i, 0)),
                  pl.BlockSpec((1, d, MOE_F_TILE), lambda i, f, be: (be[i], 0, f)),
                  pl.BlockSpec((1, d, MOE_F_TILE), lambda i, f, be: (be[i], 0, f)),
                  pl.BlockSpec((1, MOE_F_TILE, d), lambda i, f, be: (be[i], f, 0))],
        out_specs=pl.BlockSpec((MOE_ROW_TILE, d), lambda i, f, be: (i, 0)),
        scratch_shapes=[pltpu.VMEM((MOE_ROW_TILE, d), F32)],
    )
    return pl.pallas_call(
        functools.partial(_experts_kernel, n_steps=nf),
        name="experts",
        grid_spec=grid_spec,
        out_shape=jax.ShapeDtypeStruct((r, d), BF16),
        compiler_params=_cparams(("parallel", "arbitrary")),
    )(blk_e, xg, w1, w3, w2)


def _combine_kernel(x_ref, y0_ref, y1_ref, gt_ref, p_ref, g_ref, b_ref, wg_ref, wp_ref, o_ref, *, alpha):
    gt = gt_ref[...]
    ffn = y0_ref[...].astype(F32) * gt[:, 0:1] + y1_ref[...].astype(F32) * gt[:, 1:2]
    o_ref[...] = _ln_ple(x_ref[...], ffn, p_ref[...], g_ref[...], b_ref[...], wg_ref[...], wp_ref[...],
                         alpha)


def _combine_ln_ple(x, y, gates, p, g, b, wg, wp, alpha):
    m, d = x.shape
    tm = min(ROW_TILE, m)
    g2, b2 = g.reshape(1, -1), b.reshape(1, -1)
    wgb, wpb = wg.astype(BF16), wp.astype(BF16)
    gt = jnp.zeros((m, LANES), F32).at[:, :TOP_K].set(gates)
    full = lambda a: pl.BlockSpec(a.shape, lambda i: (0, 0))
    rows = lambda a: pl.BlockSpec((tm, a.shape[1]), lambda i: (i, 0))
    return pl.pallas_call(
        functools.partial(_combine_kernel, alpha=alpha),
        name="combine_ln_ple",
        grid=(m // tm,),
        in_specs=[rows(x), rows(y), pl.BlockSpec((tm, d), lambda i: (i + m // tm, 0)), rows(gt), rows(p),
                  full(g2), full(b2), full(wgb), full(wpb)],
        out_specs=rows(x),
        out_shape=jax.ShapeDtypeStruct((m, d), F32),
        compiler_params=_cparams(("parallel",)),
    )(x, y, y, gt, p, g2, b2, wgb, wpb)


def _moe_ln_ple(x, top_e, gates, p, w1, w3, w2, g, b, wg, wp, alpha):
    m, d = x.shape
    n_asg = m * TOP_K
    e_flat = top_e.reshape(-1)
    onehot = (jnp.arange(N_EXPERTS, dtype=jnp.int32)[:, None] == e_flat[None, :]).astype(jnp.int32)
    csum = jnp.cumsum(onehot, axis=1)
    rank = jnp.sum(csum * onehot, axis=0) - 1
    counts = csum[:, -1]
    padded = (counts + MOE_ROW_TILE - 1) // MOE_ROW_TILE * MOE_ROW_TILE
    p_ends = jnp.cumsum(padded)
    p_starts = p_ends - padded
    pos = (jnp.sum(onehot * p_starts[:, None], axis=0) + rank).reshape(m, TOP_K)
    n_rows = (n_asg // MOE_ROW_TILE + N_EXPERTS) * MOE_ROW_TILE
    blk_start = jnp.arange(n_rows // MOE_ROW_TILE, dtype=jnp.int32) * MOE_ROW_TILE
    blk_e = jnp.minimum(jnp.sum((blk_start[None, :] >= p_ends[:, None]).astype(jnp.int32), axis=0),
                        N_EXPERTS - 1)
    n_pad = n_rows - n_asg
    pad_i = jnp.arange(n_pad, dtype=jnp.int32)
    pad_e = jnp.sum((pad_i[None, :] >= jnp.cumsum(padded - counts)[:, None]).astype(jnp.int32), axis=0)
    asg_i = jnp.arange(n_asg, dtype=jnp.int32)
    keys = jnp.concatenate([e_flat * (2 * n_rows) + asg_i, pad_e * (2 * n_rows) + n_rows + pad_i])
    toks = jnp.concatenate([asg_i // TOP_K, jnp.zeros((n_pad,), jnp.int32)])
    _, tok_buf = lax.sort((keys, toks), num_keys=1)
    xg = x.astype(BF16)[tok_buf]
    y = _experts(xg, blk_e, w1.astype(BF16), w3.astype(BF16), w2.astype(BF16))
    y_tok = y[pos.T.reshape(-1)]
    return _combine_ln_ple(x, y_tok, gates, p, g, b, wg, wp, alpha)


def kernel(x, p, rel_bias, even_w_in, gla_w_gate, gla_b_gate, gla_norm, even_w_out, odd_w_in, hgrn_gamma,
           hgrn_norm, ret_norm, odd_w_out, ln_mix_g, ln_mix_b, ln_ffn_g, ln_ffn_b, ffn_w1, ffn_w3, ffn_w2,
           router_w, router_b, expert_w1, expert_w3, expert_w2, ple_w_gate, ple_w_proj):
    batch, seq, d = x.shape
    depth = p.shape[0]
    m = batch * seq
    alpha = (2.0 * depth) ** 0.25
    lb_p = jax.nn.softmax(hgrn_gamma.astype(F32), axis=0)
    lb_all = jnp.cumsum(lb_p, axis=0) - lb_p[0]
    xf = x.reshape(m, d)
    pf = p.reshape(depth, m, PLE_DIM)
    n_mq = 3 * MOBA_HEADS * MOBA_DH + 2 * GLA_HEADS * GLA_DK + GLA_HEADS * GLA_DV
    for i in range(depth):
        j = i // 2
        if i % 2 == 0:
            w_in = even_w_in[j]
            n_qk = 2 * MOBA_HEADS * MOBA_DH
            n_v = MOBA_HEADS * MOBA_DH
            w_q = w_in[:, :n_v] * (MOBA_DH ** -0.5 * LOG2E)
            w_b = jnp.concatenate([w_q, w_in[:, n_v:n_qk], w_in[:, n_qk + n_v:n_mq]], axis=1).astype(BF16)
            w_vt = w_in[:, n_qk:n_qk + n_v].T.astype(BF16)
            w_f = jnp.concatenate([w_in[:, n_mq + GLA_RANK:], w_in[:, n_mq:n_mq + GLA_RANK],
                                   jnp.zeros((d, LANES - GLA_RANK), F32)], axis=1).astype(BF16)
            hb, hf, vt = _proj_even(xf, w_b, w_f, w_vt)
            o_a = _moba(hb, vt, rel_bias, batch, seq)
            o_b = _gla(hb, hf, gla_w_gate[j], gla_b_gate[j], gla_norm[j], batch, seq)
            xf = _outproj_ln(o_a, o_b, xf, even_w_out[j], ln_mix_g[i], ln_mix_b[i], alpha)
            xf = _swiglu_ln_ple(xf, pf[i], ffn_w1[j], ffn_w3[j], ffn_w2[j], ln_ffn_g[i], ln_ffn_b[i],
                                ple_w_gate[i], ple_w_proj[i], alpha)
        else:
            h = _proj(xf, odd_w_in[j].astype(BF16), F32)
            o_c = _hgrn(h, lb_all[i], hgrn_norm[j], batch, seq)
            o_d = _retention(h, ret_norm[j], batch, seq)
            xf, top_e, gates = _outproj_ln(o_c, o_d, xf, odd_w_out[j], ln_mix_g[i], ln_mix_b[i], alpha,
                                           router=(router_w[j], router_b[j]))
            xf = _moe_ln_ple(xf, top_e, gates, pf[i], expert_w1[j], expert_w3[j], expert_w2[j],
                             ln_ffn_g[i], ln_ffn_b[i], ple_w_gate[i], ple_w_proj[i], alpha)
    return xf.reshape(batch, seq, d)
```

```python
import functools
import math

import numpy as np
import jax
import jax.numpy as jnp
from jax import lax
from jax.experimental import pallas as pl
from jax.experimental.pallas import tpu as pltpu
from jax.experimental.pallas import tpu_sc as plsc

D_MODEL = 1024
PLE_DIM = 256
LN_EPS = 1e-5
HEAD_NORM_EPS = 1e-6
MOBA_HEADS, MOBA_DH, MOBA_BLOCK, MOBA_TOPK = 8, 64, 256, 3
REL_BUCKETS, REL_MAX_DIST = 32, 128
GLA_HEADS, GLA_DK, GLA_DV, GLA_RANK, GLA_TAU = 4, 64, 128, 16, 16.0
HGRN_HEADS = 4
RET_HEADS = 4
ROPE_BASE = 10000.0
CHUNK = 64
FFN_DENSE = 2752
N_EXPERTS, TOP_K, FFN_EXPERT = 8, 2, 3584

LANES = 128
VMEM_LIMIT = 56 * 1024 * 1024
ROW_TILE = 512
SEQ_TILE = 512
MOE_ROW_TILE = 512
MOE_F_TILE = 1792
FFN_F_PAD = 2816
FFN_F_TILE = 1408
SWIGLU_ROWS = 256
SC_CORES, SC_SUBCORES = 2, 16
GATHER_WINDOW = 128
GATHER_COLS = 256
NEG = -1e30
LOG2E = math.log2(math.e)
SCORE_SLOTS = 3
FAR_UNROLL = 2 * SCORE_SLOTS
DENOM_ROWS = 16

BF16 = jnp.bfloat16
F32 = jnp.float32
NT_DIMS = (((1,), (1,)), ((), ()))
TN_DIMS = (((0,), (0,)), ((), ()))


def _cparams(sem):
    return pltpu.CompilerParams(dimension_semantics=sem, vmem_limit_bytes=VMEM_LIMIT)


def _sigmoid(x):
    return 1.0 / (1.0 + jnp.exp(-x))


def _silu(x):
    return x * _sigmoid(x)


def _layer_norm(y, g, b):
    yc = y - jnp.mean(y, axis=-1, keepdims=True)
    return yc * lax.rsqrt(jnp.mean(yc * yc, axis=-1, keepdims=True) + LN_EPS) * g + b


def _proj_kernel(x_ref, w_ref, o_ref):
    o_ref[...] = jnp.dot(x_ref[...].astype(BF16), w_ref[...],
                         preferred_element_type=F32).astype(o_ref.dtype)


def _proj(x, w, out_dtype):
    m, d = x.shape
    n = w.shape[1]
    tm = min(ROW_TILE, m)
    return pl.pallas_call(
        _proj_kernel,
        name="in_proj",
        grid=(m // tm,),
        in_specs=[pl.BlockSpec((tm, d), lambda i: (i, 0)),
                  pl.BlockSpec((d, n), lambda i: (0, 0))],
        out_specs=pl.BlockSpec((tm, n), lambda i: (i, 0)),
        out_shape=jax.ShapeDtypeStruct((m, n), out_dtype),
        compiler_params=_cparams(("parallel",)),
    )(x, w)


def _moba_kernel(c_far_ref, q_ref, k_ref, vt_ref, bias_ref, o_ref, kmean_ref, s_ref, smax_ref, *,
                 nb_pad, nb_gate):
    hp = pl.program_id(1)
    qi = pl.program_id(2)
    tq = MOBA_BLOCK

    @pl.when(qi == 0)
    def _():
        kmean_ref[...] = jnp.zeros_like(kmean_ref)

    k_own = k_ref[pl.ds(pl.multiple_of(qi * tq, tq), tq), :]
    q_tb = q_ref[...].astype(F32).T.astype(BF16)
    kmean = kmean_ref[0:nb_gate, :]
    kmean_ref[pl.ds(qi, 1), :] = jnp.mean(k_own.astype(F32), axis=0, keepdims=True)

    km_hi = kmean.astype(BF16)
    km_lo = (kmean - km_hi.astype(F32)).astype(BF16)
    kdim = lax.broadcasted_iota(jnp.int32, (nb_gate, LANES), 1)
    zero_k = jnp.zeros_like(km_hi)
    gate_lhs = jnp.concatenate(
        [jnp.concatenate([jnp.where((kdim < MOBA_DH) == (hd == 0), part, zero_k) for part in (km_hi, km_lo)],
                         axis=1) for hd in range(2)], axis=0)
    gate = jnp.dot(gate_lhs, jnp.concatenate([q_tb, q_tb], axis=0), preferred_element_type=F32)

    dim = lax.broadcasted_iota(jnp.int32, (LANES, tq), 0)
    blk_t = lax.broadcasted_iota(jnp.int32, (nb_gate, tq), 0)
    blk_tf = blk_t.astype(F32)
    blk = lax.broadcasted_iota(jnp.int32, (tq, nb_pad), 1)
    qs, qa = [], []
    for hd in range(2):
        g = jnp.where(blk_t < qi, gate[hd * nb_gate:(hd + 1) * nb_gate], -jnp.inf)
        sel = jnp.zeros((nb_gate, tq), jnp.bool_)
        for _ in range(MOBA_TOPK):
            gmax = jnp.max(g, axis=0, keepdims=True)
            idx = jnp.min(jnp.where(g == gmax, blk_tf, float(nb_pad)), axis=0, keepdims=True)
            pick = (blk_tf == idx) & (gmax > -jnp.inf)
            sel = sel | pick
            g = jnp.where(pick, -jnp.inf, g)
        qh_t = jnp.where((dim < MOBA_DH) == (hd == 0), q_tb, jnp.zeros_like(q_tb))
        rows = [qh_t, jnp.where(sel, 0.0, NEG).astype(BF16)]
        if nb_pad > nb_gate:
            rows.append(jnp.zeros((nb_pad - nb_gate, tq), BF16))
        qs.append(qh_t)
        qa.append(jnp.concatenate(rows, axis=0))

    ones_rows = jnp.ones((DENOM_ROWS, tq), BF16)

    def v_t(j, hd):
        return jnp.concatenate([vt_ref[j, hd * MOBA_DH:(hd + 1) * MOBA_DH, :], ones_rows], axis=0)

    def scores_into(slot, j):
        onehot = jnp.where(blk == j, 1.0, 0.0).astype(BF16)
        ka = jnp.concatenate([k_ref[pl.ds(pl.multiple_of(j * tq, tq), tq), :], onehot], axis=1)
        for hd in range(2):
            s = jnp.dot(ka, qa[hd], preferred_element_type=F32)
            s_ref[slot, hd] = s
            smax_ref[slot, hd] = jnp.max(s, axis=0, keepdims=True)

    def update(state, s, smax, shift, vt):
        m, acc = state
        m_new = jnp.maximum(m, smax + shift)
        p = jnp.exp2(s - (m_new - shift))
        acc = jnp.exp2(m - m_new) * acc + jnp.dot(vt, p.astype(BF16), preferred_element_type=F32)
        return m_new, acc

    s_own = [jnp.dot(k_own, qs[hd], preferred_element_type=F32) + bias_ref[hd, 0] for hd in range(2)]
    last = pl.num_programs(2) - 1
    for ahead in range(SCORE_SLOTS - 1):
        scores_into(ahead, jnp.minimum(ahead, last))
    state = []
    for hd in range(2):
        m0 = jnp.max(s_own[hd], axis=0, keepdims=True)
        p = jnp.exp2(s_own[hd] - m0)
        state.append((m0, jnp.dot(v_t(qi, hd), p.astype(BF16), preferred_element_type=F32)))

    n_far = qi - 1

    def far_group(jj, st):
        a = FAR_UNROLL * jj
        for i in range(FAR_UNROLL):
            ahead = i + SCORE_SLOTS - 1
            scores_into(ahead % SCORE_SLOTS, jnp.minimum(a + ahead, last))
            shift = [jnp.where(a + i < n_far, c_far_ref[2 * hp + hd], NEG) for hd in range(2)]
            st = tuple(update(st[hd], s_ref[i % SCORE_SLOTS, hd], smax_ref[i % SCORE_SLOTS, hd], shift[hd],
                              v_t(jnp.minimum(a + i, last), hd)) for hd in range(2))
        return st

    state = lax.fori_loop(0, (n_far + FAR_UNROLL - 1) // FAR_UNROLL, far_group, tuple(state))
    has_adj = qi >= 1
    j_adj = jnp.maximum(qi - 1, 0)
    scores_into(0, j_adj)
    out = []
    for hd in range(2):
        s = s_ref[0, hd] + bias_ref[hd, 1]
        _, acc = update(state[hd], s, jnp.max(s, axis=0, keepdims=True), jnp.where(has_adj, 0.0, NEG),
                        v_t(j_adj, hd))
        out.append(acc[:MOBA_DH] / acc[MOBA_DH:MOBA_DH + 1])
    o_ref[...] = jnp.concatenate(out, axis=0).T.astype(o_ref.dtype)


def _t5_bucket(dist):
    n = jnp.maximum(dist, 0)
    max_exact = REL_BUCKETS // 2
    nf = jnp.maximum(n, 1).astype(F32)
    large = max_exact + (jnp.log(nf / max_exact) / math.log(REL_MAX_DIST / max_exact)
                         * (REL_BUCKETS - max_exact)).astype(jnp.int32)
    large = jnp.minimum(large, REL_BUCKETS - 1)
    return jnp.where(n < max_exact, n, large)


def _moba_bias_tables(rel_bias):
    tbl = rel_bias.astype(F32).T
    blk, period = MOBA_BLOCK, 4 * MOBA_BLOCK
    j = jnp.arange(period)
    u = jnp.where(j[None, :] < 2 * blk, tbl[:, _t5_bucket(jnp.minimum(j, 2 * blk - 1))], NEG)
    toeplitz = jnp.tile(u, (1, blk))[:, :blk * (period - 1)].reshape(-1, blk, period - 1)
    own = toeplitz[:, :, :blk]
    adj = toeplitz[:, :, blk:2 * blk]
    return jnp.stack([own, adj], axis=1) * LOG2E, tbl[:, REL_BUCKETS - 1] * LOG2E


def _proj_even_kernel(x_ref, wb_ref, wf_ref, wvt_ref, hb_ref, hf_ref, vt_ref):
    xb = x_ref[...].astype(BF16)
    hb_ref[...] = jnp.dot(xb, wb_ref[...], preferred_element_type=F32).astype(hb_ref.dtype)
    hf_ref[...] = jnp.dot(xb, wf_ref[...], preferred_element_type=F32)
    for c in range(vt_ref.shape[0]):
        xc = xb[c * MOBA_BLOCK:(c + 1) * MOBA_BLOCK]
        vt_ref[c] = lax.dot_general(wvt_ref[...], xc, NT_DIMS, preferred_element_type=F32).astype(vt_ref.dtype)


def _proj_even(x, wb, wf, wvt):
    m, d = x.shape
    tm = min(ROW_TILE, m)
    nv = tm // MOBA_BLOCK
    full = lambda a: pl.BlockSpec(a.shape, lambda i: (0, 0))
    return pl.pallas_call(
        _proj_even_kernel,
        name="in_proj_even",
        grid=(m // tm,),
        in_specs=[pl.BlockSpec((tm, d), lambda i: (i, 0)), full(wb), full(wf), full(wvt)],
        out_specs=[pl.BlockSpec((tm, wb.shape[1]), lambda i: (i, 0)),
                   pl.BlockSpec((tm, wf.shape[1]), lambda i: (i, 0)),
                   pl.BlockSpec((nv, wvt.shape[0], MOBA_BLOCK), lambda i: (i, 0, 0))],
        out_shape=[jax.ShapeDtypeStruct((m, wb.shape[1]), BF16),
                   jax.ShapeDtypeStruct((m, wf.shape[1]), F32),
                   jax.ShapeDtypeStruct((m // MOBA_BLOCK, wvt.shape[0], MOBA_BLOCK), BF16)],
        compiler_params=_cparams(("parallel",)),
    )(x, wb, wf, wvt)


def _moba(hb, vt, rel_bias, batch, seq):
    m = hb.shape[0]
    nq = seq // MOBA_BLOCK
    nb_pad = LANES * (-(-nq // LANES))
    bias, c_far = _moba_bias_tables(rel_bias)
    hp_n = MOBA_HEADS // 2
    nb_gate = min(nb_pad, 16 * (-(-nq // 16)))
    kern = functools.partial(_moba_kernel, nb_pad=nb_pad, nb_gate=nb_gate)
    return pl.pallas_call(
        kern,
        name="moba",
        grid=(batch, hp_n, nq),
        in_specs=[pl.BlockSpec(memory_space=pltpu.SMEM),
                  pl.BlockSpec((MOBA_BLOCK, LANES), lambda b, h, i: (b * nq + i, h)),
                  pl.BlockSpec((seq, LANES), lambda b, h, i: (b, hp_n + h)),
                  pl.BlockSpec((nq, LANES, MOBA_BLOCK), lambda b, h, i: (b, h, 0)),
                  pl.BlockSpec((2, 2, MOBA_BLOCK, MOBA_BLOCK), lambda b, h, i: (h, 0, 0, 0))],
        out_specs=pl.BlockSpec((MOBA_BLOCK, LANES), lambda b, h, i: (b * nq + i, h)),
        out_shape=jax.ShapeDtypeStruct((m, MOBA_HEADS * MOBA_DH), BF16),
        scratch_shapes=[pltpu.VMEM((nb_pad, LANES), F32),
                        pltpu.VMEM((SCORE_SLOTS, 2, MOBA_BLOCK, MOBA_BLOCK), F32),
                        pltpu.VMEM((SCORE_SLOTS, 2, 1, MOBA_BLOCK), F32)],
        compiler_params=_cparams(("parallel", "parallel", "arbitrary")),
    )(c_far, hb, hb, vt, bias)


LEVELS = (32, 16, 8, 4, 2, 1)


def _chunk_cumsum(la):
    t = la.shape[0]
    row = lax.broadcasted_iota(jnp.int32, la.shape, 0) % CHUNK
    b = la
    sh = 1
    while sh < CHUNK:
        b = b + jnp.where(row >= sh, pltpu.roll(b, sh, 0), 0.0)
        sh *= 2
    return b


def _level_reference(b, h):
    t, w = b.shape
    if 2 * h >= 8:
        b3 = b.reshape(t // (2 * h), 2 * h, w)
        return jnp.broadcast_to(b3[:, h - 1:h, :], b3.shape).reshape(t, w)
    row = lax.broadcasted_iota(jnp.int32, b.shape, 0)
    if h == 2:
        r4 = row % 4
        return jnp.where(r4 == 0, pltpu.roll(b, t - 1, 0),
                         jnp.where(r4 == 1, b,
                                   jnp.where(r4 == 2, pltpu.roll(b, 1, 0), pltpu.roll(b, 2, 0))))
    return jnp.where(row % 2 == 1, pltpu.roll(b, 1, 0), b)


def _pair_masks():
    t = lax.broadcasted_iota(jnp.int32, (CHUNK, 2 * CHUNK), 0)
    s = lax.broadcasted_iota(jnp.int32, (CHUNK, 2 * CHUNK), 1) % CHUNK
    masks = []
    for h in LEVELS:
        masks.append((t // (2 * h) == s // (2 * h)) & ((t // h) % 2 == 1) & ((s // h) % 2 == 0))
    masks.append(t == s)
    return masks


def _stack_heads(x, split):
    lane = lax.broadcasted_iota(jnp.int32, x.shape, 1)
    z = jnp.zeros_like(x)
    return jnp.concatenate([jnp.where(lane < split, x, z), jnp.where(lane >= split, x, z)], axis=0)


def _gated_linear_tile(q, k, v, la, st_ref):
    t = q.shape[0]
    b = _chunk_cumsum(la)
    q_lv, k_lv = [], []
    for h in LEVELS:
        w = jnp.exp(-jnp.abs(b - _level_reference(b, h)))
        q_lv.append((q * w).astype(BF16))
        k_lv.append((k * w).astype(BF16))
    q_lv.append(q.astype(BF16))
    k_lv.append(k.astype(BF16))
    b3 = b.reshape(t // CHUNK, CHUNK, LANES)
    b_last = jnp.broadcast_to(b3[:, CHUNK - 1:CHUNK, :], b3.shape).reshape(t, LANES)
    q_in = (q * jnp.exp(b)).astype(BF16)
    k_out = (k * jnp.exp(b_last - b)).astype(BF16)
    masks = _pair_masks()
    e = lax.broadcasted_iota(jnp.int32, st_ref.shape, 0)
    d = lax.broadcasted_iota(jnp.int32, st_ref.shape, 1)
    same_head = (e < GLA_DV) == (d < GLA_DK)
    outs = []
    for c in range(t // CHUNK):
        rows = slice(c * CHUNK, (c + 1) * CHUNK)
        a = jnp.zeros((CHUNK, 2 * CHUNK), F32)
        for lv in range(len(masks)):
            p = lax.dot_general(q_lv[lv][rows], _stack_heads(k_lv[lv][rows], GLA_DK), NT_DIMS,
                                preferred_element_type=F32)
            a = a + jnp.where(masks[lv], p, 0.0)
        vc = v[rows]
        o = jnp.dot(a.astype(BF16), _stack_heads(vc, GLA_DV), preferred_element_type=F32)
        st = st_ref[...]
        o = o + lax.dot_general(q_in[rows], st.astype(BF16), NT_DIMS, preferred_element_type=F32)
        upd = lax.dot_general(vc, k_out[rows], TN_DIMS, preferred_element_type=F32)
        st_ref[...] = st * jnp.exp(b[(c + 1) * CHUNK - 1:(c + 1) * CHUNK, :]) + jnp.where(same_head, upd, 0.0)
        outs.append(o)
    return jnp.concatenate(outs, axis=0)


def _head_rms_gate(o, norm, gate):
    parts = []
    for hd in range(2):
        oh = o[:, hd * GLA_DV:(hd + 1) * GLA_DV]
        parts.append(oh * lax.rsqrt(jnp.mean(oh * oh, axis=-1, keepdims=True) + HEAD_NORM_EPS))
    return jnp.concatenate(parts, axis=1) * norm * _silu(gate)


def _log_sigmoid(x):
    return jnp.minimum(x, 0.0) - jnp.log1p(jnp.exp(-jnp.abs(x)))


def _gla_kernel(q_ref, k_ref, v_ref, ga_ref, gr_ref, wg_ref, bg_ref, norm_ref, o_ref, st_ref):
    @pl.when(pl.program_id(2) == 0)
    def _():
        st_ref[...] = jnp.zeros_like(st_ref)

    gate_in = jnp.dot(ga_ref[...].astype(BF16), wg_ref[...], preferred_element_type=F32) + bg_ref[...]
    la = _log_sigmoid(gate_in) / GLA_TAU
    q = q_ref[...].astype(F32) * (GLA_DK ** -0.5)
    o = _gated_linear_tile(q, k_ref[...].astype(F32), v_ref[...], la, st_ref)
    o_ref[...] = _head_rms_gate(o, norm_ref[...], gr_ref[...]).astype(o_ref.dtype)


def _gla(hb, hf, w_gate, b_gate, norm, batch, seq):
    m = hb.shape[0]
    tt = min(SEQ_TILE, seq)
    nt = seq // tt
    hp_n = GLA_HEADS // 2
    wg = jnp.zeros((LANES, GLA_HEADS * GLA_DK), BF16).at[:GLA_RANK].set(w_gate.astype(BF16))
    row = lambda b, h, i: b * nt + i
    return pl.pallas_call(
        _gla_kernel,
        name="gla",
        grid=(batch, hp_n, nt),
        in_specs=[pl.BlockSpec((tt, LANES), lambda b, h, i: (row(b, h, i), 8 + h)),
                  pl.BlockSpec((tt, LANES), lambda b, h, i: (row(b, h, i), 10 + h)),
                  pl.BlockSpec((tt, 2 * GLA_DV), lambda b, h, i: (row(b, h, i), 6 + h)),
                  pl.BlockSpec((tt, LANES), lambda b, h, i: (row(b, h, i), 4)),
                  pl.BlockSpec((tt, 2 * GLA_DV), lambda b, h, i: (row(b, h, i), h)),
                  pl.BlockSpec((LANES, LANES), lambda b, h, i: (0, h)),
                  pl.BlockSpec((1, LANES), lambda b, h, i: (0, h)),
                  pl.BlockSpec((1, 2 * GLA_DV), lambda b, h, i: (0, h))],
        out_specs=pl.BlockSpec((tt, 2 * GLA_DV), lambda b, h, i: (row(b, h, i), h)),
        out_shape=jax.ShapeDtypeStruct((m, GLA_HEADS * GLA_DV), BF16),
        scratch_shapes=[pltpu.VMEM((2 * GLA_DV, LANES), F32)],
        compiler_params=_cparams(("parallel", "parallel", "arbitrary")),
    )(hb, hb, hb, hf, hf, wg, b_gate.reshape(1, -1), norm.reshape(1, -1))


def _hgrn_kernel(q_ref, f_ref, v_ref, g_ref, lb_ref, norm_ref, o_ref, st_ref):
    @pl.when(pl.program_id(2) == 0)
    def _():
        st_ref[...] = jnp.zeros_like(st_ref)

    lb = lb_ref[...]
    f = lb + (1.0 - lb) * _sigmoid(f_ref[...])
    o = _gated_linear_tile(q_ref[...], 1.0 - f, v_ref[...].astype(BF16), jnp.log(f), st_ref)
    o_ref[...] = _head_rms_gate(o, norm_ref[...], g_ref[...]).astype(o_ref.dtype)


def _hgrn(h, lb, norm, batch, seq):
    m = h.shape[0]
    tt = min(SEQ_TILE, seq)
    nt = seq // tt
    row = lambda b, hh, i: b * nt + i
    return pl.pallas_call(
        _hgrn_kernel,
        name="hgrn",
        grid=(batch, HGRN_HEADS // 2, nt),
        in_specs=[pl.BlockSpec((tt, LANES), lambda b, hh, i: (row(b, hh, i), hh)),
                  pl.BlockSpec((tt, LANES), lambda b, hh, i: (row(b, hh, i), 2 + hh)),
                  pl.BlockSpec((tt, 2 * GLA_DV), lambda b, hh, i: (row(b, hh, i), 2 + hh)),
                  pl.BlockSpec((tt, 2 * GLA_DV), lambda b, hh, i: (row(b, hh, i), 4 + hh)),
                  pl.BlockSpec((1, LANES), lambda b, hh, i: (0, hh)),
                  pl.BlockSpec((1, 2 * GLA_DV), lambda b, hh, i: (0, hh))],
        out_specs=pl.BlockSpec((tt, 2 * GLA_DV), lambda b, hh, i: (row(b, hh, i), hh)),
        out_shape=jax.ShapeDtypeStruct((m, HGRN_HEADS * GLA_DV), BF16),
        scratch_shapes=[pltpu.VMEM((2 * GLA_DV, LANES), F32)],
        compiler_params=_cparams(("parallel", "parallel", "arbitrary")),
    )(h, h, h, h, lb.reshape(1, -1), norm.reshape(1, -1))


def _rotate(x, cos, sin_signed):
    lane = lax.broadcasted_iota(jnp.int32, x.shape, 1)
    half = GLA_DK // 2
    swapped = jnp.where(lane % GLA_DK < half, pltpu.roll(x, LANES - half, 1), pltpu.roll(x, half, 1))
    return x * cos + swapped * sin_signed


def _ret_kernel(q_ref, k_ref, v_ref, g_ref, cos_ref, sin_ref, dmask_ref, zeta_ref, xi_ref, gch_ref,
                norm_ref, o_ref, st_ref):
    @pl.when(pl.program_id(2) == 0)
    def _():
        st_ref[...] = jnp.zeros_like(st_ref)

    t = q_ref.shape[0]
    cos, sin = cos_ref[...], sin_ref[...]
    q = _rotate(q_ref[...], cos, sin)
    k = _rotate(k_ref[...], cos, sin) * (GLA_DK ** -0.5)
    v = v_ref[...].astype(BF16)
    nc = t // CHUNK
    zeta = jnp.broadcast_to(zeta_ref[0][None], (nc, CHUNK, LANES)).reshape(t, LANES)
    xi = jnp.broadcast_to(xi_ref[0][None], (nc, CHUNK, LANES)).reshape(t, LANES)
    q_b, k_b = q.astype(BF16), k.astype(BF16)
    q_in = (q * xi).astype(BF16)
    k_out = (k * zeta).astype(BF16)
    dmask = dmask_ref[0]
    g_chunk = gch_ref[0]
    e = lax.broadcasted_iota(jnp.int32, st_ref.shape, 0)
    d = lax.broadcasted_iota(jnp.int32, st_ref.shape, 1)
    same_head = (e < GLA_DV) == (d < GLA_DK)
    outs = []
    for c in range(nc):
        rows = slice(c * CHUNK, (c + 1) * CHUNK)
        a = lax.dot_general(q_b[rows], _stack_heads(k_b[rows], GLA_DK), NT_DIMS,
                            preferred_element_type=F32) * dmask
        vc = v[rows]
        o = jnp.dot(a.astype(BF16), _stack_heads(vc, GLA_DV), preferred_element_type=F32)
        st = st_ref[...]
        o = o + lax.dot_general(q_in[rows], st.astype(BF16), NT_DIMS, preferred_element_type=F32)
        upd = lax.dot_general(vc, k_out[rows], TN_DIMS, preferred_element_type=F32)
        st_ref[...] = st * g_chunk + jnp.where(same_head, upd, 0.0)
        outs.append(o)
    o = jnp.concatenate(outs, axis=0)
    parts = []
    for hd in range(2):
        oh = o[:, hd * GLA_DV:(hd + 1) * GLA_DV]
        oh = oh - jnp.mean(oh, axis=-1, keepdims=True)
        parts.append(oh * lax.rsqrt(jnp.mean(oh * oh, axis=-1, keepdims=True) + HEAD_NORM_EPS))
    o_ref[...] = (jnp.concatenate(parts, axis=1) * norm_ref[...] * _silu(g_ref[...])).astype(o_ref.dtype)


def _retention_tables(seq):
    half = GLA_DK // 2
    inv = np.float32(ROPE_BASE) ** (-np.arange(half, dtype=np.float32) / np.float32(half))
    ang = (np.arange(seq, dtype=np.float32)[:, None] * inv[None, :]).astype(np.float64)
    cos, sin = np.cos(ang).astype(np.float32), np.sin(ang).astype(np.float32)
    cos_t = jnp.tile(jnp.asarray(cos), (1, 4))
    sin_t = jnp.tile(jnp.asarray(np.concatenate([-sin, sin], axis=1)), (1, 2))
    log_g = jnp.log1p(-jnp.exp2(-5.0 - jnp.arange(RET_HEADS, dtype=F32)))
    pos = jnp.arange(CHUNK, dtype=F32)
    rel = pos[:, None] - pos[None, :]
    dmask = jnp.where(rel >= 0, jnp.exp(jnp.maximum(rel, 0.0)[None] * log_g[:, None, None]), 0.0)
    zeta = jnp.exp((CHUNK - 1 - pos)[None, :] * log_g[:, None])
    xi = jnp.exp((pos + 1.0)[None, :] * log_g[:, None])
    g_chunk = jnp.exp(CHUNK * log_g)
    hp_n = RET_HEADS // 2
    pair = lambda x: x.reshape(hp_n, 2, *x.shape[1:])
    dmask_p = jnp.concatenate([pair(dmask)[:, 0], pair(dmask)[:, 1]], axis=-1)
    lanes = lambda x: jnp.repeat(pair(x), GLA_DK, axis=1)
    zeta_p = jnp.swapaxes(lanes(zeta), 1, 2)
    xi_p = jnp.swapaxes(lanes(xi), 1, 2)
    gch_p = lanes(g_chunk)[:, None, :]
    return cos_t, sin_t, dmask_p, zeta_p, xi_p, gch_p


def _retention(h, norm, batch, seq):
    m = h.shape[0]
    tt = min(SEQ_TILE, seq)
    nt = seq // tt
    cos_t, sin_t, dmask, zeta, xi, gch = _retention_tables(seq)
    row = lambda b, hh, i: b * nt + i
    const3 = lambda shape: pl.BlockSpec((1,) + shape, lambda b, hh, i: (hh, 0, 0))
    return pl.pallas_call(
        _ret_kernel,
        name="retention",
        grid=(batch, RET_HEADS // 2, nt),
        in_specs=[pl.BlockSpec((tt, LANES), lambda b, hh, i: (row(b, hh, i), 12 + hh)),
                  pl.BlockSpec((tt, LANES), lambda b, hh, i: (row(b, hh, i), 14 + hh)),
                  pl.BlockSpec((tt, 2 * GLA_DV), lambda b, hh, i: (row(b, hh, i), 8 + hh)),
                  pl.BlockSpec((tt, 2 * GLA_DV), lambda b, hh, i: (row(b, hh, i), 10 + hh)),
                  pl.BlockSpec((tt, LANES), lambda b, hh, i: (i, 0)),
                  pl.BlockSpec((tt, LANES), lambda b, hh, i: (i, 0)),
                  const3((CHUNK, 2 * CHUNK)), const3((CHUNK, LANES)), const3((CHUNK, LANES)),
                  const3((1, LANES)),
                  pl.BlockSpec((1, 2 * GLA_DV), lambda b, hh, i: (0, hh))],
        out_specs=pl.BlockSpec((tt, 2 * GLA_DV), lambda b, hh, i: (row(b, hh, i), hh)),
        out_shape=jax.ShapeDtypeStruct((m, RET_HEADS * GLA_DV), BF16),
        scratch_shapes=[pltpu.VMEM((2 * GLA_DV, LANES), F32)],
        compiler_params=_cparams(("parallel", "parallel", "arbitrary")),
    )(h, h, h, h, cos_t, sin_t, dmask, zeta, xi, gch, norm.reshape(1, -1))


def _route(xn, w, b):
    logits = jnp.dot(xn.astype(BF16), w, preferred_element_type=F32) + b
    lane = lax.broadcasted_iota(jnp.int32, logits.shape, 1)
    lane_f = lane.astype(F32)
    logits = jnp.where(lane < N_EXPERTS, logits, -jnp.inf)
    m1 = jnp.max(logits, axis=1, keepdims=True)
    i1 = jnp.min(jnp.where(logits == m1, lane_f, float(LANES)), axis=1, keepdims=True)
    rest = jnp.where(lane_f == i1, -jnp.inf, logits)
    m2 = jnp.max(rest, axis=1, keepdims=True)
    i2 = jnp.min(jnp.where(rest == m2, lane_f, float(LANES)), axis=1, keepdims=True)
    z = jnp.exp(m2 - m1)
    g1 = 1.0 / (1.0 + z)
    experts = jnp.where(lane == 0, i1, jnp.where(lane == 1, i2, 0.0)).astype(jnp.int32)
    gates = jnp.where(lane == 0, g1, jnp.where(lane == 1, z * g1, 0.0))
    return experts, gates


def _outproj_kernel(oa_ref, ob_ref, x_ref, wa_ref, wb_ref, g_ref, b_ref, *rest, alpha):
    y = jnp.dot(oa_ref[...], wa_ref[...], preferred_element_type=F32)
    y = y + jnp.dot(ob_ref[...], wb_ref[...], preferred_element_type=F32)
    xn = _layer_norm(alpha * x_ref[...] + y, g_ref[...], b_ref[...])
    if len(rest) == 1:
        rest[0][...] = xn
    else:
        wr_ref, br_ref, o_ref, e_ref, gt_ref = rest
        o_ref[...] = xn
        e_ref[...], gt_ref[...] = _route(xn, wr_ref[...], br_ref[...])


def _outproj_ln(oa, ob, x, w_out, g, b, alpha, router=None):
    m, d = x.shape
    tm = min(ROW_TILE, m)
    wa = w_out[:oa.shape[1]].astype(BF16)
    wb = w_out[oa.shape[1]:].astype(BF16)
    full = lambda a: pl.BlockSpec(a.shape, lambda i: (0, 0))
    rows = lambda a: pl.BlockSpec((tm, a.shape[1]), lambda i: (i, 0))
    args = [oa, ob, x, wa, wb, g.reshape(1, -1), b.reshape(1, -1)]
    in_specs = [rows(oa), rows(ob), rows(x)] + [full(a) for a in args[3:]]
    out_specs = [rows(x)]
    out_shape = [jax.ShapeDtypeStruct((m, d), F32)]
    if router is not None:
        wr = jnp.zeros((d, LANES), F32).at[:, :N_EXPERTS].set(router[0]).astype(BF16)
        br = jnp.zeros((1, LANES), F32).at[0, :N_EXPERTS].set(router[1])
        args += [wr, br]
        in_specs += [full(wr), full(br)]
        lanes = pl.BlockSpec((tm, LANES), lambda i: (i, 0))
        out_specs += [lanes, lanes]
        out_shape += [jax.ShapeDtypeStruct((m, LANES), jnp.int32), jax.ShapeDtypeStruct((m, LANES), F32)]
    out = pl.pallas_call(
        functools.partial(_outproj_kernel, alpha=alpha),
        name="out_proj_ln",
        grid=(m // tm,),
        in_specs=in_specs,
        out_specs=out_specs,
        out_shape=out_shape,
        compiler_params=_cparams(("parallel",)),
    )(*args)
    if router is None:
        return out[0]
    return out[0], out[1][:, :TOP_K], out[2][:, :TOP_K]


def _ln_ple(x, ffn, p, g, b, wg, wp, alpha):
    y = _layer_norm(alpha * x + ffn, g, b)
    gate = _sigmoid(jnp.dot(y.astype(BF16), wg, preferred_element_type=F32))
    return y + gate * jnp.dot(p.astype(BF16), wp, preferred_element_type=F32)


def _swiglu_chunks(xb_ref, w1, w3, w2):
    n = xb_ref.shape[0] // SWIGLU_ROWS
    h = []
    for c in range(n):
        xc = xb_ref[c * SWIGLU_ROWS:(c + 1) * SWIGLU_ROWS, :]
        h.append((jnp.dot(xc, w1, preferred_element_type=F32), jnp.dot(xc, w3, preferred_element_type=F32)))
    return [jnp.dot((_silu(h1) * h3).astype(BF16), w2, preferred_element_type=F32) for h1, h3 in h]


def _accumulate_steps(parts, acc_ref, finish, n_steps):
    assert n_steps > 1
    f = pl.program_id(1)
    last = n_steps - 1
    chunk = lambda c: slice(c * SWIGLU_ROWS, (c + 1) * SWIGLU_ROWS)

    @pl.when(f == 0)
    def _():
        for c, part in enumerate(parts):
            acc_ref[chunk(c), :] = part

    @pl.when((f > 0) & (f < last))
    def _():
        for c, part in enumerate(parts):
            acc_ref[chunk(c), :] += part

    @pl.when(f == last)
    def _():
        finish(jnp.concatenate([acc_ref[chunk(c), :] + part for c, part in enumerate(parts)], axis=0))


def _swiglu_kernel(x_ref, p_ref, w1_ref, w3_ref, w2_ref, g_ref, b_ref, wg_ref, wp_ref, o_ref,
                   xb_ref, acc_ref, *, alpha, n_steps):
    @pl.when(pl.program_id(1) == 0)
    def _():
        xb_ref[...] = x_ref[...].astype(BF16)

    def finish(ffn):
        o_ref[...] = _ln_ple(x_ref[...], ffn, p_ref[...], g_ref[...], b_ref[...], wg_ref[...], wp_ref[...],
                             alpha)

    _accumulate_steps(_swiglu_chunks(xb_ref, w1_ref[...], w3_ref[...], w2_ref[...]), acc_ref, finish,
                      n_steps)


def _swiglu_ln_ple(x, p, w1, w3, w2, g, b, wg, wp, alpha):
    m, d = x.shape
    tm = min(ROW_TILE, m)
    pad = FFN_F_PAD - w1.shape[1]
    w1p = jnp.pad(w1.astype(BF16), ((0, 0), (0, pad)))
    w3p = jnp.pad(w3.astype(BF16), ((0, 0), (0, pad)))
    w2p = jnp.pad(w2.astype(BF16), ((0, pad), (0, 0)))
    nf = FFN_F_PAD // FFN_F_TILE
    g2, b2 = g.reshape(1, -1), b.reshape(1, -1)
    wgb, wpb = wg.astype(BF16), wp.astype(BF16)
    full = lambda a: pl.BlockSpec(a.shape, lambda i, f: (0, 0))
    rows = lambda a: pl.BlockSpec((tm, a.shape[1]), lambda i, f: (i, 0))
    return pl.pallas_call(
        functools.partial(_swiglu_kernel, alpha=alpha, n_steps=nf),
        name="swiglu_ln_ple",
        grid=(m // tm, nf),
        in_specs=[rows(x), rows(p),
                  pl.BlockSpec((d, FFN_F_TILE), lambda i, f: (0, f)),
                  pl.BlockSpec((d, FFN_F_TILE), lambda i, f: (0, f)),
                  pl.BlockSpec((FFN_F_TILE, d), lambda i, f: (f, 0)),
                  full(g2), full(b2), full(wgb), full(wpb)],
        out_specs=rows(x),
        out_shape=jax.ShapeDtypeStruct((m, d), F32),
        scratch_shapes=[pltpu.VMEM((tm, d), BF16), pltpu.VMEM((tm, d), F32)],
        compiler_params=_cparams(("parallel", "arbitrary")),
    )(x, p, w1p, w3p, w2p, g2, b2, wgb, wpb)


def _gather_rows(x, idx):
    n, d = x.shape
    parts = d // GATHER_COLS
    k = idx.shape[0] * parts
    pieces = x.reshape(n * parts, GATHER_COLS)
    piece_idx = (idx[:, None] * parts + jnp.arange(parts, dtype=jnp.int32)[None, :]).reshape(1, k)
    mesh = plsc.VectorSubcoreMesh(core_axis_name="core", subcore_axis_name="subcore",
                                  num_cores=SC_CORES, num_subcores=SC_SUBCORES)

    @pl.kernel(out_type=jax.ShapeDtypeStruct((k, GATHER_COLS), x.dtype), mesh=mesh, scratch_types=[],
               name="gather_rows")
    def gather(x_hbm, i_hbm, o_hbm):
        def body(i_vmem, o_vmem):
            pltpu.sync_copy(x_hbm.at[i_vmem.at[0]], o_vmem)

        pltpu.emit_pipeline(
            body,
            grid=(k // GATHER_WINDOW,),
            in_specs=[pl.BlockSpec((1, GATHER_WINDOW), lambda i: (0, i))],
            out_specs=[pl.BlockSpec((GATHER_WINDOW, GATHER_COLS), lambda i: (i, 0))],
            core_axis_name=("core", "subcore"),
            dimension_semantics=(pltpu.PARALLEL,),
        )(i_hbm, o_hbm)

    return gather(pieces, piece_idx).reshape(idx.shape[0], d)


def _experts_kernel(be_ref, x_ref, w1_ref, w3_ref, w2_ref, o_ref, xb_ref, acc_ref, *, n_steps):
    @pl.when(pl.program_id(1) == 0)
    def _():
        xb_ref[...] = x_ref[...].astype(BF16)

    def finish(y):
        o_ref[...] = y.astype(o_ref.dtype)

    _accumulate_steps(_swiglu_chunks(xb_ref, w1_ref[0], w3_ref[0], w2_ref[0]), acc_ref, finish, n_steps)


def _experts(xg, blk_e, w1, w3, w2):
    r, d = xg.shape
    fe = w1.shape[2]
    nf = fe // MOE_F_TILE
    grid_spec = pltpu.PrefetchScalarGridSpec(
        num_scalar_prefetch=1,
        grid=(r // MOE_ROW_TILE, nf),
        in_specs=[pl.BlockSpec((MOE_ROW_TILE, d), lambda i, f, be: (i, 0)),
                  pl.BlockSpec((1, d, MOE_F_TILE), lambda i, f, be: (be[i], 0, f)),
                  pl.BlockSpec((1, d, MOE_F_TILE), lambda i, f, be: (be[i], 0, f)),
                  pl.BlockSpec((1, MOE_F_TILE, d), lambda i, f, be: (be[i], f, 0))],
        out_specs=pl.BlockSpec((MOE_ROW_TILE, d), lambda i, f, be: (i, 0)),
        scratch_shapes=[pltpu.VMEM((MOE_ROW_TILE, d), BF16), pltpu.VMEM((MOE_ROW_TILE, d), F32)],
    )
    return pl.pallas_call(
        functools.partial(_experts_kernel, n_steps=nf),
        name="experts",
        grid_spec=grid_spec,
        out_shape=jax.ShapeDtypeStruct((r, d), F32),
        compiler_params=_cparams(("parallel", "arbitrary")),
    )(blk_e, xg, w1, w3, w2)


def _combine_kernel(x_ref, y0_ref, y1_ref, gt_ref, p_ref, g_ref, b_ref, wg_ref, wp_ref, o_ref, *, alpha):
    gt = gt_ref[...]
    ffn = y0_ref[...].astype(F32) * gt[:, 0:1] + y1_ref[...].astype(F32) * gt[:, 1:2]
    o_ref[...] = _ln_ple(x_ref[...], ffn, p_ref[...], g_ref[...], b_ref[...], wg_ref[...], wp_ref[...],
                         alpha)


def _combine_ln_ple(x, y, gates, p, g, b, wg, wp, alpha):
    m, d = x.shape
    tm = min(ROW_TILE, m)
    g2, b2 = g.reshape(1, -1), b.reshape(1, -1)
    wgb, wpb = wg.astype(BF16), wp.astype(BF16)
    gt = jnp.zeros((m, LANES), F32).at[:, :TOP_K].set(gates)
    full = lambda a: pl.BlockSpec(a.shape, lambda i: (0, 0))
    rows = lambda a: pl.BlockSpec((tm, a.shape[1]), lambda i: (i, 0))
    return pl.pallas_call(
        functools.partial(_combine_kernel, alpha=alpha),
        name="combine_ln_ple",
        grid=(m // tm,),
        in_specs=[rows(x), rows(y), pl.BlockSpec((tm, d), lambda i: (i + m // tm, 0)), rows(gt), rows(p),
                  full(g2), full(b2), full(wgb), full(wpb)],
        out_specs=rows(x),
        out_shape=jax.ShapeDtypeStruct((m, d), F32),
        compiler_params=_cparams(("parallel",)),
    )(x, y, y, gt, p, g2, b2, wgb, wpb)


def _moe_ln_ple(x, top_e, gates, p, w1, w3, w2, g, b, wg, wp, alpha):
    m, d = x.shape
    n_asg = m * TOP_K
    e_flat = top_e.reshape(-1)
    onehot = (jnp.arange(N_EXPERTS, dtype=jnp.int32)[:, None] == e_flat[None, :]).astype(jnp.int32)
    csum = jnp.cumsum(onehot, axis=1)
    rank = jnp.sum(csum * onehot, axis=0) - 1
    counts = csum[:, -1]
    padded = (counts + MOE_ROW_TILE - 1) // MOE_ROW_TILE * MOE_ROW_TILE
    p_ends = jnp.cumsum(padded)
    p_starts = p_ends - padded
    pos = (jnp.sum(onehot * p_starts[:, None], axis=0) + rank).reshape(m, TOP_K)
    n_rows = (n_asg // MOE_ROW_TILE + N_EXPERTS) * MOE_ROW_TILE
    blk_start = jnp.arange(n_rows // MOE_ROW_TILE, dtype=jnp.int32) * MOE_ROW_TILE
    blk_e = jnp.minimum(jnp.sum((blk_start[None, :] >= p_ends[:, None]).astype(jnp.int32), axis=0),
                        N_EXPERTS - 1)
    n_pad = n_rows - n_asg
    pad_i = jnp.arange(n_pad, dtype=jnp.int32)
    pad_e = jnp.sum((pad_i[None, :] >= jnp.cumsum(padded - counts)[:, None]).astype(jnp.int32), axis=0)
    asg_i = jnp.arange(n_asg, dtype=jnp.int32)
    keys = jnp.concatenate([e_flat * (2 * n_rows) + asg_i, pad_e * (2 * n_rows) + n_rows + pad_i])
    toks = jnp.concatenate([asg_i // TOP_K, jnp.zeros((n_pad,), jnp.int32)])
    _, tok_buf = lax.sort((keys, toks), num_keys=1)
    xg = _gather_rows(x, tok_buf)
    y = _experts(xg, blk_e, w1.astype(BF16), w3.astype(BF16), w2.astype(BF16))
    y_tok = _gather_rows(y, pos.T.reshape(-1))
    return _combine_ln_ple(x, y_tok, gates, p, g, b, wg, wp, alpha)


def kernel(x, p, rel_bias, even_w_in, gla_w_gate, gla_b_gate, gla_norm, even_w_out, odd_w_in, hgrn_gamma,
           hgrn_norm, ret_norm, odd_w_out, ln_mix_g, ln_mix_b, ln_ffn_g, ln_ffn_b, ffn_w1, ffn_w3, ffn_w2,
           router_w, router_b, expert_w1, expert_w3, expert_w2, ple_w_gate, ple_w_proj):
    batch, seq, d = x.shape
    depth = p.shape[0]
    m = batch * seq
    alpha = (2.0 * depth) ** 0.25
    lb_p = jax.nn.softmax(hgrn_gamma.astype(F32), axis=0)
    lb_all = jnp.cumsum(lb_p, axis=0) - lb_p[0]
    xf = x.reshape(m, d)
    pf = p.reshape(depth, m, PLE_DIM)
    n_mq = 3 * MOBA_HEADS * MOBA_DH + 2 * GLA_HEADS * GLA_DK + GLA_HEADS * GLA_DV
    for i in range(depth):
        j = i // 2
        if i % 2 == 0:
            w_in = even_w_in[j]
            n_qk = 2 * MOBA_HEADS * MOBA_DH
            n_v = MOBA_HEADS * MOBA_DH
            w_q = w_in[:, :n_v] * (MOBA_DH ** -0.5 * LOG2E)
            w_b = jnp.concatenate([w_q, w_in[:, n_v:n_qk], w_in[:, n_qk + n_v:n_mq]], axis=1).astype(BF16)
            w_vt = w_in[:, n_qk:n_qk + n_v].T.astype(BF16)
            w_f = jnp.concatenate([w_in[:, n_mq + GLA_RANK:], w_in[:, n_mq:n_mq + GLA_RANK],
                                   jnp.zeros((d, LANES - GLA_RANK), F32)], axis=1).astype(BF16)
            hb, hf, vt = _proj_even(xf, w_b, w_f, w_vt)
            o_a = _moba(hb, vt, rel_bias, batch, seq)
            o_b = _gla(hb, hf, gla_w_gate[j], gla_b_gate[j], gla_norm[j], batch, seq)
            xf = _outproj_ln(o_a, o_b, xf, even_w_out[j], ln_mix_g[i], ln_mix_b[i], alpha)
            xf = _swiglu_ln_ple(xf, pf[i], ffn_w1[j], ffn_w3[j], ffn_w2[j], ln_ffn_g[i], ln_ffn_b[i],
                                ple_w_gate[i], ple_w_proj[i], alpha)
        else:
            h = _proj(xf, odd_w_in[j].astype(BF16), F32)
            o_c = _hgrn(h, lb_all[i], hgrn_norm[j], batch, seq)
            o_d = _retention(h, ret_norm[j], batch, seq)
            xf, top_e, gates = _outproj_ln(o_c, o_d, xf, odd_w_out[j], ln_mix_g[i], ln_mix_b[i], alpha,
                                           router=(router_w[j], router_b[j]))
            xf = _moe_ln_ple(xf, top_e, gates, pf[i], expert_w1[j], expert_w3[j], expert_w2[j],
                             ln_ffn_g[i], ln_ffn_b[i], ple_w_gate[i], ple_w_proj[i], alpha)
    return xf.reshape(batch, seq, d)
```

```python
import functools
import math

import numpy as np
import jax
import jax.numpy as jnp
from jax import lax
from jax.experimental import pallas as pl
from jax.experimental.pallas import tpu as pltpu
from jax.experimental.pallas import tpu_sc as plsc

D_MODEL = 1024
PLE_DIM = 256
LN_EPS = 1e-5
HEAD_NORM_EPS = 1e-6
MOBA_HEADS, MOBA_DH, MOBA_BLOCK, MOBA_TOPK = 8, 64, 256, 3
REL_BUCKETS, REL_MAX_DIST = 32, 128
GLA_HEADS, GLA_DK, GLA_DV, GLA_RANK, GLA_TAU = 4, 64, 128, 16, 16.0
HGRN_HEADS = 4
RET_HEADS = 4
ROPE_BASE = 10000.0
CHUNK = 64
FFN_DENSE = 2752
N_EXPERTS, TOP_K, FFN_EXPERT = 8, 2, 3584

LANES = 128
VMEM_LIMIT = 56 * 1024 * 1024
ROW_TILE = 512
SEQ_TILE = 512
MOE_ROW_TILE = 512
MOE_F_TILE = 1792
FFN_F_PAD = 2816
FFN_F_TILE = 1408
SWIGLU_ROWS = 256
SC_CORES, SC_SUBCORES = 2, 16
GATHER_WINDOW = 128
GATHER_ROWS = 32
NEG = -1e30
LOG2E = math.log2(math.e)
SCORE_SLOTS = 3
FAR_UNROLL = 2 * SCORE_SLOTS
DENOM_ROWS = 16

BF16 = jnp.bfloat16
F32 = jnp.float32
NT_DIMS = (((1,), (1,)), ((), ()))
TN_DIMS = (((0,), (0,)), ((), ()))


def _cparams(sem):
    return pltpu.CompilerParams(dimension_semantics=sem, vmem_limit_bytes=VMEM_LIMIT)


def _sigmoid(x):
    return 1.0 / (1.0 + jnp.exp(-x))


def _silu(x):
    return x * _sigmoid(x)


def _layer_norm(y, g, b):
    yc = y - jnp.mean(y, axis=-1, keepdims=True)
    return yc * lax.rsqrt(jnp.mean(yc * yc, axis=-1, keepdims=True) + LN_EPS) * g + b


def _proj_kernel(x_ref, w_ref, o_ref):
    o_ref[...] = jnp.dot(x_ref[...].astype(BF16), w_ref[...],
                         preferred_element_type=F32).astype(o_ref.dtype)


def _proj(x, w, out_dtype):
    m, d = x.shape
    n = w.shape[1]
    tm = min(ROW_TILE, m)
    return pl.pallas_call(
        _proj_kernel,
        name="in_proj",
        grid=(m // tm,),
        in_specs=[pl.BlockSpec((tm, d), lambda i: (i, 0)),
                  pl.BlockSpec((d, n), lambda i: (0, 0))],
        out_specs=pl.BlockSpec((tm, n), lambda i: (i, 0)),
        out_shape=jax.ShapeDtypeStruct((m, n), out_dtype),
        compiler_params=_cparams(("parallel",)),
    )(x, w)


def _moba_kernel(c_far_ref, q_ref, k_ref, vt_ref, bias_ref, o_ref, kmean_ref, s_ref, smax_ref, *,
                 nb_pad, nb_gate):
    hp = pl.program_id(1)
    qi = pl.program_id(2)
    tq = MOBA_BLOCK

    @pl.when(qi == 0)
    def _():
        kmean_ref[...] = jnp.zeros_like(kmean_ref)

    k_own = k_ref[pl.ds(pl.multiple_of(qi * tq, tq), tq), :]
    q_tb = q_ref[...].astype(F32).T.astype(BF16)
    kmean = kmean_ref[0:nb_gate, :]
    kmean_ref[pl.ds(qi, 1), :] = jnp.mean(k_own.astype(F32), axis=0, keepdims=True)

    km_hi = kmean.astype(BF16)
    km_lo = (kmean - km_hi.astype(F32)).astype(BF16)
    kdim = lax.broadcasted_iota(jnp.int32, (nb_gate, LANES), 1)
    zero_k = jnp.zeros_like(km_hi)
    gate_lhs = jnp.concatenate(
        [jnp.concatenate([jnp.where((kdim < MOBA_DH) == (hd == 0), part, zero_k) for part in (km_hi, km_lo)],
                         axis=1) for hd in range(2)], axis=0)
    gate = jnp.dot(gate_lhs, jnp.concatenate([q_tb, q_tb], axis=0), preferred_element_type=F32)

    dim = lax.broadcasted_iota(jnp.int32, (LANES, tq), 0)
    blk_t = lax.broadcasted_iota(jnp.int32, (nb_gate, tq), 0)
    blk_tf = blk_t.astype(F32)
    blk = lax.broadcasted_iota(jnp.int32, (tq, nb_pad), 1)
    qs, qa = [], []
    for hd in range(2):
        g = jnp.where(blk_t < qi, gate[hd * nb_gate:(hd + 1) * nb_gate], -jnp.inf)
        sel = jnp.zeros((nb_gate, tq), jnp.bool_)
        for _ in range(MOBA_TOPK):
            gmax = jnp.max(g, axis=0, keepdims=True)
            idx = jnp.min(jnp.where(g == gmax, blk_tf, float(nb_pad)), axis=0, keepdims=True)
            pick = (blk_tf == idx) & (gmax > -jnp.inf)
            sel = sel | pick
            g = jnp.where(pick, -jnp.inf, g)
        qh_t = jnp.where((dim < MOBA_DH) == (hd == 0), q_tb, jnp.zeros_like(q_tb))
        rows = [qh_t, jnp.where(sel, 0.0, NEG).astype(BF16)]
        if nb_pad > nb_gate:
            rows.append(jnp.zeros((nb_pad - nb_gate, tq), BF16))
        qs.append(qh_t)
        qa.append(jnp.concatenate(rows, axis=0))

    ones_rows = jnp.ones((DENOM_ROWS, tq), BF16)

    def v_t(j, hd):
        return jnp.concatenate([vt_ref[j, hd * MOBA_DH:(hd + 1) * MOBA_DH, :], ones_rows], axis=0)

    def scores_into(slot, j):
        onehot = jnp.where(blk == j, 1.0, 0.0).astype(BF16)
        ka = jnp.concatenate([k_ref[pl.ds(pl.multiple_of(j * tq, tq), tq), :], onehot], axis=1)
        for hd in range(2):
            s = jnp.dot(ka, qa[hd], preferred_element_type=F32)
            s_ref[slot, hd] = s
            smax_ref[slot, hd] = jnp.max(s, axis=0, keepdims=True)

    def update(state, s, smax, shift, vt):
        m, acc = state
        m_new = jnp.maximum(m, smax + shift)
        p = jnp.exp2(s - (m_new - shift))
        acc = jnp.exp2(m - m_new) * acc + jnp.dot(vt, p.astype(BF16), preferred_element_type=F32)
        return m_new, acc

    s_own = [jnp.dot(k_own, qs[hd], preferred_element_type=F32) + bias_ref[hd, 0] for hd in range(2)]
    last = pl.num_programs(2) - 1
    for ahead in range(SCORE_SLOTS - 1):
        scores_into(ahead, jnp.minimum(ahead, last))
    state = []
    for hd in range(2):
        m0 = jnp.max(s_own[hd], axis=0, keepdims=True)
        p = jnp.exp2(s_own[hd] - m0)
        state.append((m0, jnp.dot(v_t(qi, hd), p.astype(BF16), preferred_element_type=F32)))

    n_far = qi - 1

    def far_group(jj, st):
        a = FAR_UNROLL * jj
        for i in range(FAR_UNROLL):
            ahead = i + SCORE_SLOTS - 1
            scores_into(ahead % SCORE_SLOTS, jnp.minimum(a + ahead, last))
            shift = [jnp.where(a + i < n_far, c_far_ref[2 * hp + hd], NEG) for hd in range(2)]
            st = tuple(update(st[hd], s_ref[i % SCORE_SLOTS, hd], smax_ref[i % SCORE_SLOTS, hd], shift[hd],
                              v_t(jnp.minimum(a + i, last), hd)) for hd in range(2))
        return st

    state = lax.fori_loop(0, (n_far + FAR_UNROLL - 1) // FAR_UNROLL, far_group, tuple(state))
    has_adj = qi >= 1
    j_adj = jnp.maximum(qi - 1, 0)
    scores_into(0, j_adj)
    out = []
    for hd in range(2):
        s = s_ref[0, hd] + bias_ref[hd, 1]
        _, acc = update(state[hd], s, jnp.max(s, axis=0, keepdims=True), jnp.where(has_adj, 0.0, NEG),
                        v_t(j_adj, hd))
        out.append(acc[:MOBA_DH] / acc[MOBA_DH:MOBA_DH + 1])
    o_ref[...] = jnp.concatenate(out, axis=0).T.astype(o_ref.dtype)


def _t5_bucket(dist):
    n = jnp.maximum(dist, 0)
    max_exact = REL_BUCKETS // 2
    nf = jnp.maximum(n, 1).astype(F32)
    large = max_exact + (jnp.log(nf / max_exact) / math.log(REL_MAX_DIST / max_exact)
                         * (REL_BUCKETS - max_exact)).astype(jnp.int32)
    large = jnp.minimum(large, REL_BUCKETS - 1)
    return jnp.where(n < max_exact, n, large)


def _moba_bias_tables(rel_bias):
    tbl = rel_bias.astype(F32).T
    blk, period = MOBA_BLOCK, 4 * MOBA_BLOCK
    j = jnp.arange(period)
    u = jnp.where(j[None, :] < 2 * blk, tbl[:, _t5_bucket(jnp.minimum(j, 2 * blk - 1))], NEG)
    toeplitz = jnp.tile(u, (1, blk))[:, :blk * (period - 1)].reshape(-1, blk, period - 1)
    own = toeplitz[:, :, :blk]
    adj = toeplitz[:, :, blk:2 * blk]
    return jnp.stack([own, adj], axis=1) * LOG2E, tbl[:, REL_BUCKETS - 1] * LOG2E


def _proj_even_kernel(x_ref, wb_ref, wf_ref, wvt_ref, hb_ref, hf_ref, vt_ref):
    xb = x_ref[...].astype(BF16)
    hb_ref[...] = jnp.dot(xb, wb_ref[...], preferred_element_type=F32).astype(hb_ref.dtype)
    hf_ref[...] = jnp.dot(xb, wf_ref[...], preferred_element_type=F32)
    for c in range(vt_ref.shape[0]):
        xc = xb[c * MOBA_BLOCK:(c + 1) * MOBA_BLOCK]
        vt_ref[c] = lax.dot_general(wvt_ref[...], xc, NT_DIMS, preferred_element_type=F32).astype(vt_ref.dtype)


def _proj_even(x, wb, wf, wvt):
    m, d = x.shape
    tm = min(ROW_TILE, m)
    nv = tm // MOBA_BLOCK
    full = lambda a: pl.BlockSpec(a.shape, lambda i: (0, 0))
    return pl.pallas_call(
        _proj_even_kernel,
        name="in_proj_even",
        grid=(m // tm,),
        in_specs=[pl.BlockSpec((tm, d), lambda i: (i, 0)), full(wb), full(wf), full(wvt)],
        out_specs=[pl.BlockSpec((tm, wb.shape[1]), lambda i: (i, 0)),
                   pl.BlockSpec((tm, wf.shape[1]), lambda i: (i, 0)),
                   pl.BlockSpec((nv, wvt.shape[0], MOBA_BLOCK), lambda i: (i, 0, 0))],
        out_shape=[jax.ShapeDtypeStruct((m, wb.shape[1]), BF16),
                   jax.ShapeDtypeStruct((m, wf.shape[1]), F32),
                   jax.ShapeDtypeStruct((m // MOBA_BLOCK, wvt.shape[0], MOBA_BLOCK), BF16)],
        compiler_params=_cparams(("parallel",)),
    )(x, wb, wf, wvt)


def _moba(hb, vt, rel_bias, batch, seq):
    m = hb.shape[0]
    nq = seq // MOBA_BLOCK
    nb_pad = LANES * (-(-nq // LANES))
    bias, c_far = _moba_bias_tables(rel_bias)
    hp_n = MOBA_HEADS // 2
    nb_gate = min(nb_pad, 16 * (-(-nq // 16)))
    kern = functools.partial(_moba_kernel, nb_pad=nb_pad, nb_gate=nb_gate)
    return pl.pallas_call(
        kern,
        name="moba",
        grid=(batch, hp_n, nq),
        in_specs=[pl.BlockSpec(memory_space=pltpu.SMEM),
                  pl.BlockSpec((MOBA_BLOCK, LANES), lambda b, h, i: (b * nq + i, h)),
                  pl.BlockSpec((seq, LANES), lambda b, h, i: (b, hp_n + h)),
                  pl.BlockSpec((nq, LANES, MOBA_BLOCK), lambda b, h, i: (b, h, 0)),
                  pl.BlockSpec((2, 2, MOBA_BLOCK, MOBA_BLOCK), lambda b, h, i: (h, 0, 0, 0))],
        out_specs=pl.BlockSpec((MOBA_BLOCK, LANES), lambda b, h, i: (b * nq + i, h)),
        out_shape=jax.ShapeDtypeStruct((m, MOBA_HEADS * MOBA_DH), BF16),
        scratch_shapes=[pltpu.VMEM((nb_pad, LANES), F32),
                        pltpu.VMEM((SCORE_SLOTS, 2, MOBA_BLOCK, MOBA_BLOCK), F32),
                        pltpu.VMEM((SCORE_SLOTS, 2, 1, MOBA_BLOCK), F32)],
        compiler_params=_cparams(("parallel", "parallel", "arbitrary")),
    )(c_far, hb, hb, vt, bias)


LEVELS = (32, 16, 8, 4, 2, 1)


def _chunk_cumsum(la):
    t = la.shape[0]
    row = lax.broadcasted_iota(jnp.int32, la.shape, 0) % CHUNK
    b = la
    sh = 1
    while sh < CHUNK:
        b = b + jnp.where(row >= sh, pltpu.roll(b, sh, 0), 0.0)
        sh *= 2
    return b


def _level_reference(b, h):
    t, w = b.shape
    if 2 * h >= 8:
        b3 = b.reshape(t // (2 * h), 2 * h, w)
        return jnp.broadcast_to(b3[:, h - 1:h, :], b3.shape).reshape(t, w)
    row = lax.broadcasted_iota(jnp.int32, b.shape, 0)
    if h == 2:
        r4 = row % 4
        return jnp.where(r4 == 0, pltpu.roll(b, t - 1, 0),
                         jnp.where(r4 == 1, b,
                                   jnp.where(r4 == 2, pltpu.roll(b, 1, 0), pltpu.roll(b, 2, 0))))
    return jnp.where(row % 2 == 1, pltpu.roll(b, 1, 0), b)


def _pair_masks():
    t = lax.broadcasted_iota(jnp.int32, (CHUNK, 2 * CHUNK), 0)
    s = lax.broadcasted_iota(jnp.int32, (CHUNK, 2 * CHUNK), 1) % CHUNK
    masks = []
    for h in LEVELS:
        masks.append((t // (2 * h) == s // (2 * h)) & ((t // h) % 2 == 1) & ((s // h) % 2 == 0))
    masks.append(t == s)
    return masks


def _stack_heads(x, split):
    lane = lax.broadcasted_iota(jnp.int32, x.shape, 1)
    z = jnp.zeros_like(x)
    return jnp.concatenate([jnp.where(lane < split, x, z), jnp.where(lane >= split, x, z)], axis=0)


def _gated_linear_tile(q, k, v, la, st_ref):
    t = q.shape[0]
    b = _chunk_cumsum(la)
    q_lv, k_lv = [], []
    for h in LEVELS:
        w = jnp.exp(-jnp.abs(b - _level_reference(b, h)))
        q_lv.append((q * w).astype(BF16))
        k_lv.append((k * w).astype(BF16))
    q_lv.append(q.astype(BF16))
    k_lv.append(k.astype(BF16))
    b3 = b.reshape(t // CHUNK, CHUNK, LANES)
    b_last = jnp.broadcast_to(b3[:, CHUNK - 1:CHUNK, :], b3.shape).reshape(t, LANES)
    q_in = (q * jnp.exp(b)).astype(BF16)
    k_out = (k * jnp.exp(b_last - b)).astype(BF16)
    masks = _pair_masks()
    e = lax.broadcasted_iota(jnp.int32, st_ref.shape, 0)
    d = lax.broadcasted_iota(jnp.int32, st_ref.shape, 1)
    same_head = (e < GLA_DV) == (d < GLA_DK)
    outs = []
    for c in range(t // CHUNK):
        rows = slice(c * CHUNK, (c + 1) * CHUNK)
        a = jnp.zeros((CHUNK, 2 * CHUNK), F32)
        for lv in range(len(masks)):
            p = lax.dot_general(q_lv[lv][rows], _stack_heads(k_lv[lv][rows], GLA_DK), NT_DIMS,
                                preferred_element_type=F32)
            a = a + jnp.where(masks[lv], p, 0.0)
        vc = v[rows]
        o = jnp.dot(a.astype(BF16), _stack_heads(vc, GLA_DV), preferred_element_type=F32)
        st = st_ref[...]
        o = o + lax.dot_general(q_in[rows], st.astype(BF16), NT_DIMS, preferred_element_type=F32)
        upd = lax.dot_general(vc, k_out[rows], TN_DIMS, preferred_element_type=F32)
        st_ref[...] = st * jnp.exp(b[(c + 1) * CHUNK - 1:(c + 1) * CHUNK, :]) + jnp.where(same_head, upd, 0.0)
        outs.append(o)
    return jnp.concatenate(outs, axis=0)


def _head_rms_gate(o, norm, gate):
    parts = []
    for hd in range(2):
        oh = o[:, hd * GLA_DV:(hd + 1) * GLA_DV]
        parts.append(oh * lax.rsqrt(jnp.mean(oh * oh, axis=-1, keepdims=True) + HEAD_NORM_EPS))
    return jnp.concatenate(parts, axis=1) * norm * _silu(gate)


def _log_sigmoid(x):
    return jnp.minimum(x, 0.0) - jnp.log1p(jnp.exp(-jnp.abs(x)))


def _gla_kernel(q_ref, k_ref, v_ref, ga_ref, gr_ref, wg_ref, bg_ref, norm_ref, o_ref, st_ref):
    @pl.when(pl.program_id(2) == 0)
    def _():
        st_ref[...] = jnp.zeros_like(st_ref)

    gate_in = jnp.dot(ga_ref[...].astype(BF16), wg_ref[...], preferred_element_type=F32) + bg_ref[...]
    la = _log_sigmoid(gate_in) / GLA_TAU
    q = q_ref[...].astype(F32) * (GLA_DK ** -0.5)
    o = _gated_linear_tile(q, k_ref[...].astype(F32), v_ref[...], la, st_ref)
    o_ref[...] = _head_rms_gate(o, norm_ref[...], gr_ref[...]).astype(o_ref.dtype)


def _gla(hb, hf, w_gate, b_gate, norm, batch, seq):
    m = hb.shape[0]
    tt = min(SEQ_TILE, seq)
    nt = seq // tt
    hp_n = GLA_HEADS // 2
    wg = jnp.zeros((LANES, GLA_HEADS * GLA_DK), BF16).at[:GLA_RANK].set(w_gate.astype(BF16))
    row = lambda b, h, i: b * nt + i
    return pl.pallas_call(
        _gla_kernel,
        name="gla",
        grid=(batch, hp_n, nt),
        in_specs=[pl.BlockSpec((tt, LANES), lambda b, h, i: (row(b, h, i), 8 + h)),
                  pl.BlockSpec((tt, LANES), lambda b, h, i: (row(b, h, i), 10 + h)),
                  pl.BlockSpec((tt, 2 * GLA_DV), lambda b, h, i: (row(b, h, i), 6 + h)),
                  pl.BlockSpec((tt, LANES), lambda b, h, i: (row(b, h, i), 4)),
                  pl.BlockSpec((tt, 2 * GLA_DV), lambda b, h, i: (row(b, h, i), h)),
                  pl.BlockSpec((LANES, LANES), lambda b, h, i: (0, h)),
                  pl.BlockSpec((1, LANES), lambda b, h, i: (0, h)),
                  pl.BlockSpec((1, 2 * GLA_DV), lambda b, h, i: (0, h))],
        out_specs=pl.BlockSpec((tt, 2 * GLA_DV), lambda b, h, i: (row(b, h, i), h)),
        out_shape=jax.ShapeDtypeStruct((m, GLA_HEADS * GLA_DV), BF16),
        scratch_shapes=[pltpu.VMEM((2 * GLA_DV, LANES), F32)],
        compiler_params=_cparams(("parallel", "parallel", "arbitrary")),
    )(hb, hb, hb, hf, hf, wg, b_gate.reshape(1, -1), norm.reshape(1, -1))


def _hgrn_kernel(q_ref, f_ref, v_ref, g_ref, lb_ref, norm_ref, o_ref, st_ref):
    @pl.when(pl.program_id(2) == 0)
    def _():
        st_ref[...] = jnp.zeros_like(st_ref)

    lb = lb_ref[...]
    f = lb + (1.0 - lb) * _sigmoid(f_ref[...])
    o = _gated_linear_tile(q_ref[...], 1.0 - f, v_ref[...].astype(BF16), jnp.log(f), st_ref)
    o_ref[...] = _head_rms_gate(o, norm_ref[...], g_ref[...]).astype(o_ref.dtype)


def _hgrn(h, lb, norm, batch, seq):
    m = h.shape[0]
    tt = min(SEQ_TILE, seq)
    nt = seq // tt
    row = lambda b, hh, i: b * nt + i
    return pl.pallas_call(
        _hgrn_kernel,
        name="hgrn",
        grid=(batch, HGRN_HEADS // 2, nt),
        in_specs=[pl.BlockSpec((tt, LANES), lambda b, hh, i: (row(b, hh, i), hh)),
                  pl.BlockSpec((tt, LANES), lambda b, hh, i: (row(b, hh, i), 2 + hh)),
                  pl.BlockSpec((tt, 2 * GLA_DV), lambda b, hh, i: (row(b, hh, i), 2 + hh)),
                  pl.BlockSpec((tt, 2 * GLA_DV), lambda b, hh, i: (row(b, hh, i), 4 + hh)),
                  pl.BlockSpec((1, LANES), lambda b, hh, i: (0, hh)),
                  pl.BlockSpec((1, 2 * GLA_DV), lambda b, hh, i: (0, hh))],
        out_specs=pl.BlockSpec((tt, 2 * GLA_DV), lambda b, hh, i: (row(b, hh, i), hh)),
        out_shape=jax.ShapeDtypeStruct((m, HGRN_HEADS * GLA_DV), BF16),
        scratch_shapes=[pltpu.VMEM((2 * GLA_DV, LANES), F32)],
        compiler_params=_cparams(("parallel", "parallel", "arbitrary")),
    )(h, h, h, h, lb.reshape(1, -1), norm.reshape(1, -1))


def _rotate(x, cos, sin_signed):
    lane = lax.broadcasted_iota(jnp.int32, x.shape, 1)
    half = GLA_DK // 2
    swapped = jnp.where(lane % GLA_DK < half, pltpu.roll(x, LANES - half, 1), pltpu.roll(x, half, 1))
    return x * cos + swapped * sin_signed


def _ret_kernel(q_ref, k_ref, v_ref, g_ref, cos_ref, sin_ref, dmask_ref, zeta_ref, xi_ref, gch_ref,
                norm_ref, o_ref, st_ref):
    @pl.when(pl.program_id(2) == 0)
    def _():
        st_ref[...] = jnp.zeros_like(st_ref)

    t = q_ref.shape[0]
    cos, sin = cos_ref[...], sin_ref[...]
    q = _rotate(q_ref[...], cos, sin)
    k = _rotate(k_ref[...], cos, sin) * (GLA_DK ** -0.5)
    v = v_ref[...].astype(BF16)
    nc = t // CHUNK
    zeta = jnp.broadcast_to(zeta_ref[0][None], (nc, CHUNK, LANES)).reshape(t, LANES)
    xi = jnp.broadcast_to(xi_ref[0][None], (nc, CHUNK, LANES)).reshape(t, LANES)
    q_b, k_b = q.astype(BF16), k.astype(BF16)
    q_in = (q * xi).astype(BF16)
    k_out = (k * zeta).astype(BF16)
    dmask = dmask_ref[0]
    g_chunk = gch_ref[0]
    e = lax.broadcasted_iota(jnp.int32, st_ref.shape, 0)
    d = lax.broadcasted_iota(jnp.int32, st_ref.shape, 1)
    same_head = (e < GLA_DV) == (d < GLA_DK)
    outs = []
    for c in range(nc):
        rows = slice(c * CHUNK, (c + 1) * CHUNK)
        a = lax.dot_general(q_b[rows], _stack_heads(k_b[rows], GLA_DK), NT_DIMS,
                            preferred_element_type=F32) * dmask
        vc = v[rows]
        o = jnp.dot(a.astype(BF16), _stack_heads(vc, GLA_DV), preferred_element_type=F32)
        st = st_ref[...]
        o = o + lax.dot_general(q_in[rows], st.astype(BF16), NT_DIMS, preferred_element_type=F32)
        upd = lax.dot_general(vc, k_out[rows], TN_DIMS, preferred_element_type=F32)
        st_ref[...] = st * g_chunk + jnp.where(same_head, upd, 0.0)
        outs.append(o)
    o = jnp.concatenate(outs, axis=0)
    parts = []
    for hd in range(2):
        oh = o[:, hd * GLA_DV:(hd + 1) * GLA_DV]
        oh = oh - jnp.mean(oh, axis=-1, keepdims=True)
        parts.append(oh * lax.rsqrt(jnp.mean(oh * oh, axis=-1, keepdims=True) + HEAD_NORM_EPS))
    o_ref[...] = (jnp.concatenate(parts, axis=1) * norm_ref[...] * _silu(g_ref[...])).astype(o_ref.dtype)


def _retention_tables(seq):
    half = GLA_DK // 2
    inv = np.float32(ROPE_BASE) ** (-np.arange(half, dtype=np.float32) / np.float32(half))
    ang = (np.arange(seq, dtype=np.float32)[:, None] * inv[None, :]).astype(np.float64)
    cos, sin = np.cos(ang).astype(np.float32), np.sin(ang).astype(np.float32)
    cos_t = jnp.tile(jnp.asarray(cos), (1, 4))
    sin_t = jnp.tile(jnp.asarray(np.concatenate([-sin, sin], axis=1)), (1, 2))
    log_g = jnp.log1p(-jnp.exp2(-5.0 - jnp.arange(RET_HEADS, dtype=F32)))
    pos = jnp.arange(CHUNK, dtype=F32)
    rel = pos[:, None] - pos[None, :]
    dmask = jnp.where(rel >= 0, jnp.exp(jnp.maximum(rel, 0.0)[None] * log_g[:, None, None]), 0.0)
    zeta = jnp.exp((CHUNK - 1 - pos)[None, :] * log_g[:, None])
    xi = jnp.exp((pos + 1.0)[None, :] * log_g[:, None])
    g_chunk = jnp.exp(CHUNK * log_g)
    hp_n = RET_HEADS // 2
    pair = lambda x: x.reshape(hp_n, 2, *x.shape[1:])
    dmask_p = jnp.concatenate([pair(dmask)[:, 0], pair(dmask)[:, 1]], axis=-1)
    lanes = lambda x: jnp.repeat(pair(x), GLA_DK, axis=1)
    zeta_p = jnp.swapaxes(lanes(zeta), 1, 2)
    xi_p = jnp.swapaxes(lanes(xi), 1, 2)
    gch_p = lanes(g_chunk)[:, None, :]
    return cos_t, sin_t, dmask_p, zeta_p, xi_p, gch_p


def _retention(h, norm, batch, seq):
    m = h.shape[0]
    tt = min(SEQ_TILE, seq)
    nt = seq // tt
    cos_t, sin_t, dmask, zeta, xi, gch = _retention_tables(seq)
    row = lambda b, hh, i: b * nt + i
    const3 = lambda shape: pl.BlockSpec((1,) + shape, lambda b, hh, i: (hh, 0, 0))
    return pl.pallas_call(
        _ret_kernel,
        name="retention",
        grid=(batch, RET_HEADS // 2, nt),
        in_specs=[pl.BlockSpec((tt, LANES), lambda b, hh, i: (row(b, hh, i), 12 + hh)),
                  pl.BlockSpec((tt, LANES), lambda b, hh, i: (row(b, hh, i), 14 + hh)),
                  pl.BlockSpec((tt, 2 * GLA_DV), lambda b, hh, i: (row(b, hh, i), 8 + hh)),
                  pl.BlockSpec((tt, 2 * GLA_DV), lambda b, hh, i: (row(b, hh, i), 10 + hh)),
                  pl.BlockSpec((tt, LANES), lambda b, hh, i: (i, 0)),
                  pl.BlockSpec((tt, LANES), lambda b, hh, i: (i, 0)),
                  const3((CHUNK, 2 * CHUNK)), const3((CHUNK, LANES)), const3((CHUNK, LANES)),
                  const3((1, LANES)),
                  pl.BlockSpec((1, 2 * GLA_DV), lambda b, hh, i: (0, hh))],
        out_specs=pl.BlockSpec((tt, 2 * GLA_DV), lambda b, hh, i: (row(b, hh, i), hh)),
        out_shape=jax.ShapeDtypeStruct((m, RET_HEADS * GLA_DV), BF16),
        scratch_shapes=[pltpu.VMEM((2 * GLA_DV, LANES), F32)],
        compiler_params=_cparams(("parallel", "parallel", "arbitrary")),
    )(h, h, h, h, cos_t, sin_t, dmask, zeta, xi, gch, norm.reshape(1, -1))


def _route(xn, w, b):
    logits = jnp.dot(xn.astype(BF16), w, preferred_element_type=F32) + b
    lane = lax.broadcasted_iota(jnp.int32, logits.shape, 1)
    lane_f = lane.astype(F32)
    logits = jnp.where(lane < N_EXPERTS, logits, -jnp.inf)
    m1 = jnp.max(logits, axis=1, keepdims=True)
    i1 = jnp.min(jnp.where(logits == m1, lane_f, float(LANES)), axis=1, keepdims=True)
    rest = jnp.where(lane_f == i1, -jnp.inf, logits)
    m2 = jnp.max(rest, axis=1, keepdims=True)
    i2 = jnp.min(jnp.where(rest == m2, lane_f, float(LANES)), axis=1, keepdims=True)
    z = jnp.exp(m2 - m1)
    g1 = 1.0 / (1.0 + z)
    experts = jnp.where(lane == 0, i1, jnp.where(lane == 1, i2, 0.0)).astype(jnp.int32)
    gates = jnp.where(lane == 0, g1, jnp.where(lane == 1, z * g1, 0.0))
    return experts, gates


def _outproj_kernel(oa_ref, ob_ref, x_ref, wa_ref, wb_ref, g_ref, b_ref, *rest, alpha):
    y = jnp.dot(oa_ref[...], wa_ref[...], preferred_element_type=F32)
    y = y + jnp.dot(ob_ref[...], wb_ref[...], preferred_element_type=F32)
    xn = _layer_norm(alpha * x_ref[...] + y, g_ref[...], b_ref[...])
    if len(rest) == 1:
        rest[0][...] = xn
    else:
        wr_ref, br_ref, o_ref, e_ref, gt_ref = rest
        o_ref[...] = xn
        e_ref[...], gt_ref[...] = _route(xn, wr_ref[...], br_ref[...])


def _outproj_ln(oa, ob, x, w_out, g, b, alpha, router=None):
    m, d = x.shape
    tm = min(ROW_TILE, m)
    wa = w_out[:oa.shape[1]].astype(BF16)
    wb = w_out[oa.shape[1]:].astype(BF16)
    full = lambda a: pl.BlockSpec(a.shape, lambda i: (0, 0))
    rows = lambda a: pl.BlockSpec((tm, a.shape[1]), lambda i: (i, 0))
    args = [oa, ob, x, wa, wb, g.reshape(1, -1), b.reshape(1, -1)]
    in_specs = [rows(oa), rows(ob), rows(x)] + [full(a) for a in args[3:]]
    out_specs = [rows(x)]
    out_shape = [jax.ShapeDtypeStruct((m, d), F32)]
    if router is not None:
        wr = jnp.zeros((d, LANES), F32).at[:, :N_EXPERTS].set(router[0]).astype(BF16)
        br = jnp.zeros((1, LANES), F32).at[0, :N_EXPERTS].set(router[1])
        args += [wr, br]
        in_specs += [full(wr), full(br)]
        lanes = pl.BlockSpec((tm, LANES), lambda i: (i, 0))
        out_specs += [lanes, lanes]
        out_shape += [jax.ShapeDtypeStruct((m, LANES), jnp.int32), jax.ShapeDtypeStruct((m, LANES), F32)]
    out = pl.pallas_call(
        functools.partial(_outproj_kernel, alpha=alpha),
        name="out_proj_ln",
        grid=(m // tm,),
        in_specs=in_specs,
        out_specs=out_specs,
        out_shape=out_shape,
        compiler_params=_cparams(("parallel",)),
    )(*args)
    if router is None:
        return out[0]
    return out[0], out[1][:, :TOP_K], out[2][:, :TOP_K]


def _ln_ple(x, ffn, p, g, b, wg, wp, alpha):
    y = _layer_norm(alpha * x + ffn, g, b)
    gate = _sigmoid(jnp.dot(y.astype(BF16), wg, preferred_element_type=F32))
    return y + gate * jnp.dot(p.astype(BF16), wp, preferred_element_type=F32)


def _swiglu_chunks(xb_ref, w1, w3, w2):
    n = xb_ref.shape[0] // SWIGLU_ROWS
    h = []
    for c in range(n):
        xc = xb_ref[c * SWIGLU_ROWS:(c + 1) * SWIGLU_ROWS, :]
        h.append((jnp.dot(xc, w1, preferred_element_type=F32), jnp.dot(xc, w3, preferred_element_type=F32)))
    return [jnp.dot((_silu(h1) * h3).astype(BF16), w2, preferred_element_type=F32) for h1, h3 in h]


def _accumulate_steps(parts, acc_ref, finish, n_steps):
    assert n_steps > 1
    f = pl.program_id(1)
    last = n_steps - 1
    chunk = lambda c: slice(c * SWIGLU_ROWS, (c + 1) * SWIGLU_ROWS)

    @pl.when(f == 0)
    def _():
        for c, part in enumerate(parts):
            acc_ref[chunk(c), :] = part

    @pl.when((f > 0) & (f < last))
    def _():
        for c, part in enumerate(parts):
            acc_ref[chunk(c), :] += part

    @pl.when(f == last)
    def _():
        finish(jnp.concatenate([acc_ref[chunk(c), :] + part for c, part in enumerate(parts)], axis=0))


def _swiglu_kernel(x_ref, p_ref, w1_ref, w3_ref, w2_ref, g_ref, b_ref, wg_ref, wp_ref, o_ref,
                   xb_ref, acc_ref, *, alpha, n_steps):
    @pl.when(pl.program_id(1) == 0)
    def _():
        xb_ref[...] = x_ref[...].astype(BF16)

    def finish(ffn):
        o_ref[...] = _ln_ple(x_ref[...], ffn, p_ref[...], g_ref[...], b_ref[...], wg_ref[...], wp_ref[...],
                             alpha)

    _accumulate_steps(_swiglu_chunks(xb_ref, w1_ref[...], w3_ref[...], w2_ref[...]), acc_ref, finish,
                      n_steps)


def _swiglu_ln_ple(x, p, w1, w3, w2, g, b, wg, wp, alpha):
    m, d = x.shape
    tm = min(ROW_TILE, m)
    pad = FFN_F_PAD - w1.shape[1]
    w1p = jnp.pad(w1.astype(BF16), ((0, 0), (0, pad)))
    w3p = jnp.pad(w3.astype(BF16), ((0, 0), (0, pad)))
    w2p = jnp.pad(w2.astype(BF16), ((0, pad), (0, 0)))
    nf = FFN_F_PAD // FFN_F_TILE
    g2, b2 = g.reshape(1, -1), b.reshape(1, -1)
    wgb, wpb = wg.astype(BF16), wp.astype(BF16)
    full = lambda a: pl.BlockSpec(a.shape, lambda i, f: (0, 0))
    rows = lambda a: pl.BlockSpec((tm, a.shape[1]), lambda i, f: (i, 0))
    return pl.pallas_call(
        functools.partial(_swiglu_kernel, alpha=alpha, n_steps=nf),
        name="swiglu_ln_ple",
        grid=(m // tm, nf),
        in_specs=[rows(x), rows(p),
                  pl.BlockSpec((d, FFN_F_TILE), lambda i, f: (0, f)),
                  pl.BlockSpec((d, FFN_F_TILE), lambda i, f: (0, f)),
                  pl.BlockSpec((FFN_F_TILE, d), lambda i, f: (f, 0)),
                  full(g2), full(b2), full(wgb), full(wpb)],
        out_specs=rows(x),
        out_shape=jax.ShapeDtypeStruct((m, d), F32),
        scratch_shapes=[pltpu.VMEM((tm, d), BF16), pltpu.VMEM((tm, d), F32)],
        compiler_params=_cparams(("parallel", "arbitrary")),
    )(x, p, w1p, w3p, w2p, g2, b2, wgb, wpb)


def _gather_rows(x, idx):
    k = idx.shape[0]
    d = x.shape[1]
    idx_tiles = jnp.pad(idx.reshape(k // GATHER_ROWS, GATHER_ROWS), ((0, 0), (0, GATHER_WINDOW - GATHER_ROWS)))
    mesh = plsc.VectorSubcoreMesh(core_axis_name="core", subcore_axis_name="subcore",
                                  num_cores=SC_CORES, num_subcores=SC_SUBCORES)

    @pl.kernel(out_type=jax.ShapeDtypeStruct((k, d), x.dtype), mesh=mesh, scratch_types=[],
               name="gather_rows")
    def gather(x_hbm, i_hbm, o_hbm):
        def body(i_vmem, o_vmem):
            pltpu.sync_copy(x_hbm.at[i_vmem.at[0, pl.ds(0, GATHER_ROWS)]], o_vmem)

        pltpu.emit_pipeline(
            body,
            grid=(k // GATHER_ROWS,),
            in_specs=[pl.BlockSpec((1, GATHER_WINDOW), lambda i: (i, 0))],
            out_specs=[pl.BlockSpec((GATHER_ROWS, d), lambda i: (i, 0))],
            core_axis_name=("core", "subcore"),
            dimension_semantics=(pltpu.PARALLEL,),
        )(i_hbm, o_hbm)

    return gather(x, idx_tiles)


def _experts_kernel(be_ref, x_ref, w1_ref, w3_ref, w2_ref, o_ref, xb_ref, acc_ref, *, n_steps):
    @pl.when(pl.program_id(1) == 0)
    def _():
        xb_ref[...] = x_ref[...].astype(BF16)

    def finish(y):
        o_ref[...] = y.astype(o_ref.dtype)

    _accumulate_steps(_swiglu_chunks(xb_ref, w1_ref[0], w3_ref[0], w2_ref[0]), acc_ref, finish, n_steps)


def _experts(xg, blk_e, w1, w3, w2):
    r, d = xg.shape
    fe = w1.shape[2]
    nf = fe // MOE_F_TILE
    grid_spec = pltpu.PrefetchScalarGridSpec(
        num_scalar_prefetch=1,
        grid=(r // MOE_ROW_TILE, nf),
        in_specs=[pl.BlockSpec((MOE_ROW_TILE, d), lambda i, f, be: (i, 0)),
                  pl.BlockSpec((1, d, MOE_F_TILE), lambda i, f, be: (be[i], 0, f)),
                  pl.BlockSpec((1, d, MOE_F_TILE), lambda i, f, be: (be[i], 0, f)),
                  pl.BlockSpec((1, MOE_F_TILE, d), lambda i, f, be: (be[i], f, 0))],
        out_specs=pl.BlockSpec((MOE_ROW_TILE, d), lambda i, f, be: (i, 0)),
        scratch_shapes=[pltpu.VMEM((MOE_ROW_TILE, d), BF16), pltpu.VMEM((MOE_ROW_TILE, d), F32)],
    )
    return pl.pallas_call(
        functools.partial(_experts_kernel, n_steps=nf),
        name="experts",
        grid_spec=grid_spec,
        out_shape=jax.ShapeDtypeStruct((r, d), F32),
        compiler_params=_cparams(("parallel", "arbitrary")),
    )(blk_e, xg, w1, w3, w2)


def _combine_kernel(x_ref, y0_ref, y1_ref, gt_ref, p_ref, g_ref, b_ref, wg_ref, wp_ref, o_ref, *, alpha):
    gt = gt_ref[...]
    ffn = y0_ref[...].astype(F32) * gt[:, 0:1] + y1_ref[...].astype(F32) * gt[:, 1:2]
    o_ref[...] = _ln_ple(x_ref[...], ffn, p_ref[...], g_ref[...], b_ref[...], wg_ref[...], wp_ref[...],
                         alpha)


def _combine_ln_ple(x, y, gates, p, g, b, wg, wp, alpha):
    m, d = x.shape
    tm = min(ROW_TILE, m)
    g2, b2 = g.reshape(1, -1), b.reshape(1, -1)
    wgb, wpb = wg.astype(BF16), wp.astype(BF16)
    gt = jnp.zeros((m, LANES), F32).at[:, :TOP_K].set(gates)
    full = lambda a: pl.BlockSpec(a.shape, lambda i: (0, 0))
    rows = lambda a: pl.BlockSpec((tm, a.shape[1]), lambda i: (i, 0))
    return pl.pallas_call(
        functools.partial(_combine_kernel, alpha=alpha),
        name="combine_ln_ple",
        grid=(m // tm,),
        in_specs=[rows(x), rows(y), pl.BlockSpec((tm, d), lambda i: (i + m // tm, 0)), rows(gt), rows(p),
                  full(g2), full(b2), full(wgb), full(wpb)],
        out_specs=rows(x),
        out_shape=jax.ShapeDtypeStruct((m, d), F32),
        compiler_params=_cparams(("parallel",)),
    )(x, y, y, gt, p, g2, b2, wgb, wpb)


def _moe_ln_ple(x, top_e, gates, p, w1, w3, w2, g, b, wg, wp, alpha):
    m, d = x.shape
    n_asg = m * TOP_K
    e_flat = top_e.reshape(-1)
    onehot = (jnp.arange(N_EXPERTS, dtype=jnp.int32)[:, None] == e_flat[None, :]).astype(jnp.int32)
    csum = jnp.cumsum(onehot, axis=1)
    rank = jnp.sum(csum * onehot, axis=0) - 1
    counts = csum[:, -1]
    padded = (counts + MOE_ROW_TILE - 1) // MOE_ROW_TILE * MOE_ROW_TILE
    p_ends = jnp.cumsum(padded)
    p_starts = p_ends - padded
    pos = (jnp.sum(onehot * p_starts[:, None], axis=0) + rank).reshape(m, TOP_K)
    n_rows = (n_asg // MOE_ROW_TILE + N_EXPERTS) * MOE_ROW_TILE
    blk_start = jnp.arange(n_rows // MOE_ROW_TILE, dtype=jnp.int32) * MOE_ROW_TILE
    blk_e = jnp.minimum(jnp.sum((blk_start[None, :] >= p_ends[:, None]).astype(jnp.int32), axis=0),
                        N_EXPERTS - 1)
    n_pad = n_rows - n_asg
    pad_i = jnp.arange(n_pad, dtype=jnp.int32)
    pad_e = jnp.sum((pad_i[None, :] >= jnp.cumsum(padded - counts)[:, None]).astype(jnp.int32), axis=0)
    asg_i = jnp.arange(n_asg, dtype=jnp.int32)
    keys = jnp.concatenate([e_flat * (2 * n_rows) + asg_i, pad_e * (2 * n_rows) + n_rows + pad_i])
    toks = jnp.concatenate([asg_i // TOP_K, jnp.zeros((n_pad,), jnp.int32)])
    _, tok_buf = lax.sort((keys, toks), num_keys=1)
    xg = _gather_rows(x, tok_buf)
    y = _experts(xg, blk_e, w1.astype(BF16), w3.astype(BF16), w2.astype(BF16))
    y_tok = _gather_rows(y, pos.T.reshape(-1))
    return _combine_ln_ple(x, y_tok, gates, p, g, b, wg, wp, alpha)


def kernel(x, p, rel_bias, even_w_in, gla_w_gate, gla_b_gate, gla_norm, even_w_out, odd_w_in, hgrn_gamma,
           hgrn_norm, ret_norm, odd_w_out, ln_mix_g, ln_mix_b, ln_ffn_g, ln_ffn_b, ffn_w1, ffn_w3, ffn_w2,
           router_w, router_b, expert_w1, expert_w3, expert_w2, ple_w_gate, ple_w_proj):
    batch, seq, d = x.shape
    depth = p.shape[0]
    m = batch * seq
    alpha = (2.0 * depth) ** 0.25
    lb_p = jax.nn.softmax(hgrn_gamma.astype(F32), axis=0)
    lb_all = jnp.cumsum(lb_p, axis=0) - lb_p[0]
    xf = x.reshape(m, d)
    pf = p.reshape(depth, m, PLE_DIM)
    n_mq = 3 * MOBA_HEADS * MOBA_DH + 2 * GLA_HEADS * GLA_DK + GLA_HEADS * GLA_DV
    for i in range(depth):
        j = i // 2
        if i % 2 == 0:
            w_in = even_w_in[j]
            n_qk = 2 * MOBA_HEADS * MOBA_DH
            n_v = MOBA_HEADS * MOBA_DH
            w_q = w_in[:, :n_v] * (MOBA_DH ** -0.5 * LOG2E)
            w_b = jnp.concatenate([w_q, w_in[:, n_v:n_qk], w_in[:, n_qk + n_v:n_mq]], axis=1).astype(BF16)
            w_vt = w_in[:, n_qk:n_qk + n_v].T.astype(BF16)
            w_f = jnp.concatenate([w_in[:, n_mq + GLA_RANK:], w_in[:, n_mq:n_mq + GLA_RANK],
                                   jnp.zeros((d, LANES - GLA_RANK), F32)], axis=1).astype(BF16)
            hb, hf, vt = _proj_even(xf, w_b, w_f, w_vt)
            o_a = _moba(hb, vt, rel_bias, batch, seq)
            o_b = _gla(hb, hf, gla_w_gate[j], gla_b_gate[j], gla_norm[j], batch, seq)
            xf = _outproj_ln(o_a, o_b, xf, even_w_out[j], ln_mix_g[i], ln_mix_b[i], alpha)
            xf = _swiglu_ln_ple(xf, pf[i], ffn_w1[j], ffn_w3[j], ffn_w2[j], ln_ffn_g[i], ln_ffn_b[i],
                                ple_w_gate[i], ple_w_proj[i], alpha)
        else:
            h = _proj(xf, odd_w_in[j].astype(BF16), F32)
            o_c = _hgrn(h, lb_all[i], hgrn_norm[j], batch, seq)
            o_d = _retention(h, ret_norm[j], batch, seq)
            xf, top_e, gates = _outproj_ln(o_c, o_d, xf, odd_w_out[j], ln_mix_g[i], ln_mix_b[i], alpha,
                                           router=(router_w[j], router_b[j]))
            xf = _moe_ln_ple(xf, top_e, gates, pf[i], expert_w1[j], expert_w3[j], expert_w2[j],
                             ln_ffn_g[i], ln_ffn_b[i], ple_w_gate[i], ple_w_proj[i], alpha)
    return xf.reshape(batch, seq, d)
```
